```python
import math
import jax, jax.numpy as jnp
from jax import lax
import numpy as np

D_MODEL = 2048
BATCH = 4
SEQ = 4096
DEPTH = 4

GRID_W = 64
CTX_LEN = 256

f32 = jnp.float32
NORM_EPS = 1e-6
ROPE_BASE = 10000.0
ROT_DIM = 64
Q_BLOCK = 128
GROUP_WIDTH = D_MODEL // 4
MIX_WIDTH = 4 * GROUP_WIDTH

DA_HEAD_DIM = ROT_DIM
DA_V_DIM = 2 * DA_HEAD_DIM
DA_HEADS = GROUP_WIDTH // DA_V_DIM
DA_QK_WIDTH = 2 * DA_HEADS * DA_HEAD_DIM
DA_WIDTH = DA_HEADS * DA_V_DIM
DA_SCALE = DA_HEAD_DIM ** -0.5

S5_WIDTH = GROUP_WIDTH
S5_GROUP = 16
S5_GROUPS = S5_WIDTH // S5_GROUP
S5_STATE = 64
S5_DT_MIN = 0.001
S5_DT_MAX = 0.1

MLA_NOPE_DIM = 128
MLA_ROPE_DIM = ROT_DIM
MLA_V_DIM = 128
MLA_HEADS = GROUP_WIDTH // MLA_V_DIM
MLA_Q_RANK = 512
MLA_KV_RANK = 256
MLA_IN_WIDTH = MLA_Q_RANK + MLA_KV_RANK + MLA_ROPE_DIM
MLA_WIDTH = MLA_HEADS * MLA_V_DIM
MLA_SCALE = (MLA_NOPE_DIM + MLA_ROPE_DIM) ** -0.5

CONV_CH = GROUP_WIDTH
CONV_K = 31

IN_SPLITS = (DA_QK_WIDTH,
             2 * DA_QK_WIDTH,
             2 * DA_QK_WIDTH + DA_WIDTH,
             2 * DA_QK_WIDTH + DA_WIDTH + S5_WIDTH,
             2 * DA_QK_WIDTH + DA_WIDTH + S5_WIDTH + MLA_IN_WIDTH)
IN_WIDTH = IN_SPLITS[-1] + 2 * CONV_CH

D_FF = -(-8 * D_MODEL // (3 * 256)) * 256

kernel_name = 'hybrid_parallel_groups_diffusion_block'


def _rms(x, g):
    xf = x.astype(f32)
    y = xf * lax.rsqrt(jnp.mean(xf * xf, axis=-1, keepdims=True) + NORM_EPS)
    return (y * g.astype(f32)).astype(x.dtype)


def _layernorm(x, g, b):
    xf = x.astype(f32)
    mu = jnp.mean(xf, axis=-1, keepdims=True)
    var = jnp.mean(jnp.square(xf - mu), axis=-1, keepdims=True)
    y = (xf - mu) * lax.rsqrt(var + NORM_EPS) * g.astype(f32) + b.astype(f32)
    return y.astype(x.dtype)


def _modulate(h, shift, scale):
    return h * (1.0 + scale) + shift


def _axial_rope_tables(n_tokens, rot_dim):
    n_rows = n_tokens // GRID_W
    row = jnp.repeat(jnp.arange(n_rows, dtype=f32), GRID_W)
    col = jnp.tile(jnp.arange(GRID_W, dtype=f32), n_rows)
    n_freq = rot_dim // 4
    inv = ROPE_BASE ** (-jnp.arange(n_freq, dtype=f32) / n_freq)
    ang = jnp.concatenate([row[:, None] * inv, col[:, None] * inv], axis=-1)
    return jnp.cos(ang), jnp.sin(ang)


def _rope(x, cos, sin):
    half = x.shape[-1] // 2
    x1, x2 = x[..., :half], x[..., half:]
    c = cos[None, :, None, :].astype(x.dtype)
    s = sin[None, :, None, :].astype(x.dtype)
    return jnp.concatenate([x1 * c - x2 * s, x1 * s + x2 * c], axis=-1)


def _attend(q, k, v, scale):
    bsz, lq, nh, dq = q.shape
    nb = lq // Q_BLOCK
    qb = q.reshape(bsz, nb, Q_BLOCK, nh, dq).transpose(1, 0, 2, 3, 4)

    def block(qblk):
        s = jnp.einsum('bqhd,bkhd->bhqk', qblk, k).astype(f32) * scale
        p = jax.nn.softmax(s, axis=-1).astype(v.dtype)
        return jnp.einsum('bhqk,bkhe->bqhe', p, v)

    o = lax.map(block, qb)
    return o.transpose(1, 0, 2, 3, 4).reshape(bsz, lq, nh, v.shape[-1])


def _da_heads(q, k, v):
    bsz, n, _ = q.shape
    q = q.reshape(bsz, n, DA_HEADS, 2, DA_HEAD_DIM)
    k = k.reshape(bsz, n, DA_HEADS, 2, DA_HEAD_DIM)
    q = jnp.concatenate([q[:, :, :, 0], q[:, :, :, 1]], axis=2)
    k = jnp.concatenate([k[:, :, :, 0], k[:, :, :, 1]], axis=2)
    v = v.reshape(bsz, n, DA_HEADS, DA_V_DIM)
    return q, k, jnp.concatenate([v, v], axis=2)


def _diff_combine(o, lam, subln_g, lambda_init):
    bsz, n = o.shape[0], o.shape[1]
    d = o[:, :, :DA_HEADS] - lam.astype(o.dtype) * o[:, :, DA_HEADS:]
    return (_rms(d, subln_g) * (1.0 - lambda_init)).reshape(bsz, n, DA_WIDTH)


def _mla_qkv(pm, p, rope):
    bsz, n, _ = pm.shape
    cq, ckv, kr = jnp.split(pm, [MLA_Q_RANK, MLA_Q_RANK + MLA_KV_RANK], axis=-1)
    q = (_rms(cq, p['mla_q_norm']) @ p['mla_w_uq']).reshape(bsz, n, MLA_HEADS, MLA_NOPE_DIM + MLA_ROPE_DIM)
    kv = (_rms(ckv, p['mla_kv_norm']) @ p['mla_w_ukv']).reshape(bsz, n, MLA_HEADS, MLA_NOPE_DIM + MLA_V_DIM)
    q_nope, q_rope = q[..., :MLA_NOPE_DIM], q[..., MLA_NOPE_DIM:]
    k_nope, v = kv[..., :MLA_NOPE_DIM], kv[..., MLA_NOPE_DIM:]
    kr = kr[:, :, None, :]
    if rope is not None:
        q_rope = _rope(q_rope, rope[0], rope[1])
        kr = _rope(kr, rope[0], rope[1])
    q = jnp.concatenate([q_nope, q_rope], axis=-1)
    k = jnp.concatenate([k_nope, jnp.broadcast_to(kr, (bsz, n, MLA_HEADS, MLA_ROPE_DIM))], axis=-1)
    return q, k, v


def _cmul(ar, ai, br, bi):
    return ar * br - ai * bi, ar * bi + ai * br


def _scan_combine(e1, e2):
    a1r, a1i, b1r, b1i = e1
    a2r, a2i, b2r, b2i = e2
    ar, ai = _cmul(a2r, a2i, a1r, a1i)
    br, bi = _cmul(a2r, a2i, b1r, b1i)
    return ar, ai, br + b2r, bi + b2i


def _s5_discretise(lam_re, lam_im, log_step, b_re, b_im):
    lam_re = lam_re.astype(f32)
    lam_im = lam_im.astype(f32)
    step = jnp.exp(log_step.astype(f32))[:, None]
    mag = jnp.exp(lam_re * step)
    abar_re = mag * jnp.cos(lam_im * step)
    abar_im = mag * jnp.sin(lam_im * step)
    den = lam_re * lam_re + lam_im * lam_im
    f_re = ((abar_re - 1.0) * lam_re + abar_im * lam_im) / den
    f_im = (abar_im * lam_re - (abar_re - 1.0) * lam_im) / den
    bbar_re, bbar_im = _cmul(f_re[..., None], f_im[..., None], b_re.astype(f32), b_im.astype(f32))
    return abar_re, abar_im, bbar_re, bbar_im


def _s5_scan(u, abar_re, abar_im, bbar_re, bbar_im, h0_re, h0_im, reverse):
    if reverse:
        u = jnp.flip(u, axis=1)
    bu_re = jnp.einsum('blgh,gph->blgp', u, bbar_re)
    bu_im = jnp.einsum('blgh,gph->blgp', u, bbar_im)
    i_re, i_im = _cmul(abar_re, abar_im, h0_re, h0_im)
    bu_re = bu_re.at[:, 0].add(i_re)
    bu_im = bu_im.at[:, 0].add(i_im)
    a_re = jnp.broadcast_to(abar_re, bu_re.shape)
    a_im = jnp.broadcast_to(abar_im, bu_im.shape)
    _, _, s_re, s_im = lax.associative_scan(_scan_combine, (a_re, a_im, bu_re, bu_im), axis=1)
    fin_re, fin_im = s_re[:, -1], s_im[:, -1]
    if reverse:
        s_re = jnp.flip(s_re, axis=1)
        s_im = jnp.flip(s_im, axis=1)
    return s_re, s_im, fin_re, fin_im


def _s5_input(ps):
    bsz, n, _ = ps.shape
    return ps.astype(f32).reshape(bsz, n, S5_GROUPS, S5_GROUP)


def _s5_readout(u, states, p, out_dtype):
    bsz, n = u.shape[0], u.shape[1]
    y = p['s5_d'].astype(f32).reshape(S5_GROUPS, S5_GROUP) * u
    for d in range(2):
        s_re, s_im = states[d][0], states[d][1]
        y = y + jnp.einsum('blgp,ghp->blgh', s_re, p['s5_c_re'][d].astype(f32)) \
              - jnp.einsum('blgp,ghp->blgh', s_im, p['s5_c_im'][d].astype(f32))
    g = jax.nn.gelu(y.reshape(bsz, n, S5_WIDTH))
    return (g * jax.nn.sigmoid(g @ p['s5_w_glu'].astype(f32))).astype(out_dtype)


def _conv_module(pcv, p):
    val, gate = pcv[..., :CONV_CH], pcv[..., CONV_CH:]
    u = val * jax.nn.sigmoid(gate)
    w = p['conv_w'].astype(u.dtype)[:, None, :]
    y = lax.conv_general_dilated(u, w, window_strides=(1,), padding=[(CONV_K // 2, CONV_K // 2)],
                                 dimension_numbers=('NWC', 'WIO', 'NWC'), feature_group_count=CONV_CH)
    y = y + p['conv_b']
    return jax.nn.silu(_layernorm(y, p['conv_ln_g'], p['conv_ln_b']))


def _mixer(hx, hc, p, rope, lambda_init, ctx_out):
    bsz, n, _ = hx.shape
    cos, sin = rope
    px = jnp.split(hx @ p['w_in'], IN_SPLITS, axis=-1)
    pc = jnp.split(hc @ p['w_in'], IN_SPLITS, axis=-1)

    qx, kx, vx = _da_heads(px[0], px[1], px[2])
    qx = _rope(qx, cos, sin)
    kx = _rope(kx, cos, sin)
    qc, kc, vc = _da_heads(pc[0], pc[1], pc[2])
    lam = (jnp.exp(jnp.sum(p['da_lam_q1'].astype(f32) * p['da_lam_k1'].astype(f32)))
           - jnp.exp(jnp.sum(p['da_lam_q2'].astype(f32) * p['da_lam_k2'].astype(f32))) + lambda_init)
    y_da_x = _diff_combine(_attend(qx, jnp.concatenate([kc, kx], 1), jnp.concatenate([vc, vx], 1), DA_SCALE),
                           lam, p['da_subln'], lambda_init)

    disc = [_s5_discretise(p['s5_lam_re'][d], p['s5_lam_im'][d], p['s5_log_step'][d],
                           p['s5_b_re'][d], p['s5_b_im'][d]) for d in range(2)]
    u_x = _s5_input(px[3])
    u_c = _s5_input(pc[3])
    zero = jnp.zeros((bsz, S5_GROUPS, S5_STATE), f32)
    st_c = [_s5_scan(u_c, *disc[d], zero, zero, d == 1) for d in range(2)]
    st_x = [_s5_scan(u_x, *disc[d], st_c[d][2], st_c[d][3], d == 1) for d in range(2)]
    y_s5_x = _s5_readout(u_x, st_x, p, hx.dtype)

    mq_x, mk_x, mv_x = _mla_qkv(px[4], p, rope)
    mq_c, mk_c, mv_c = _mla_qkv(pc[4], p, None)
    y_mla_x = _attend(mq_x, jnp.concatenate([mk_c, mk_x], 1), jnp.concatenate([mv_c, mv_x], 1),
                      MLA_SCALE).reshape(bsz, n, MLA_WIDTH)

    y_cv_x = _conv_module(px[5], p)

    yx = jnp.concatenate([y_da_x, y_s5_x, y_mla_x, y_cv_x], axis=-1) @ p['w_out']
    if not ctx_out:
        return yx, None

    m = hc.shape[1]
    y_da_c = _diff_combine(_attend(qc, kc, vc, DA_SCALE), lam, p['da_subln'], lambda_init)
    y_s5_c = _s5_readout(u_c, st_c, p, hc.dtype)
    y_mla_c = _attend(mq_c, mk_c, mv_c, MLA_SCALE).reshape(bsz, m, MLA_WIDTH)
    y_cv_c = _conv_module(pc[5], p)
    yc = jnp.concatenate([y_da_c, y_s5_c, y_mla_c, y_cv_c], axis=-1) @ p['w_out']
    return yx, yc


def _swiglu(h, w_in, w_out):
    gu = h @ w_in
    return (jax.nn.silu(gu[..., :D_FF]) * gu[..., D_FF:]) @ w_out


def setup_inputs(seed: int = 0) -> dict:
    key = jax.random.key(seed)
    ks = iter(jax.random.split(key, 48))

    def nrm(shape, scale):
        return jax.random.normal(next(ks), shape, f32) * scale

    def gain(shape):
        return 1.0 + nrm(shape, 0.02)

    L, G, P, H = DEPTH, S5_GROUPS, S5_STATE, S5_GROUP
    return {
        'x': nrm((BATCH, SEQ, D_MODEL), 1.0),
        'c': nrm((BATCH, D_MODEL), 1.0),
        'ctx': nrm((BATCH, CTX_LEN, D_MODEL), 1.0),
        'c_ctx': nrm((D_MODEL,), 1.0),
        'w_ada': nrm((L, D_MODEL, 6 * D_MODEL), 0.5 * D_MODEL ** -0.5),
        'b_ada': nrm((L, 6 * D_MODEL), 0.02),
        'norm_pre_mix': gain((L, D_MODEL)),
        'norm_post_mix': gain((L, D_MODEL)),
        'norm_pre_ffn': gain((L, D_MODEL)),
        'norm_post_ffn': gain((L, D_MODEL)),
        'w_in': nrm((L, D_MODEL, IN_WIDTH), D_MODEL ** -0.5),
        'w_out': nrm((L, MIX_WIDTH, D_MODEL), MIX_WIDTH ** -0.5),
        'da_lam_q1': nrm((L, DA_HEAD_DIM), 0.1),
        'da_lam_k1': nrm((L, DA_HEAD_DIM), 0.1),
        'da_lam_q2': nrm((L, DA_HEAD_DIM), 0.1),
        'da_lam_k2': nrm((L, DA_HEAD_DIM), 0.1),
        'da_subln': gain((L, DA_V_DIM)),
        's5_lam_re': -0.5 + nrm((L, 2, G, P), 0.01),
        's5_lam_im': jnp.arange(P, dtype=f32) * math.pi + nrm((L, 2, G, P), 0.01),
        's5_log_step': jax.random.uniform(next(ks), (L, 2, G), f32, math.log(S5_DT_MIN), math.log(S5_DT_MAX)),
        's5_b_re': nrm((L, 2, G, P, H), (2 * H) ** -0.5),
        's5_b_im': nrm((L, 2, G, P, H), (2 * H) ** -0.5),
        's5_c_re': nrm((L, 2, G, H, P), (2 * P) ** -0.5 * 4.0),
        's5_c_im': nrm((L, 2, G, H, P), (2 * P) ** -0.5 * 4.0),
        's5_d': nrm((L, S5_WIDTH), 0.5),
        's5_w_glu': nrm((L, S5_WIDTH, S5_WIDTH), S5_WIDTH ** -0.5),
        'mla_q_norm': gain((L, MLA_Q_RANK)),
        'mla_kv_norm': gain((L, MLA_KV_RANK)),
        'mla_w_uq': nrm((L, MLA_Q_RANK, MLA_HEADS * (MLA_NOPE_DIM + MLA_ROPE_DIM)), MLA_Q_RANK ** -0.5),
        'mla_w_ukv': nrm((L, MLA_KV_RANK, MLA_HEADS * (MLA_NOPE_DIM + MLA_V_DIM)), MLA_KV_RANK ** -0.5),
        'conv_w': nrm((L, CONV_K, CONV_CH), CONV_K ** -0.5),
        'conv_b': nrm((L, CONV_CH), 0.02),
        'conv_ln_g': gain((L, CONV_CH)),
        'conv_ln_b': nrm((L, CONV_CH), 0.02),
        'w_ffn_in': nrm((L, D_MODEL, 2 * D_FF), D_MODEL ** -0.5),
        'w_ffn_out': nrm((L, D_FF, D_MODEL), D_FF ** -0.5),
    }


def reference(x, c, ctx, c_ctx, w_ada, b_ada, norm_pre_mix, norm_post_mix, norm_pre_ffn, norm_post_ffn,
              w_in, w_out, da_lam_q1, da_lam_k1, da_lam_q2, da_lam_k2, da_subln,
              s5_lam_re, s5_lam_im, s5_log_step, s5_b_re, s5_b_im, s5_c_re, s5_c_im, s5_d, s5_w_glu,
              mla_q_norm, mla_kv_norm, mla_w_uq, mla_w_ukv,
              conv_w, conv_b, conv_ln_g, conv_ln_b, w_ffn_in, w_ffn_out):
    rope = _axial_rope_tables(x.shape[1], ROT_DIM)
    silu_c = jax.nn.silu(c)
    silu_cc = jax.nn.silu(c_ctx)
    for l in range(DEPTH):
        last = l == DEPTH - 1
        lambda_init = 0.8 - 0.6 * math.exp(-0.3 * l)
        p = {
            'w_in': w_in[l], 'w_out': w_out[l],
            'da_lam_q1': da_lam_q1[l], 'da_lam_k1': da_lam_k1[l],
            'da_lam_q2': da_lam_q2[l], 'da_lam_k2': da_lam_k2[l], 'da_subln': da_subln[l],
            's5_lam_re': s5_lam_re[l], 's5_lam_im': s5_lam_im[l], 's5_log_step': s5_log_step[l],
            's5_b_re': s5_b_re[l], 's5_b_im': s5_b_im[l], 's5_c_re': s5_c_re[l], 's5_c_im': s5_c_im[l],
            's5_d': s5_d[l], 's5_w_glu': s5_w_glu[l],
            'mla_q_norm': mla_q_norm[l], 'mla_kv_norm': mla_kv_norm[l],
            'mla_w_uq': mla_w_uq[l], 'mla_w_ukv': mla_w_ukv[l],
            'conv_w': conv_w[l], 'conv_b': conv_b[l], 'conv_ln_g': conv_ln_g[l], 'conv_ln_b': conv_ln_b[l],
        }
        mod_x = jnp.split((silu_c @ w_ada[l] + b_ada[l])[:, None, :], 6, axis=-1)
        mod_c = jnp.split(silu_cc @ w_ada[l] + b_ada[l], 6, axis=-1)

        hx = _modulate(_rms(x, norm_pre_mix[l]), mod_x[0], mod_x[1])
        hc = _modulate(_rms(ctx, norm_pre_mix[l]), mod_c[0], mod_c[1])
        yx, yc = _mixer(hx, hc, p, rope, lambda_init, not last)

        x = x + mod_x[2] * _rms(yx, norm_post_mix[l])
        fx = _swiglu(_modulate(_rms(x, norm_pre_ffn[l]), mod_x[3], mod_x[4]), w_ffn_in[l], w_ffn_out[l])
        x = x + mod_x[5] * _rms(fx, norm_post_ffn[l])

        if not last:
            ctx = ctx + mod_c[2] * _rms(yc, norm_post_mix[l])
            fc = _swiglu(_modulate(_rms(ctx, norm_pre_ffn[l]), mod_c[3], mod_c[4]), w_ffn_in[l], w_ffn_out[l])
            ctx = ctx + mod_c[5] * _rms(fc, norm_post_ffn[l])
    return x
```

```python
import functools
import math

import jax
import jax.numpy as jnp
from jax import lax
from jax.experimental import pallas as pl
from jax.experimental.pallas import tpu as pltpu

f32 = jnp.float32
bf16 = jnp.bfloat16

NORM_EPS = 1e-6
ROPE_BASE = 10000.0
GRID_W = 64
ROT_DIM = 64
CONV_K = 31
S5_GROUP = 16
S5_STATE = 64

LANES = 128
SUBLANES = 8
SEC = 512
N_SEC = 8
HEAD_W = 128
MLA_QK = 256
MLA_KV_RANK = 256
V7X_VMEM_BYTES = 64 * 1024 * 1024
VMEM_LIMIT = V7X_VMEM_BYTES * 7 // 8


def _cp(*sem):
    return pltpu.CompilerParams(dimension_semantics=sem, vmem_limit_bytes=VMEM_LIMIT)


def _row_tile(rows, want):
    t = min(rows, want)
    assert rows % t == 0, (rows, want)
    return t


def _ada_kernel(cc_ref, w_ref, b_ref, o_ref):
    s = jax.nn.silu(cc_ref[...]).astype(bf16)
    o_ref[...] = jnp.dot(s, w_ref[...].astype(bf16), preferred_element_type=f32) + b_ref[...]


def _ada(cc, w_ada, b_ada):
    depth, d, n = w_ada.shape
    g = cc.shape[0]
    tn = 512
    return pl.pallas_call(
        _ada_kernel,
        grid=(depth, n // tn),
        in_specs=[pl.BlockSpec((g, d), lambda l, j: (0, 0)),
                  pl.BlockSpec((None, d, tn), lambda l, j: (l, 0, j)),
                  pl.BlockSpec((None, 1, tn), lambda l, j: (l, 0, j))],
        out_specs=pl.BlockSpec((None, g, tn), lambda l, j: (l, 0, j)),
        out_shape=jax.ShapeDtypeStruct((depth, g, n), f32),
        compiler_params=_cp("arbitrary", "arbitrary"),
        name="ada",
    )(cc, w_ada, b_ada.reshape(depth, 1, n))


def _norm_mod(x, g, shift, scale):
    ms = jnp.mean(x * x, axis=-1, keepdims=True)
    return (x * lax.rsqrt(ms + NORM_EPS) * g) * (1.0 + scale) + shift


def _in_proj_kernel(x_ref, g_ref, sh_ref, sc_ref, w_ref, o_ref, h_ref):
    @pl.when(pl.program_id(1) == 0)
    def _():
        h_ref[...] = _norm_mod(x_ref[...], g_ref[...], sh_ref[...], sc_ref[...]).astype(bf16)

    o_ref[...] = jnp.dot(h_ref[...], w_ref[...], preferred_element_type=f32).astype(o_ref.dtype)


def _ffn_in_kernel(x_ref, g_ref, sh_ref, sc_ref, wg_ref, wu_ref, o_ref, h_ref):
    @pl.when(pl.program_id(1) == 0)
    def _():
        h_ref[...] = _norm_mod(x_ref[...], g_ref[...], sh_ref[...], sc_ref[...]).astype(bf16)

    h = h_ref[...]
    a = jnp.dot(h, wg_ref[...], preferred_element_type=f32)
    b = jnp.dot(h, wu_ref[...], preferred_element_type=f32)
    o_ref[...] = (jax.nn.silu(a) * b).astype(o_ref.dtype)


def _mod_spec(l, grp, k, d):
    return pl.BlockSpec((None, None, 1, d), lambda i, j: (l, grp(i), 0, k))


def _in_proj(x, g, mods, l, grp, w, tm, tn):
    rows, d = x.shape
    n = w.shape[1]
    return pl.pallas_call(
        _in_proj_kernel,
        grid=(rows // tm, n // tn),
        in_specs=[pl.BlockSpec((tm, d), lambda i, j: (i, 0)),
                  pl.BlockSpec((1, d), lambda i, j: (0, 0)),
                  _mod_spec(l, grp, 0, d), _mod_spec(l, grp, 1, d),
                  pl.BlockSpec((d, tn), lambda i, j: (0, j))],
        out_specs=pl.BlockSpec((tm, tn), lambda i, j: (i, j)),
        out_shape=jax.ShapeDtypeStruct((rows, n), f32),
        scratch_shapes=[pltpu.VMEM((tm, d), bf16)],
        compiler_params=_cp("arbitrary", "arbitrary"),
        name="in_proj",
    )(x, g, mods, mods, w)


def _ffn_in(x, g, mods, l, grp, w, tm, tn):
    rows, d = x.shape
    dff = w.shape[1] // 2
    nj = dff // tn
    return pl.pallas_call(
        _ffn_in_kernel,
        grid=(rows // tm, nj),
        in_specs=[pl.BlockSpec((tm, d), lambda i, j: (i, 0)),
                  pl.BlockSpec((1, d), lambda i, j: (0, 0)),
                  _mod_spec(l, grp, 3, d), _mod_spec(l, grp, 4, d),
                  pl.BlockSpec((d, tn), lambda i, j: (0, j)),
                  pl.BlockSpec((d, tn), lambda i, j: (0, j + nj))],
        out_specs=pl.BlockSpec((tm, tn), lambda i, j: (i, j)),
        out_shape=jax.ShapeDtypeStruct((rows, dff), bf16),
        scratch_shapes=[pltpu.VMEM((tm, d), bf16)],
        compiler_params=_cp("arbitrary", "arbitrary"),
        name="ffn_in",
    )(x, g, mods, mods, w, w)


def _residual(acc, x, g, gate):
    ms = jnp.mean(acc * acc, axis=-1, keepdims=True)
    return x + gate * (acc * lax.rsqrt(ms + NORM_EPS) * g)


def _mix_out_kernel(y0_ref, y1_ref, y2_ref, y3_ref, w_ref, x_ref, g_ref, gate_ref, o_ref, acc_ref):
    k = pl.program_id(1)

    @pl.when(k == 0)
    def _():
        acc_ref[...] = jnp.zeros_like(acc_ref)

    for s, y_ref in enumerate((y0_ref, y1_ref, y2_ref, y3_ref)):
        @pl.when(k == s)
        def _(y_ref=y_ref):
            acc_ref[...] += jnp.dot(y_ref[...], w_ref[...], preferred_element_type=f32)

    @pl.when(k == pl.num_programs(1) - 1)
    def _():
        o_ref[...] = _residual(acc_ref[...], x_ref[...], g_ref[...], gate_ref[...])


def _ffn_out_kernel(y_ref, w_ref, x_ref, g_ref, gate_ref, o_ref, acc_ref):
    k = pl.program_id(1)

    @pl.when(k == 0)
    def _():
        acc_ref[...] = jnp.zeros_like(acc_ref)

    acc_ref[...] += jnp.dot(y_ref[...], w_ref[...], preferred_element_type=f32)

    @pl.when(k == pl.num_programs(1) - 1)
    def _():
        o_ref[...] = _residual(acc_ref[...], x_ref[...], g_ref[...], gate_ref[...])


def _mix_out(ys, w, x, g, mods, l, grp, tm):
    rows, d = x.shape
    tk = ys[0].shape[1]
    return pl.pallas_call(
        _mix_out_kernel,
        grid=(rows // tm, len(ys)),
        in_specs=[pl.BlockSpec((tm, tk), lambda i, k: (i, 0))] * len(ys) + [
            pl.BlockSpec((tk, d), lambda i, k: (k, 0)),
            pl.BlockSpec((tm, d), lambda i, k: (i, 0)),
            pl.BlockSpec((1, d), lambda i, k: (0, 0)),
            _mod_spec(l, grp, 2, d)],
        out_specs=pl.BlockSpec((tm, d), lambda i, k: (i, 0)),
        out_shape=jax.ShapeDtypeStruct((rows, d), f32),
        scratch_shapes=[pltpu.VMEM((tm, d), f32)],
        compiler_params=_cp("arbitrary", "arbitrary"),
        name="mix_out",
    )(*ys, w, x, g, mods)


def _ffn_out(y, w, x, g, mods, l, grp, tm, tk):
    rows, d = x.shape
    return pl.pallas_call(
        _ffn_out_kernel,
        grid=(rows // tm, y.shape[1] // tk),
        in_specs=[pl.BlockSpec((tm, tk), lambda i, k: (i, k)),
                  pl.BlockSpec((tk, d), lambda i, k: (k, 0)),
                  pl.BlockSpec((tm, d), lambda i, k: (i, 0)),
                  pl.BlockSpec((1, d), lambda i, k: (0, 0)),
                  _mod_spec(l, grp, 5, d)],
        out_specs=pl.BlockSpec((tm, d), lambda i, k: (i, 0)),
        out_shape=jax.ShapeDtypeStruct((rows, d), f32),
        scratch_shapes=[pltpu.VMEM((tm, d), f32)],
        compiler_params=_cp("arbitrary", "arbitrary"),
        name="ffn_out",
    )(y, w, x, g, mods)


def _rope_tables(n_tokens):
    n_rows = n_tokens // GRID_W
    row = jnp.repeat(jnp.arange(n_rows, dtype=f32), GRID_W)
    col = jnp.tile(jnp.arange(GRID_W, dtype=f32), n_rows)
    n_freq = ROT_DIM // 4
    inv = ROPE_BASE ** (-jnp.arange(n_freq, dtype=f32) / n_freq)
    ang = jnp.concatenate([row[:, None] * inv, col[:, None] * inv], axis=-1)
    cos, sin = jnp.cos(ang), jnp.sin(ang)
    return jnp.tile(cos, (1, 4)), jnp.tile(jnp.concatenate([-sin, sin], axis=-1), (1, 2))


def _rope_slab(x, cos, sin):
    lane = lax.broadcasted_iota(jnp.int32, x.shape, 1)
    partner = jnp.where(lane % ROT_DIM < ROT_DIM // 2,
                        pltpu.roll(x, LANES - ROT_DIM // 2, 1), pltpu.roll(x, ROT_DIM // 2, 1))
    return x * cos + partner * sin


def _da_prep_kernel(q_ref, k_ref, v_ref, cos_ref, sin_ref, qo_ref, ko_ref, vo_ref, *, scale):
    cos, sin = cos_ref[...], sin_ref[...]
    for h in range(SEC // LANES):
        sl = slice(h * LANES, (h + 1) * LANES)
        qo_ref[:, sl] = (_rope_slab(q_ref[:, sl], cos, sin) * scale).astype(bf16)
        ko_ref[:, sl] = _rope_slab(k_ref[:, sl], cos, sin).astype(bf16)
    vo_ref[...] = v_ref[...].astype(bf16)


def _da_prep(px, cos, sin, pos_tiles, tm, scale):
    rows = px.shape[0]
    sec = lambda s: pl.BlockSpec((tm, SEC), lambda i: (i, s))
    tab = pl.BlockSpec((tm, LANES), lambda i: (i % pos_tiles, 0))
    out = jax.ShapeDtypeStruct((rows, SEC), bf16)
    return pl.pallas_call(
        functools.partial(_da_prep_kernel, scale=scale),
        grid=(rows // tm,),
        in_specs=[sec(0), sec(1), sec(2), tab, tab],
        out_specs=[pl.BlockSpec((tm, SEC), lambda i: (i, 0))] * 3,
        out_shape=[out, out, out],
        compiler_params=_cp("arbitrary"),
        name="da_prep",
    )(px, px, px, cos, sin)


def _rms_g(x, g):
    ms = jnp.mean(x * x, axis=-1, keepdims=True)
    return x * lax.rsqrt(ms + NORM_EPS) * g


def _mla_prep_kernel(cq_ref, ckv_ref, gq_ref, gkv_ref, wq_ref, wkv_ref, cos_ref, sin_ref,
                     qo_ref, ko_ref, vo_ref, *, scale):
    cos, sin = cos_ref[...], sin_ref[...]
    n_heads = SEC // HEAD_W
    q = jnp.dot(_rms_g(cq_ref[...], gq_ref[...]).astype(bf16), wq_ref[...], preferred_element_type=f32)
    kv = jnp.dot(_rms_g(ckv_ref[:, :MLA_KV_RANK], gkv_ref[...]).astype(bf16), wkv_ref[...],
                 preferred_element_type=f32)
    kr = _rope_slab(ckv_ref[:, MLA_KV_RANK:MLA_KV_RANK + LANES], cos, sin).astype(bf16)
    for h in range(n_heads):
        base = h * MLA_QK
        qo_ref[:, base:base + LANES] = (q[:, base:base + LANES] * scale).astype(bf16)
        qo_ref[:, base + LANES:base + MLA_QK] = (
            _rope_slab(q[:, base + LANES:base + MLA_QK], cos, sin) * scale).astype(bf16)
        ko_ref[:, base:base + LANES] = kv[:, h * LANES:(h + 1) * LANES].astype(bf16)
        ko_ref[:, base + LANES:base + MLA_QK] = kr
    vo_ref[...] = kv[:, n_heads * LANES:].astype(bf16)


def _mla_prep(px, gq, gkv, wq, wkv, cos, sin, pos_tiles, tm, scale):
    rows = px.shape[0]
    n_heads = SEC // HEAD_W
    sec = lambda s: pl.BlockSpec((tm, SEC), lambda i: (i, s))
    tab = pl.BlockSpec((tm, LANES), lambda i: (i % pos_tiles, 0))
    full = lambda a: pl.BlockSpec(a.shape, lambda i: (0,) * a.ndim)
    return pl.pallas_call(
        functools.partial(_mla_prep_kernel, scale=scale),
        grid=(rows // tm,),
        in_specs=[sec(4), sec(5), full(gq), full(gkv), full(wq), full(wkv), tab, tab],
        out_specs=[pl.BlockSpec((tm, n_heads * MLA_QK), lambda i: (i, 0)),
                   pl.BlockSpec((tm, n_heads * MLA_QK), lambda i: (i, 0)),
                   pl.BlockSpec((tm, SEC), lambda i: (i, 0))],
        out_shape=[jax.ShapeDtypeStruct((rows, n_heads * MLA_QK), bf16),
                   jax.ShapeDtypeStruct((rows, n_heads * MLA_QK), bf16),
                   jax.ShapeDtypeStruct((rows, SEC), bf16)],
        compiler_params=_cp("arbitrary"),
        name="mla_prep",
    )(px, px, gq, gkv, wq, wkv, cos, sin)


def _attn_kernel(*refs, diff, has_x_keys, tk, lambda_init):
    refs = list(refs)
    o_ref = refs.pop()
    q_ref, kc_ref, vc_ref = refs[:3]
    refs = refs[3:]
    if has_x_keys:
        kx_ref, vx_ref = refs[:2]
        refs = refs[2:]
    q = q_ref[...]
    tq = q.shape[0]
    if diff:
        lane = lax.broadcasted_iota(jnp.int32, q.shape, 1)
        zero = jnp.zeros_like(q)
        qs = [jnp.where(lane < ROT_DIM, q, zero), jnp.where(lane >= ROT_DIM, q, zero)]
    else:
        qs = [q]

    def update(state, k, v):
        new = []
        for qm, (m, l, acc) in zip(qs, state):
            s = lax.dot_general(qm, k, (((1,), (1,)), ((), ())), preferred_element_type=f32)
            m_new = jnp.maximum(m, jnp.max(s, axis=-1, keepdims=True))
            alpha = jnp.exp(m - m_new)
            p = jnp.exp(s - m_new)
            l = alpha * l + jnp.sum(p, axis=-1, keepdims=True)
            acc = alpha * acc + jnp.dot(p.astype(bf16), v, preferred_element_type=f32)
            new.append((m_new, l, acc))
        return tuple(new)

    init = tuple((jnp.full((tq, 1), -jnp.inf, f32), jnp.zeros((tq, 1), f32), jnp.zeros((tq, HEAD_W), f32))
                 for _ in qs)
    state = update(init, kc_ref[...], vc_ref[...])
    if has_x_keys:
        def body(c, st):
            start = pl.multiple_of(c * tk, tk)
            return update(st, kx_ref[pl.ds(start, tk), :], vx_ref[pl.ds(start, tk), :])

        state = lax.fori_loop(0, kx_ref.shape[0] // tk, body, state)

    outs = [acc / l for (_, l, acc) in state]
    if diff:
        lq1, lk1, lq2, lk2, subln = (r[...] for r in refs)
        lam = (jnp.exp(jnp.sum(lq1 * lk1, axis=-1, keepdims=True))
               - jnp.exp(jnp.sum(lq2 * lk2, axis=-1, keepdims=True)) + lambda_init)
        dlt = outs[0] - lam * outs[1]
        o_ref[...] = (_rms_g(dlt, subln) * (1.0 - lambda_init)).astype(o_ref.dtype)
    else:
        o_ref[...] = outs[0].astype(o_ref.dtype)


def _attention(q, kc, vc, kx, vx, params, n_batch, ctx_len, tq, tk, lambda_init):
    n_heads = vc.shape[1] // HEAD_W
    dq = q.shape[1] // n_heads
    lq = q.shape[0] // n_batch
    nq = lq // tq
    diff = params is not None
    in_specs = [pl.BlockSpec((tq, dq), lambda b, h, i: (b * nq + i, h)),
                pl.BlockSpec((ctx_len, dq), lambda b, h, i: (b, h)),
                pl.BlockSpec((ctx_len, HEAD_W), lambda b, h, i: (b, h))]
    args = [q, kc, vc]
    if kx is not None:
        lx = kx.shape[0] // n_batch
        in_specs += [pl.BlockSpec((lx, dq), lambda b, h, i: (b, h)),
                     pl.BlockSpec((lx, HEAD_W), lambda b, h, i: (b, h))]
        args += [kx, vx]
    if diff:
        in_specs += [pl.BlockSpec(p.shape, lambda b, h, i: (0, 0)) for p in params]
        args += list(params)
    return pl.pallas_call(
        functools.partial(_attn_kernel, diff=diff, has_x_keys=kx is not None, tk=tk, lambda_init=lambda_init),
        grid=(n_batch, n_heads, nq),
        in_specs=in_specs,
        out_specs=pl.BlockSpec((tq, HEAD_W), lambda b, h, i: (b * nq + i, h)),
        out_shape=jax.ShapeDtypeStruct((q.shape[0], n_heads * HEAD_W), bf16),
        compiler_params=_cp("arbitrary", "arbitrary", "arbitrary"),
        name="attention",
    )(*args)


def _s5_kernel(uc_ref, ux_ref, lre_ref, lim_ref, lst_ref, btr_ref, bti_ref, ctr_ref, cti_ref,
               yc_ref, yx_ref, wb_re, wb_im, wc_re, wc_im, a_re, a_im, h_re, h_im, s_re, s_im):
    d = pl.program_id(0)
    c = pl.program_id(1)
    n_batch, t_len, width = uc_ref.shape
    n_state = lre_ref.shape[1]
    half = n_state // 2
    n_groups = width // S5_GROUP
    n_blk = half // LANES

    @pl.when(c == 0)
    def _():
        lre, lim = lre_ref[...], lim_ref[...]
        step = jnp.exp(lst_ref[...])
        mag = jnp.exp(lre * step)
        are = mag * jnp.cos(lim * step)
        aim = mag * jnp.sin(lim * step)
        den = lre * lre + lim * lim
        fre = ((are - 1.0) * lre + aim * lim) / den
        fim = (aim * lre - (are - 1.0) * lim) / den
        row = lax.broadcasted_iota(jnp.int32, (width, n_state), 0) // S5_GROUP
        col = lax.broadcasted_iota(jnp.int32, (width, n_state), 1) // S5_STATE
        blk = row == col
        tile = lambda r: jnp.broadcast_to(r[...][None], (n_groups, S5_GROUP, n_state)).reshape(width, n_state)
        btr, bti = tile(btr_ref), tile(bti_ref)
        wb_re[...] = jnp.where(blk, fre * btr - fim * bti, 0.0).astype(bf16)
        wb_im[...] = jnp.where(blk, fre * bti + fim * btr, 0.0).astype(bf16)
        wc_re[...] = jnp.where(blk, tile(ctr_ref), 0.0).astype(bf16)
        wc_im[...] = jnp.where(blk, -tile(cti_ref), 0.0).astype(bf16)
        for hf in range(2):
            rows = slice(hf * n_batch, (hf + 1) * n_batch)
            for j in range(n_blk):
                cols = slice(hf * half + j * LANES, hf * half + (j + 1) * LANES)
                a_re[j, rows, :] = jnp.broadcast_to(are[:, cols], (n_batch, LANES))
                a_im[j, rows, :] = jnp.broadcast_to(aim[:, cols], (n_batch, LANES))
        h_re[...] = jnp.zeros_like(h_re)
        h_im[...] = jnp.zeros_like(h_im)

    def run(u_ref, y_ref):
        for b in range(n_batch):
            u = u_ref[b].astype(bf16)
            bre = jnp.dot(u, wb_re[...], preferred_element_type=f32)
            bim = jnp.dot(u, wb_im[...], preferred_element_type=f32)
            for hf in range(2):
                rows = pl.ds((hf * n_batch + b) * t_len, t_len)
                for j in range(n_blk):
                    cols = slice(hf * half + j * LANES, hf * half + (j + 1) * LANES)
                    s_re[j, rows, :] = bre[:, cols]
                    s_im[j, rows, :] = bim[:, cols]

        ar = [a_re[j] for j in range(n_blk)]
        ai = [a_im[j] for j in range(n_blk)]

        def step(k, carry):
            hr, hi = carry
            t = k + d * (t_len - 1 - 2 * k)
            rows = pl.ds(t, 2 * n_batch, stride=t_len)
            nr, ni = [], []
            for j in range(n_blk):
                nr.append(ar[j] * hr[j] - ai[j] * hi[j] + s_re[j, rows, :])
                ni.append(ar[j] * hi[j] + ai[j] * hr[j] + s_im[j, rows, :])
                s_re[j, rows, :] = nr[j]
                s_im[j, rows, :] = ni[j]
            return nr, ni

        init = ([h_re[j] for j in range(n_blk)], [h_im[j] for j in range(n_blk)])
        hr, hi = lax.fori_loop(0, t_len, step, init)
        for j in range(n_blk):
            h_re[j] = hr[j]
            h_im[j] = hi[j]

        nt = (((1,), (1,)), ((), ()))
        for b in range(n_batch):
            y = jnp.zeros((t_len, width), f32)
            for hf in range(2):
                rows = pl.ds((hf * n_batch + b) * t_len, t_len)
                cols = slice(hf * half, (hf + 1) * half)
                sr = jnp.concatenate([s_re[j, rows, :] for j in range(n_blk)], axis=1).astype(bf16)
                si = jnp.concatenate([s_im[j, rows, :] for j in range(n_blk)], axis=1).astype(bf16)
                y += lax.dot_general(sr, wc_re[:, cols], nt, preferred_element_type=f32)
                y += lax.dot_general(si, wc_im[:, cols], nt, preferred_element_type=f32)
            y_ref[b] = y

    @pl.when(c == 0)
    def _():
        run(uc_ref, yc_ref)

    @pl.when(c > 0)
    def _():
        run(ux_ref, yx_ref)


def _s5_scan(px_c, px_x, lam_re, lam_im, log_step, bt_re, bt_im, ct_re, ct_im, n_batch):
    t_len = px_c.shape[1]
    seq = px_x.shape[1]
    assert seq % t_len == 0
    nch = seq // t_len
    n_state = lam_re.shape[-1]
    half = n_state // 2

    def x_chunk(d, c):
        return jnp.where(d == 0, jnp.maximum(c - 1, 0), jnp.minimum(nch - c, nch - 1))

    vec = pl.BlockSpec((None, 1, n_state), lambda d, c: (d, 0, 0))
    mat = pl.BlockSpec((None, S5_GROUP, n_state), lambda d, c: (d, 0, 0))
    return pl.pallas_call(
        _s5_kernel,
        grid=(2, nch + 1),
        in_specs=[pl.BlockSpec((n_batch, t_len, SEC), lambda d, c: (0, 0, 3)),
                  pl.BlockSpec((n_batch, t_len, SEC), lambda d, c: (0, x_chunk(d, c), 3)),
                  vec, vec, vec, mat, mat, mat, mat],
        out_specs=[pl.BlockSpec((None, n_batch, t_len, SEC), lambda d, c: (d, 0, 0, 0)),
                   pl.BlockSpec((None, n_batch, t_len, SEC), lambda d, c: (d, 0, x_chunk(d, c), 0))],
        out_shape=[jax.ShapeDtypeStruct((2, n_batch, t_len, SEC), f32),
                   jax.ShapeDtypeStruct((2, n_batch, seq, SEC), f32)],
        scratch_shapes=[pltpu.VMEM((SEC, n_state), bf16)] * 4
        + [pltpu.VMEM((half // LANES, 2 * n_batch, LANES), f32)] * 4
        + [pltpu.VMEM((half // LANES, 2 * n_batch * t_len, LANES), f32)] * 2,
        compiler_params=_cp("arbitrary", "arbitrary"),
        name="s5_scan",
    )(px_c, px_x, lam_re, lam_im, log_step, bt_re, bt_im, ct_re, ct_im)


def _s5_glu_kernel(yf_ref, yb_ref, u_ref, d_ref, w_ref, o_ref):
    y = d_ref[...] * u_ref[...] + yf_ref[...] + yb_ref[...]
    g = jax.nn.gelu(y)
    z = jnp.dot(g.astype(bf16), w_ref[...], preferred_element_type=f32)
    o_ref[...] = (g * jax.nn.sigmoid(z)).astype(o_ref.dtype)


def _s5_glu(y, px, s5_d, w_glu, tm):
    rows = px.shape[0]
    return pl.pallas_call(
        _s5_glu_kernel,
        grid=(rows // tm,),
        in_specs=[pl.BlockSpec((None, tm, SEC), lambda i: (0, i, 0)),
                  pl.BlockSpec((None, tm, SEC), lambda i: (1, i, 0)),
                  pl.BlockSpec((tm, SEC), lambda i: (i, 3)),
                  pl.BlockSpec((1, SEC), lambda i: (0, 0)),
                  pl.BlockSpec((SEC, SEC), lambda i: (0, 0))],
        out_specs=pl.BlockSpec((tm, SEC), lambda i: (i, 0)),
        out_shape=jax.ShapeDtypeStruct((rows, SEC), bf16),
        compiler_params=_cp("arbitrary"),
        name="s5_glu",
    )(y, y, px, s5_d, w_glu)


CONV_HALO = 16


def _conv_kernel(vp_ref, vc_ref, vn_ref, gp_ref, gc_ref, gn_ref, w_ref, b_ref, lg_ref, lb_ref, o_ref, ext_ref):
    j = pl.program_id(1)
    tc = vc_ref.shape[0]
    glu = lambda v, g: v[...] * jax.nn.sigmoid(g[...])
    ext_ref[pl.ds(0, CONV_HALO), :] = jnp.where(j > 0, glu(vp_ref, gp_ref), 0.0)
    ext_ref[pl.ds(CONV_HALO, tc), :] = glu(vc_ref, gc_ref)
    ext_ref[pl.ds(CONV_HALO + tc, CONV_HALO), :] = jnp.where(j < pl.num_programs(1) - 1, glu(vn_ref, gn_ref), 0.0)
    y = jnp.zeros((tc, vc_ref.shape[1]), f32)
    for k in range(CONV_K):
        y += w_ref[pl.ds(k, 1), :] * ext_ref[pl.ds(CONV_HALO - CONV_K // 2 + k, tc), :]
    y = y + b_ref[...]
    mu = jnp.mean(y, axis=-1, keepdims=True)
    var = jnp.mean(jnp.square(y - mu), axis=-1, keepdims=True)
    z = (y - mu) * lax.rsqrt(var + NORM_EPS) * lg_ref[...] + lb_ref[...]
    o_ref[...] = jax.nn.silu(z).astype(o_ref.dtype)


def _conv(px, conv_w, conv_b, ln_g, ln_b, n_batch, tc):
    rows = px.shape[0]
    seq = rows // n_batch
    nt = seq // tc
    hpt = tc // CONV_HALO
    cur = lambda s: pl.BlockSpec((tc, SEC), lambda b, j: (b * nt + j, s))
    prev = lambda s: pl.BlockSpec((CONV_HALO, SEC), lambda b, j: (jnp.maximum((b * nt + j) * hpt - 1, 0), s))
    nxt = lambda s: pl.BlockSpec(
        (CONV_HALO, SEC), lambda b, j: (jnp.minimum((b * nt + j + 1) * hpt, rows // CONV_HALO - 1), s))
    full = lambda a: pl.BlockSpec(a.shape, lambda b, j: (0,) * a.ndim)
    return pl.pallas_call(
        _conv_kernel,
        grid=(n_batch, nt),
        in_specs=[prev(6), cur(6), nxt(6), prev(7), cur(7), nxt(7),
                  full(conv_w), full(conv_b), full(ln_g), full(ln_b)],
        out_specs=pl.BlockSpec((tc, SEC), lambda b, j: (b * nt + j, 0)),
        out_shape=jax.ShapeDtypeStruct((rows, SEC), bf16),
        scratch_shapes=[pltpu.VMEM((tc + 2 * CONV_HALO, SEC), f32)],
        compiler_params=_cp("arbitrary", "arbitrary"),
        name="conv",
    )(px, px, px, px, px, px, conv_w, conv_b, ln_g, ln_b)


def kernel(x, c, ctx, c_ctx, w_ada, b_ada, norm_pre_mix, norm_post_mix, norm_pre_ffn, norm_post_ffn, w_in, w_out, da_lam_q1, da_lam_k1, da_lam_q2, da_lam_k2, da_subln, s5_lam_re, s5_lam_im, s5_log_step, s5_b_re, s5_b_im, s5_c_re, s5_c_im, s5_d, s5_w_glu, mla_q_norm, mla_kv_norm, mla_w_uq, mla_w_ukv, conv_w, conv_b, conv_ln_g, conv_ln_b, w_ffn_in, w_ffn_out):
    n_batch, seq, d = x.shape
    ctx_len = ctx.shape[1]
    depth = w_in.shape[0]
    n_state = s5_lam_re.shape[2] * s5_lam_re.shape[3]
    mla_end = 4 * SEC + SEC + MLA_KV_RANK + ROT_DIM
    assert d == 4 * SEC and w_in.shape[2] == mla_end + 2 * SEC

    da_scale = ROT_DIM ** -0.5
    mla_scale = (HEAD_W + ROT_DIM) ** -0.5

    tm_x = _row_tile(seq, 1024)
    tm_c = _row_tile(n_batch * ctx_len, 1024)
    tpb = seq // tm_x
    grp_x = lambda i: i // tpb
    grp_c = lambda i: n_batch
    tq = _row_tile(seq, 256)
    tk = _row_tile(seq, 512)

    n_grp = -(-(n_batch + 1) // SUBLANES) * SUBLANES
    cc = jnp.concatenate([c, c_ctx[None], jnp.zeros((n_grp - n_batch - 1, d), f32)], axis=0)
    mods = _ada(cc, w_ada, b_ada).reshape(depth, n_grp, 1, 6 * d)

    cos_x, sin_x = _rope_tables(seq)
    cos_c, sin_c = jnp.ones((tm_c, LANES), f32), jnp.zeros((tm_c, LANES), f32)

    w_in_p = jnp.concatenate([w_in[:, :, :mla_end], jnp.zeros((depth, d, N_SEC * SEC - w_in.shape[2]), f32),
                              w_in[:, :, mla_end:]], axis=2).astype(bf16)
    w_out_b = w_out.astype(bf16)
    w_ffn_in_b = w_ffn_in.astype(bf16)
    w_ffn_out_b = w_ffn_out.astype(bf16)
    w_glu_b = s5_w_glu.astype(bf16)
    n_mla = SEC // HEAD_W
    wq = mla_w_uq.reshape(depth, SEC, n_mla, HEAD_W + ROT_DIM)
    wq = jnp.pad(wq, ((0, 0), (0, 0), (0, 0), (0, MLA_QK - HEAD_W - ROT_DIM))).reshape(depth, SEC, n_mla * MLA_QK)
    wq = wq.astype(bf16)
    wkv = mla_w_ukv.reshape(depth, MLA_KV_RANK, n_mla, 2, HEAD_W).transpose(0, 1, 3, 2, 4)
    wkv = wkv.reshape(depth, MLA_KV_RANK, 2 * n_mla * HEAD_W).astype(bf16)

    flat = lambda a: a.reshape(depth, 2, 1, n_state)
    s5_lre, s5_lim = flat(s5_lam_re), flat(s5_lam_im)
    s5_lst = flat(jnp.broadcast_to(s5_log_step[..., None], s5_lam_re.shape))
    s5_btr = s5_b_re.transpose(0, 1, 4, 2, 3).reshape(depth, 2, S5_GROUP, n_state)
    s5_bti = s5_b_im.transpose(0, 1, 4, 2, 3).reshape(depth, 2, S5_GROUP, n_state)
    s5_ctr = s5_c_re.transpose(0, 1, 3, 2, 4).reshape(depth, 2, S5_GROUP, n_state)
    s5_cti = s5_c_im.transpose(0, 1, 3, 2, 4).reshape(depth, 2, S5_GROUP, n_state)

    row = lambda a, l: a[l][None]
    conv_w_p = jnp.pad(conv_w, ((0, 0), (0, 1), (0, 0)))

    xs = x.reshape(n_batch * seq, d)
    cs = ctx.reshape(n_batch * ctx_len, d)
    for l in range(depth):
        last = l == depth - 1
        lambda_init = 0.8 - 0.6 * math.exp(-0.3 * l)
        g_pre = row(norm_pre_mix, l)

        px = _in_proj(xs, g_pre, mods, l, grp_x, w_in_p[l], tm_x, SEC)
        pc = _in_proj(cs, g_pre, mods, l, grp_c, w_in_p[l], tm_c, SEC)

        qx, kx, vx = _da_prep(px, cos_x, sin_x, tpb, tm_x, da_scale)
        qc, kc, vc = _da_prep(pc, cos_c, sin_c, 1, tm_c, da_scale)
        da_params = (row(da_lam_q1, l), row(da_lam_k1, l), row(da_lam_q2, l), row(da_lam_k2, l), row(da_subln, l))
        y_da_x = _attention(qx, kc, vc, kx, vx, da_params, n_batch, ctx_len, tq, tk, lambda_init)

        mla_w = (row(mla_q_norm, l), row(mla_kv_norm, l), wq[l], wkv[l])
        mqx, mkx, mvx = _mla_prep(px, *mla_w, cos_x, sin_x, tpb, tm_x, mla_scale)
        mqc, mkc, mvc = _mla_prep(pc, *mla_w, cos_c, sin_c, 1, tm_c, mla_scale)
        y_mla_x = _attention(mqx, mkc, mvc, mkx, mvx, None, n_batch, ctx_len, tq, tk, lambda_init)

        ys_c, ys_x = _s5_scan(pc.reshape(n_batch, ctx_len, -1), px.reshape(n_batch, seq, -1),
                              s5_lre[l], s5_lim[l], s5_lst[l], s5_btr[l], s5_bti[l], s5_ctr[l], s5_cti[l], n_batch)
        y_s5_x = _s5_glu(ys_x.reshape(2, n_batch * seq, SEC), px, row(s5_d, l), w_glu_b[l], tm_x)

        cv = (conv_w_p[l], row(conv_b, l), row(conv_ln_g, l), row(conv_ln_b, l))
        y_cv_x = _conv(px, *cv, n_batch, _row_tile(seq, 512))

        g_post, g_pre_f, g_post_f = row(norm_post_mix, l), row(norm_pre_ffn, l), row(norm_post_ffn, l)
        tm_o = _row_tile(tm_x, 512)
        x1 = _mix_out((y_da_x, y_s5_x, y_mla_x, y_cv_x), w_out_b[l], xs, g_post, mods, l,
                      lambda i: i // (seq // tm_o), tm_o)
        act = _ffn_in(x1, g_pre_f, mods, l, grp_x, w_ffn_in_b[l], tm_x, SEC)
        xs = _ffn_out(act, w_ffn_out_b[l], x1, g_post_f, mods, l, lambda i: i // (seq // tm_o), tm_o, SEC)

        if not last:
            y_da_c = _attention(qc, kc, vc, None, None, da_params, n_batch, ctx_len, ctx_len, tk, lambda_init)
            y_mla_c = _attention(mqc, mkc, mvc, None, None, None, n_batch, ctx_len, ctx_len, tk, lambda_init)
            y_s5_c = _s5_glu(ys_c.reshape(2, n_batch * ctx_len, SEC), pc, row(s5_d, l), w_glu_b[l], tm_c)
            y_cv_c = _conv(pc, *cv, n_batch, ctx_len)
            tm_oc = _row_tile(tm_c, 512)
            c1 = _mix_out((y_da_c, y_s5_c, y_mla_c, y_cv_c), w_out_b[l], cs, g_post, mods, l, grp_c, tm_oc)
            act_c = _ffn_in(c1, g_pre_f, mods, l, grp_c, w_ffn_in_b[l], tm_c, SEC)
            cs = _ffn_out(act_c, w_ffn_out_b[l], c1, g_post_f, mods, l, grp_c, tm_oc, SEC)
    return xs.reshape(n_batch, seq, d)
```

```python
import functools
import math

import jax
import jax.numpy as jnp
from jax import lax
from jax.experimental import pallas as pl
from jax.experimental.pallas import tpu as pltpu

f32 = jnp.float32
bf16 = jnp.bfloat16

NORM_EPS = 1e-6
ROPE_BASE = 10000.0
GRID_W = 64
ROT_DIM = 64
CONV_K = 31
S5_GROUP = 16
S5_STATE = 64

LANES = 128
SUBLANES = 8
SEC = 512
N_SEC = 8
HEAD_W = 128
MLA_QK = 256
MLA_KV_RANK = 256
V7X_VMEM_BYTES = 64 * 1024 * 1024
VMEM_LIMIT = V7X_VMEM_BYTES * 7 // 8


def _cp(*sem):
    return pltpu.CompilerParams(dimension_semantics=sem, vmem_limit_bytes=VMEM_LIMIT)


def _row_tile(rows, want):
    t = min(rows, want)
    assert rows % t == 0, (rows, want)
    return t


def _ada_kernel(cc_ref, w_ref, b_ref, o_ref):
    s = jax.nn.silu(cc_ref[...]).astype(bf16)
    o_ref[...] = jnp.dot(s, w_ref[...].astype(bf16), preferred_element_type=f32) + b_ref[...]


def _ada(cc, w_ada, b_ada):
    depth, d, n = w_ada.shape
    g = cc.shape[0]
    tn = 512
    return pl.pallas_call(
        _ada_kernel,
        grid=(depth, n // tn),
        in_specs=[pl.BlockSpec((g, d), lambda l, j: (0, 0)),
                  pl.BlockSpec((None, d, tn), lambda l, j: (l, 0, j)),
                  pl.BlockSpec((None, 1, tn), lambda l, j: (l, 0, j))],
        out_specs=pl.BlockSpec((None, g, tn), lambda l, j: (l, 0, j)),
        out_shape=jax.ShapeDtypeStruct((depth, g, n), f32),
        compiler_params=_cp("arbitrary", "arbitrary"),
        name="ada",
    )(cc, w_ada, b_ada.reshape(depth, 1, n))


def _norm_mod(x, g, shift, scale):
    ms = jnp.mean(x * x, axis=-1, keepdims=True)
    return (x * lax.rsqrt(ms + NORM_EPS) * g) * (1.0 + scale) + shift


def _in_proj_kernel(x_ref, g_ref, sh_ref, sc_ref, w_ref, o_ref, h_ref):
    @pl.when(pl.program_id(1) == 0)
    def _():
        h_ref[...] = _norm_mod(x_ref[...], g_ref[...], sh_ref[...], sc_ref[...]).astype(bf16)

    o_ref[...] = jnp.dot(h_ref[...], w_ref[...], preferred_element_type=f32).astype(o_ref.dtype)


def _ffn_in_kernel(x_ref, g_ref, sh_ref, sc_ref, wg_ref, wu_ref, o_ref, h_ref):
    @pl.when(pl.program_id(1) == 0)
    def _():
        h_ref[...] = _norm_mod(x_ref[...], g_ref[...], sh_ref[...], sc_ref[...]).astype(bf16)

    h = h_ref[...]
    a = jnp.dot(h, wg_ref[...], preferred_element_type=f32)
    b = jnp.dot(h, wu_ref[...], preferred_element_type=f32)
    o_ref[...] = (jax.nn.silu(a) * b).astype(o_ref.dtype)


def _mod_spec(l, grp, k, d):
    return pl.BlockSpec((None, None, 1, d), lambda i, j: (l, grp(i), 0, k))


def _in_proj(x, g, mods, l, grp, w, tm, tn):
    rows, d = x.shape
    n = w.shape[1]
    return pl.pallas_call(
        _in_proj_kernel,
        grid=(rows // tm, n // tn),
        in_specs=[pl.BlockSpec((tm, d), lambda i, j: (i, 0)),
                  pl.BlockSpec((1, d), lambda i, j: (0, 0)),
                  _mod_spec(l, grp, 0, d), _mod_spec(l, grp, 1, d),
                  pl.BlockSpec((d, tn), lambda i, j: (0, j))],
        out_specs=pl.BlockSpec((tm, tn), lambda i, j: (i, j)),
        out_shape=jax.ShapeDtypeStruct((rows, n), f32),
        scratch_shapes=[pltpu.VMEM((tm, d), bf16)],
        compiler_params=_cp("arbitrary", "arbitrary"),
        name="in_proj",
    )(x, g, mods, mods, w)


def _ffn_in(x, g, mods, l, grp, w, tm, tn):
    rows, d = x.shape
    dff = w.shape[1] // 2
    nj = dff // tn
    return pl.pallas_call(
        _ffn_in_kernel,
        grid=(rows // tm, nj),
        in_specs=[pl.BlockSpec((tm, d), lambda i, j: (i, 0)),
                  pl.BlockSpec((1, d), lambda i, j: (0, 0)),
                  _mod_spec(l, grp, 3, d), _mod_spec(l, grp, 4, d),
                  pl.BlockSpec((d, tn), lambda i, j: (0, j)),
                  pl.BlockSpec((d, tn), lambda i, j: (0, j + nj))],
        out_specs=pl.BlockSpec((tm, tn), lambda i, j: (i, j)),
        out_shape=jax.ShapeDtypeStruct((rows, dff), bf16),
        scratch_shapes=[pltpu.VMEM((tm, d), bf16)],
        compiler_params=_cp("arbitrary", "arbitrary"),
        name="ffn_in",
    )(x, g, mods, mods, w, w)


def _residual(acc, x, g, gate):
    ms = jnp.mean(acc * acc, axis=-1, keepdims=True)
    return x + gate * (acc * lax.rsqrt(ms + NORM_EPS) * g)


def _mix_out_kernel(y0_ref, y1_ref, y2_ref, y3_ref, w_ref, x_ref, g_ref, gate_ref, o_ref, acc_ref):
    k = pl.program_id(1)

    @pl.when(k == 0)
    def _():
        acc_ref[...] = jnp.zeros_like(acc_ref)

    for s, y_ref in enumerate((y0_ref, y1_ref, y2_ref, y3_ref)):
        @pl.when(k == s)
        def _(y_ref=y_ref):
            acc_ref[...] += jnp.dot(y_ref[...], w_ref[...], preferred_element_type=f32)

    @pl.when(k == pl.num_programs(1) - 1)
    def _():
        o_ref[...] = _residual(acc_ref[...], x_ref[...], g_ref[...], gate_ref[...])


def _ffn_out_kernel(y_ref, w_ref, x_ref, g_ref, gate_ref, o_ref, acc_ref):
    k = pl.program_id(1)

    @pl.when(k == 0)
    def _():
        acc_ref[...] = jnp.zeros_like(acc_ref)

    acc_ref[...] += jnp.dot(y_ref[...], w_ref[...], preferred_element_type=f32)

    @pl.when(k == pl.num_programs(1) - 1)
    def _():
        o_ref[...] = _residual(acc_ref[...], x_ref[...], g_ref[...], gate_ref[...])


def _mix_out(ys, w, x, g, mods, l, grp, tm):
    rows, d = x.shape
    tk = ys[0].shape[1]
    return pl.pallas_call(
        _mix_out_kernel,
        grid=(rows // tm, len(ys)),
        in_specs=[pl.BlockSpec((tm, tk), lambda i, k: (i, 0))] * len(ys) + [
            pl.BlockSpec((tk, d), lambda i, k: (k, 0)),
            pl.BlockSpec((tm, d), lambda i, k: (i, 0)),
            pl.BlockSpec((1, d), lambda i, k: (0, 0)),
            _mod_spec(l, grp, 2, d)],
        out_specs=pl.BlockSpec((tm, d), lambda i, k: (i, 0)),
        out_shape=jax.ShapeDtypeStruct((rows, d), f32),
        scratch_shapes=[pltpu.VMEM((tm, d), f32)],
        compiler_params=_cp("arbitrary", "arbitrary"),
        name="mix_out",
    )(*ys, w, x, g, mods)


def _ffn_out(y, w, x, g, mods, l, grp, tm, tk):
    rows, d = x.shape
    return pl.pallas_call(
        _ffn_out_kernel,
        grid=(rows // tm, y.shape[1] // tk),
        in_specs=[pl.BlockSpec((tm, tk), lambda i, k: (i, k)),
                  pl.BlockSpec((tk, d), lambda i, k: (k, 0)),
                  pl.BlockSpec((tm, d), lambda i, k: (i, 0)),
                  pl.BlockSpec((1, d), lambda i, k: (0, 0)),
                  _mod_spec(l, grp, 5, d)],
        out_specs=pl.BlockSpec((tm, d), lambda i, k: (i, 0)),
        out_shape=jax.ShapeDtypeStruct((rows, d), f32),
        scratch_shapes=[pltpu.VMEM((tm, d), f32)],
        compiler_params=_cp("arbitrary", "arbitrary"),
        name="ffn_out",
    )(y, w, x, g, mods)


def _rope_tables(n_tokens):
    n_rows = n_tokens // GRID_W
    row = jnp.repeat(jnp.arange(n_rows, dtype=f32), GRID_W)
    col = jnp.tile(jnp.arange(GRID_W, dtype=f32), n_rows)
    n_freq = ROT_DIM // 4
    inv = ROPE_BASE ** (-jnp.arange(n_freq, dtype=f32) / n_freq)
    ang = jnp.concatenate([row[:, None] * inv, col[:, None] * inv], axis=-1)
    cos, sin = jnp.cos(ang), jnp.sin(ang)
    return jnp.tile(cos, (1, 4)), jnp.tile(jnp.concatenate([-sin, sin], axis=-1), (1, 2))


def _rope_slab(x, cos, sin):
    lane = lax.broadcasted_iota(jnp.int32, x.shape, 1)
    partner = jnp.where(lane % ROT_DIM < ROT_DIM // 2,
                        pltpu.roll(x, LANES - ROT_DIM // 2, 1), pltpu.roll(x, ROT_DIM // 2, 1))
    return x * cos + partner * sin


def _da_prep_kernel(q_ref, k_ref, v_ref, cos_ref, sin_ref, qo_ref, ko_ref, vo_ref, *, scale):
    cos, sin = cos_ref[...], sin_ref[...]
    for h in range(SEC // LANES):
        sl = slice(h * LANES, (h + 1) * LANES)
        qo_ref[:, sl] = (_rope_slab(q_ref[:, sl], cos, sin) * scale).astype(bf16)
        ko_ref[:, sl] = _rope_slab(k_ref[:, sl], cos, sin).astype(bf16)
    vo_ref[...] = v_ref[...].astype(bf16)


def _da_prep(px, cos, sin, pos_tiles, tm, scale):
    rows = px.shape[0]
    sec = lambda s: pl.BlockSpec((tm, SEC), lambda i: (i, s))
    tab = pl.BlockSpec((tm, LANES), lambda i: (i % pos_tiles, 0))
    out = jax.ShapeDtypeStruct((rows, SEC), bf16)
    return pl.pallas_call(
        functools.partial(_da_prep_kernel, scale=scale),
        grid=(rows // tm,),
        in_specs=[sec(0), sec(1), sec(2), tab, tab],
        out_specs=[pl.BlockSpec((tm, SEC), lambda i: (i, 0))] * 3,
        out_shape=[out, out, out],
        compiler_params=_cp("arbitrary"),
        name="da_prep",
    )(px, px, px, cos, sin)


def _rms_g(x, g):
    ms = jnp.mean(x * x, axis=-1, keepdims=True)
    return x * lax.rsqrt(ms + NORM_EPS) * g


def _mla_prep_kernel(cq_ref, ckv_ref, gq_ref, gkv_ref, wq_ref, wkv_ref, cos_ref, sin_ref,
                     qo_ref, ko_ref, vo_ref, *, scale):
    cos, sin = cos_ref[...], sin_ref[...]
    n_heads = SEC // HEAD_W
    q = jnp.dot(_rms_g(cq_ref[...], gq_ref[...]).astype(bf16), wq_ref[...], preferred_element_type=f32)
    kv = jnp.dot(_rms_g(ckv_ref[:, :MLA_KV_RANK], gkv_ref[...]).astype(bf16), wkv_ref[...],
                 preferred_element_type=f32)
    kr = _rope_slab(ckv_ref[:, MLA_KV_RANK:MLA_KV_RANK + LANES], cos, sin).astype(bf16)
    for h in range(n_heads):
        base = h * MLA_QK
        qo_ref[:, base:base + LANES] = (q[:, base:base + LANES] * scale).astype(bf16)
        qo_ref[:, base + LANES:base + MLA_QK] = (
            _rope_slab(q[:, base + LANES:base + MLA_QK], cos, sin) * scale).astype(bf16)
        ko_ref[:, base:base + LANES] = kv[:, h * LANES:(h + 1) * LANES].astype(bf16)
        ko_ref[:, base + LANES:base + MLA_QK] = kr
    vo_ref[...] = kv[:, n_heads * LANES:].astype(bf16)


def _mla_prep(px, gq, gkv, wq, wkv, cos, sin, pos_tiles, tm, scale):
    rows = px.shape[0]
    n_heads = SEC // HEAD_W
    sec = lambda s: pl.BlockSpec((tm, SEC), lambda i: (i, s))
    tab = pl.BlockSpec((tm, LANES), lambda i: (i % pos_tiles, 0))
    full = lambda a: pl.BlockSpec(a.shape, lambda i: (0,) * a.ndim)
    return pl.pallas_call(
        functools.partial(_mla_prep_kernel, scale=scale),
        grid=(rows // tm,),
        in_specs=[sec(4), sec(5), full(gq), full(gkv), full(wq), full(wkv), tab, tab],
        out_specs=[pl.BlockSpec((tm, n_heads * MLA_QK), lambda i: (i, 0)),
                   pl.BlockSpec((tm, n_heads * MLA_QK), lambda i: (i, 0)),
                   pl.BlockSpec((tm, SEC), lambda i: (i, 0))],
        out_shape=[jax.ShapeDtypeStruct((rows, n_heads * MLA_QK), bf16),
                   jax.ShapeDtypeStruct((rows, n_heads * MLA_QK), bf16),
                   jax.ShapeDtypeStruct((rows, SEC), bf16)],
        compiler_params=_cp("arbitrary"),
        name="mla_prep",
    )(px, px, gq, gkv, wq, wkv, cos, sin)


def _attn_kernel(*refs, diff, has_x_keys, tk, lambda_init):
    refs = list(refs)
    s_ref = refs.pop()
    o_ref = refs.pop()
    q_ref, kc_ref, vc_ref = refs[:3]
    refs = refs[3:]
    chunks = [(kc_ref, vc_ref, 0, kc_ref.shape[0])]
    if has_x_keys:
        kx_ref, vx_ref = refs[:2]
        refs = refs[2:]
        chunks += [(kx_ref, vx_ref, c * tk, tk) for c in range(kx_ref.shape[0] // tk)]
    q = q_ref[...]
    tq = q.shape[0]
    if diff:
        lane = lax.broadcasted_iota(jnp.int32, q.shape, 1)
        zero = jnp.zeros_like(q)
        qs = [jnp.where(lane < ROT_DIM, q, zero), jnp.where(lane >= ROT_DIM, q, zero)]
    else:
        qs = [q]
    lane_tiles = lambda a: [a[:, j * LANES:(j + 1) * LANES] for j in range(a.shape[1] // LANES)]

    m_acc = [jnp.full((tq, LANES), -jnp.inf, f32) for _ in qs]
    col = 0
    for k_ref, _, start, size in chunks:
        k = k_ref[pl.ds(start, size), :]
        for mi, qm in enumerate(qs):
            s = lax.dot_general(qm, k, (((1,), (1,)), ((), ())), preferred_element_type=f32)
            s_ref[mi, :, col:col + size] = s
            for t in lane_tiles(s):
                m_acc[mi] = jnp.maximum(m_acc[mi], t)
        col += size
    m_row = [jnp.broadcast_to(jnp.max(m, axis=-1, keepdims=True), (tq, LANES)) for m in m_acc]

    l_acc = [jnp.zeros((tq, LANES), f32) for _ in qs]
    acc = [jnp.zeros((tq, HEAD_W), f32) for _ in qs]
    col = 0
    for _, v_ref, start, size in chunks:
        v = v_ref[pl.ds(start, size), :]
        for mi in range(len(qs)):
            p = jnp.exp2(s_ref[mi, :, col:col + size] - jnp.tile(m_row[mi], (1, size // LANES)))
            for t in lane_tiles(p):
                l_acc[mi] = l_acc[mi] + t
            acc[mi] = acc[mi] + jnp.dot(p.astype(bf16), v, preferred_element_type=f32)
        col += size

    outs = [a / jnp.sum(l, axis=-1, keepdims=True) for a, l in zip(acc, l_acc)]
    if diff:
        lq1, lk1, lq2, lk2, subln = (r[...] for r in refs)
        lam = (jnp.exp(jnp.sum(lq1 * lk1, axis=-1, keepdims=True))
               - jnp.exp(jnp.sum(lq2 * lk2, axis=-1, keepdims=True)) + lambda_init)
        dlt = outs[0] - lam * outs[1]
        o_ref[...] = (_rms_g(dlt, subln) * (1.0 - lambda_init)).astype(o_ref.dtype)
    else:
        o_ref[...] = outs[0].astype(o_ref.dtype)


def _attention(q, kc, vc, kx, vx, params, n_batch, ctx_len, tq, tk, lambda_init):
    n_heads = vc.shape[1] // HEAD_W
    dq = q.shape[1] // n_heads
    lq = q.shape[0] // n_batch
    nq = lq // tq
    diff = params is not None
    in_specs = [pl.BlockSpec((tq, dq), lambda b, h, i: (b * nq + i, h)),
                pl.BlockSpec((ctx_len, dq), lambda b, h, i: (b, h)),
                pl.BlockSpec((ctx_len, HEAD_W), lambda b, h, i: (b, h))]
    args = [q, kc, vc]
    if kx is not None:
        lx = kx.shape[0] // n_batch
        in_specs += [pl.BlockSpec((lx, dq), lambda b, h, i: (b, h)),
                     pl.BlockSpec((lx, HEAD_W), lambda b, h, i: (b, h))]
        args += [kx, vx]
    n_keys = ctx_len + (kx.shape[0] // n_batch if kx is not None else 0)
    if diff:
        in_specs += [pl.BlockSpec(p.shape, lambda b, h, i: (0, 0)) for p in params]
        args += list(params)
    return pl.pallas_call(
        functools.partial(_attn_kernel, diff=diff, has_x_keys=kx is not None, tk=tk, lambda_init=lambda_init),
        grid=(n_batch, n_heads, nq),
        in_specs=in_specs,
        out_specs=pl.BlockSpec((tq, HEAD_W), lambda b, h, i: (b * nq + i, h)),
        out_shape=jax.ShapeDtypeStruct((q.shape[0], n_heads * HEAD_W), bf16),
        scratch_shapes=[pltpu.VMEM((2 if diff else 1, tq, n_keys), f32)],
        compiler_params=_cp("arbitrary", "arbitrary", "arbitrary"),
        name="attention",
    )(*args)


def _s5_kernel(uc_ref, ux_ref, lre_ref, lim_ref, lst_ref, btr_ref, bti_ref, ctr_ref, cti_ref,
               yc_ref, yx_ref, wb_re, wb_im, wc_re, wc_im, a_re, a_im, h_re, h_im, s_re, s_im):
    d = pl.program_id(0)
    c = pl.program_id(1)
    n_batch, t_len, width = uc_ref.shape
    n_state = lre_ref.shape[1]
    half = n_state // 2
    n_groups = width // S5_GROUP
    n_blk = half // LANES

    @pl.when(c == 0)
    def _():
        lre, lim = lre_ref[...], lim_ref[...]
        step = jnp.exp(lst_ref[...])
        mag = jnp.exp(lre * step)
        are = mag * jnp.cos(lim * step)
        aim = mag * jnp.sin(lim * step)
        den = lre * lre + lim * lim
        fre = ((are - 1.0) * lre + aim * lim) / den
        fim = (aim * lre - (are - 1.0) * lim) / den
        row = lax.broadcasted_iota(jnp.int32, (width, n_state), 0) // S5_GROUP
        col = lax.broadcasted_iota(jnp.int32, (width, n_state), 1) // S5_STATE
        blk = row == col
        tile = lambda r: jnp.broadcast_to(r[...][None], (n_groups, S5_GROUP, n_state)).reshape(width, n_state)
        btr, bti = tile(btr_ref), tile(bti_ref)
        wb_re[...] = jnp.where(blk, fre * btr - fim * bti, 0.0).astype(bf16)
        wb_im[...] = jnp.where(blk, fre * bti + fim * btr, 0.0).astype(bf16)
        wc_re[...] = jnp.where(blk, tile(ctr_ref), 0.0).astype(bf16)
        wc_im[...] = jnp.where(blk, -tile(cti_ref), 0.0).astype(bf16)
        for hf in range(2):
            rows = slice(hf * n_batch, (hf + 1) * n_batch)
            for j in range(n_blk):
                cols = slice(hf * half + j * LANES, hf * half + (j + 1) * LANES)
                a_re[j, rows, :] = jnp.broadcast_to(are[:, cols], (n_batch, LANES))
                a_im[j, rows, :] = jnp.broadcast_to(aim[:, cols], (n_batch, LANES))
        h_re[...] = jnp.zeros_like(h_re)
        h_im[...] = jnp.zeros_like(h_im)

    pitch = s_re.shape[1] // (2 * n_batch)
    kb = LANES
    sb = kb // S5_GROUP * S5_STATE
    nt = (((1,), (1,)), ((), ()))

    def slab(col0, b):
        hf, rem = divmod(col0, half)
        return rem // LANES, pl.ds((hf * n_batch + b) * pitch, t_len)

    def run(u_ref, y_ref):
        for b in range(n_batch):
            u = u_ref[b].astype(bf16)
            for bb in range(width // kb):
                ub = u[:, bb * kb:(bb + 1) * kb]
                bre = jnp.dot(ub, wb_re[bb * kb:(bb + 1) * kb, bb * sb:(bb + 1) * sb], preferred_element_type=f32)
                bim = jnp.dot(ub, wb_im[bb * kb:(bb + 1) * kb, bb * sb:(bb + 1) * sb], preferred_element_type=f32)
                for jj in range(sb // LANES):
                    j, rows = slab(bb * sb + jj * LANES, b)
                    s_re[j, rows, :] = bre[:, jj * LANES:(jj + 1) * LANES]
                    s_im[j, rows, :] = bim[:, jj * LANES:(jj + 1) * LANES]

        ar = [a_re[j] for j in range(n_blk)]
        ai = [a_im[j] for j in range(n_blk)]

        def step(k, carry):
            hr, hi = carry
            t = k + d * (t_len - 1 - 2 * k)
            rows = pl.ds(t, 2 * n_batch, stride=pitch)
            nr, ni = [], []
            for j in range(n_blk):
                nr.append(ar[j] * hr[j] - ai[j] * hi[j] + s_re[j, rows, :])
                ni.append(ar[j] * hi[j] + ai[j] * hr[j] + s_im[j, rows, :])
                s_re[j, rows, :] = nr[j]
                s_im[j, rows, :] = ni[j]
            return nr, ni

        init = ([h_re[j] for j in range(n_blk)], [h_im[j] for j in range(n_blk)])
        hr, hi = lax.fori_loop(0, t_len, step, init)
        for j in range(n_blk):
            h_re[j] = hr[j]
            h_im[j] = hi[j]

        for b in range(n_batch):
            ys = []
            for bb in range(width // kb):
                slabs = [slab(bb * sb + jj * LANES, b) for jj in range(sb // LANES)]
                sr = jnp.concatenate([s_re[j, rows, :] for j, rows in slabs], axis=1).astype(bf16)
                si = jnp.concatenate([s_im[j, rows, :] for j, rows in slabs], axis=1).astype(bf16)
                ys.append(
                    lax.dot_general(sr, wc_re[bb * kb:(bb + 1) * kb, bb * sb:(bb + 1) * sb], nt,
                                    preferred_element_type=f32)
                    + lax.dot_general(si, wc_im[bb * kb:(bb + 1) * kb, bb * sb:(bb + 1) * sb], nt,
                                      preferred_element_type=f32))
            y_ref[b] = jnp.concatenate(ys, axis=1)

    @pl.when(c == 0)
    def _():
        run(uc_ref, yc_ref)

    @pl.when(c > 0)
    def _():
        run(ux_ref, yx_ref)


def _s5_scan(px_c, px_x, lam_re, lam_im, log_step, bt_re, bt_im, ct_re, ct_im, n_batch):
    t_len = px_c.shape[1]
    seq = px_x.shape[1]
    assert seq % t_len == 0
    nch = seq // t_len
    n_state = lam_re.shape[-1]
    half = n_state // 2
    pitch = t_len + SUBLANES
    assert t_len % (2 * SUBLANES) == 0

    def x_chunk(d, c):
        return jnp.where(d == 0, jnp.maximum(c - 1, 0), jnp.minimum(nch - c, nch - 1))

    vec = pl.BlockSpec((None, 1, n_state), lambda d, c: (d, 0, 0))
    mat = pl.BlockSpec((None, S5_GROUP, n_state), lambda d, c: (d, 0, 0))
    return pl.pallas_call(
        _s5_kernel,
        grid=(2, nch + 1),
        in_specs=[pl.BlockSpec((n_batch, t_len, SEC), lambda d, c: (0, 0, 3)),
                  pl.BlockSpec((n_batch, t_len, SEC), lambda d, c: (0, x_chunk(d, c), 3)),
                  vec, vec, vec, mat, mat, mat, mat],
        out_specs=[pl.BlockSpec((None, n_batch, t_len, SEC), lambda d, c: (d, 0, 0, 0)),
                   pl.BlockSpec((None, n_batch, t_len, SEC), lambda d, c: (d, 0, x_chunk(d, c), 0))],
        out_shape=[jax.ShapeDtypeStruct((2, n_batch, t_len, SEC), f32),
                   jax.ShapeDtypeStruct((2, n_batch, seq, SEC), f32)],
        scratch_shapes=[pltpu.VMEM((SEC, n_state), bf16)] * 4
        + [pltpu.VMEM((half // LANES, 2 * n_batch, LANES), f32)] * 4
        + [pltpu.VMEM((half // LANES, 2 * n_batch * pitch, LANES), f32)] * 2,
        compiler_params=_cp("arbitrary", "arbitrary"),
        name="s5_scan",
    )(px_c, px_x, lam_re, lam_im, log_step, bt_re, bt_im, ct_re, ct_im)


def _s5_glu_kernel(yf_ref, yb_ref, u_ref, d_ref, w_ref, o_ref):
    y = d_ref[...] * u_ref[...] + yf_ref[...] + yb_ref[...]
    g = jax.nn.gelu(y)
    z = jnp.dot(g.astype(bf16), w_ref[...], preferred_element_type=f32)
    o_ref[...] = (g * jax.nn.sigmoid(z)).astype(o_ref.dtype)


def _s5_glu(y, px, s5_d, w_glu, tm):
    rows = px.shape[0]
    return pl.pallas_call(
        _s5_glu_kernel,
        grid=(rows // tm,),
        in_specs=[pl.BlockSpec((None, tm, SEC), lambda i: (0, i, 0)),
                  pl.BlockSpec((None, tm, SEC), lambda i: (1, i, 0)),
                  pl.BlockSpec((tm, SEC), lambda i: (i, 3)),
                  pl.BlockSpec((1, SEC), lambda i: (0, 0)),
                  pl.BlockSpec((SEC, SEC), lambda i: (0, 0))],
        out_specs=pl.BlockSpec((tm, SEC), lambda i: (i, 0)),
        out_shape=jax.ShapeDtypeStruct((rows, SEC), bf16),
        compiler_params=_cp("arbitrary"),
        name="s5_glu",
    )(y, y, px, s5_d, w_glu)


CONV_HALO = 16


def _conv_kernel(vp_ref, vc_ref, vn_ref, gp_ref, gc_ref, gn_ref, w_ref, b_ref, lg_ref, lb_ref, o_ref, ext_ref):
    j = pl.program_id(1)
    tc = vc_ref.shape[0]
    glu = lambda v, g: v[...] * jax.nn.sigmoid(g[...])
    ext_ref[pl.ds(0, CONV_HALO), :] = jnp.where(j > 0, glu(vp_ref, gp_ref), 0.0)
    ext_ref[pl.ds(CONV_HALO, tc), :] = glu(vc_ref, gc_ref)
    ext_ref[pl.ds(CONV_HALO + tc, CONV_HALO), :] = jnp.where(j < pl.num_programs(1) - 1, glu(vn_ref, gn_ref), 0.0)
    y = jnp.zeros((tc, vc_ref.shape[1]), f32)
    for k in range(CONV_K):
        y += w_ref[pl.ds(k, 1), :] * ext_ref[pl.ds(CONV_HALO - CONV_K // 2 + k, tc), :]
    y = y + b_ref[...]
    mu = jnp.mean(y, axis=-1, keepdims=True)
    var = jnp.mean(jnp.square(y - mu), axis=-1, keepdims=True)
    z = (y - mu) * lax.rsqrt(var + NORM_EPS) * lg_ref[...] + lb_ref[...]
    o_ref[...] = jax.nn.silu(z).astype(o_ref.dtype)


def _conv(px, conv_w, conv_b, ln_g, ln_b, n_batch, tc):
    rows = px.shape[0]
    seq = rows // n_batch
    nt = seq // tc
    hpt = tc // CONV_HALO
    cur = lambda s: pl.BlockSpec((tc, SEC), lambda b, j: (b * nt + j, s))
    prev = lambda s: pl.BlockSpec((CONV_HALO, SEC), lambda b, j: (jnp.maximum((b * nt + j) * hpt - 1, 0), s))
    nxt = lambda s: pl.BlockSpec(
        (CONV_HALO, SEC), lambda b, j: (jnp.minimum((b * nt + j + 1) * hpt, rows // CONV_HALO - 1), s))
    full = lambda a: pl.BlockSpec(a.shape, lambda b, j: (0,) * a.ndim)
    return pl.pallas_call(
        _conv_kernel,
        grid=(n_batch, nt),
        in_specs=[prev(6), cur(6), nxt(6), prev(7), cur(7), nxt(7),
                  full(conv_w), full(conv_b), full(ln_g), full(ln_b)],
        out_specs=pl.BlockSpec((tc, SEC), lambda b, j: (b * nt + j, 0)),
        out_shape=jax.ShapeDtypeStruct((rows, SEC), bf16),
        scratch_shapes=[pltpu.VMEM((tc + 2 * CONV_HALO, SEC), f32)],
        compiler_params=_cp("arbitrary", "arbitrary"),
        name="conv",
    )(px, px, px, px, px, px, conv_w, conv_b, ln_g, ln_b)


def kernel(x, c, ctx, c_ctx, w_ada, b_ada, norm_pre_mix, norm_post_mix, norm_pre_ffn, norm_post_ffn, w_in, w_out, da_lam_q1, da_lam_k1, da_lam_q2, da_lam_k2, da_subln, s5_lam_re, s5_lam_im, s5_log_step, s5_b_re, s5_b_im, s5_c_re, s5_c_im, s5_d, s5_w_glu, mla_q_norm, mla_kv_norm, mla_w_uq, mla_w_ukv, conv_w, conv_b, conv_ln_g, conv_ln_b, w_ffn_in, w_ffn_out):
    n_batch, seq, d = x.shape
    ctx_len = ctx.shape[1]
    depth = w_in.shape[0]
    n_state = s5_lam_re.shape[2] * s5_lam_re.shape[3]
    mla_end = 4 * SEC + SEC + MLA_KV_RANK + ROT_DIM
    assert d == 4 * SEC and w_in.shape[2] == mla_end + 2 * SEC

    da_scale = ROT_DIM ** -0.5 * math.log2(math.e)
    mla_scale = (HEAD_W + ROT_DIM) ** -0.5 * math.log2(math.e)

    tm_x = _row_tile(seq, 1024)
    tm_c = _row_tile(n_batch * ctx_len, 1024)
    tpb = seq // tm_x
    grp_x = lambda i: i // tpb
    grp_c = lambda i: n_batch
    tq = _row_tile(seq, 256)
    tk = _row_tile(seq, 512)

    n_grp = -(-(n_batch + 1) // SUBLANES) * SUBLANES
    cc = jnp.concatenate([c, c_ctx[None], jnp.zeros((n_grp - n_batch - 1, d), f32)], axis=0)
    mods = _ada(cc, w_ada, b_ada).reshape(depth, n_grp, 1, 6 * d)

    cos_x, sin_x = _rope_tables(seq)
    cos_c, sin_c = jnp.ones((tm_c, LANES), f32), jnp.zeros((tm_c, LANES), f32)

    w_in_p = jnp.concatenate([w_in[:, :, :mla_end], jnp.zeros((depth, d, N_SEC * SEC - w_in.shape[2]), f32),
                              w_in[:, :, mla_end:]], axis=2).astype(bf16)
    w_out_b = w_out.astype(bf16)
    w_ffn_in_b = w_ffn_in.astype(bf16)
    w_ffn_out_b = w_ffn_out.astype(bf16)
    w_glu_b = s5_w_glu.astype(bf16)
    n_mla = SEC // HEAD_W
    wq = mla_w_uq.reshape(depth, SEC, n_mla, HEAD_W + ROT_DIM)
    wq = jnp.pad(wq, ((0, 0), (0, 0), (0, 0), (0, MLA_QK - HEAD_W - ROT_DIM))).reshape(depth, SEC, n_mla * MLA_QK)
    wq = wq.astype(bf16)
    wkv = mla_w_ukv.reshape(depth, MLA_KV_RANK, n_mla, 2, HEAD_W).transpose(0, 1, 3, 2, 4)
    wkv = wkv.reshape(depth, MLA_KV_RANK, 2 * n_mla * HEAD_W).astype(bf16)

    flat = lambda a: a.reshape(depth, 2, 1, n_state)
    s5_lre, s5_lim = flat(s5_lam_re), flat(s5_lam_im)
    s5_lst = flat(jnp.broadcast_to(s5_log_step[..., None], s5_lam_re.shape))
    s5_btr = s5_b_re.transpose(0, 1, 4, 2, 3).reshape(depth, 2, S5_GROUP, n_state)
    s5_bti = s5_b_im.transpose(0, 1, 4, 2, 3).reshape(depth, 2, S5_GROUP, n_state)
    s5_ctr = s5_c_re.transpose(0, 1, 3, 2, 4).reshape(depth, 2, S5_GROUP, n_state)
    s5_cti = s5_c_im.transpose(0, 1, 3, 2, 4).reshape(depth, 2, S5_GROUP, n_state)

    row = lambda a, l: a[l][None]
    conv_w_p = jnp.pad(conv_w, ((0, 0), (0, 1), (0, 0)))

    xs = x.reshape(n_batch * seq, d)
    cs = ctx.reshape(n_batch * ctx_len, d)
    for l in range(depth):
        last = l == depth - 1
        lambda_init = 0.8 - 0.6 * math.exp(-0.3 * l)
        g_pre = row(norm_pre_mix, l)

        px = _in_proj(xs, g_pre, mods, l, grp_x, w_in_p[l], tm_x, SEC)
        pc = _in_proj(cs, g_pre, mods, l, grp_c, w_in_p[l], tm_c, SEC)

        qx, kx, vx = _da_prep(px, cos_x, sin_x, tpb, tm_x, da_scale)
        qc, kc, vc = _da_prep(pc, cos_c, sin_c, 1, tm_c, da_scale)
        da_params = (row(da_lam_q1, l), row(da_lam_k1, l), row(da_lam_q2, l), row(da_lam_k2, l), row(da_subln, l))
        y_da_x = _attention(qx, kc, vc, kx, vx, da_params, n_batch, ctx_len, tq, tk, lambda_init)

        mla_w = (row(mla_q_norm, l), row(mla_kv_norm, l), wq[l], wkv[l])
        mqx, mkx, mvx = _mla_prep(px, *mla_w, cos_x, sin_x, tpb, tm_x, mla_scale)
        mqc, mkc, mvc = _mla_prep(pc, *mla_w, cos_c, sin_c, 1, tm_c, mla_scale)
        y_mla_x = _attention(mqx, mkc, mvc, mkx, mvx, None, n_batch, ctx_len, tq, tk, lambda_init)

        ys_c, ys_x = _s5_scan(pc.reshape(n_batch, ctx_len, -1), px.reshape(n_batch, seq, -1),
                              s5_lre[l], s5_lim[l], s5_lst[l], s5_btr[l], s5_bti[l], s5_ctr[l], s5_cti[l], n_batch)
        y_s5_x = _s5_glu(ys_x.reshape(2, n_batch * seq, SEC), px, row(s5_d, l), w_glu_b[l], tm_x)

        cv = (conv_w_p[l], row(conv_b, l), row(conv_ln_g, l), row(conv_ln_b, l))
        y_cv_x = _conv(px, *cv, n_batch, _row_tile(seq, 512))

        g_post, g_pre_f, g_post_f = row(norm_post_mix, l), row(norm_pre_ffn, l), row(norm_post_ffn, l)
        tm_o = _row_tile(tm_x, 512)
        x1 = _mix_out((y_da_x, y_s5_x, y_mla_x, y_cv_x), w_out_b[l], xs, g_post, mods, l,
                      lambda i: i // (seq // tm_o), tm_o)
        act = _ffn_in(x1, g_pre_f, mods, l, grp_x, w_ffn_in_b[l], tm_x, SEC)
        xs = _ffn_out(act, w_ffn_out_b[l], x1, g_post_f, mods, l, lambda i: i // (seq // tm_o), tm_o, SEC)

        if not last:
            y_da_c = _attention(qc, kc, vc, None, None, da_params, n_batch, ctx_len, ctx_len, tk, lambda_init)
            y_mla_c = _attention(mqc, mkc, mvc, None, None, None, n_batch, ctx_len, ctx_len, tk, lambda_init)
            y_s5_c = _s5_glu(ys_c.reshape(2, n_batch * ctx_len, SEC), pc, row(s5_d, l), w_glu_b[l], tm_c)
            y_cv_c = _conv(pc, *cv, n_batch, ctx_len)
            tm_oc = _row_tile(tm_c, 512)
            c1 = _mix_out((y_da_c, y_s5_c, y_mla_c, y_cv_c), w_out_b[l], cs, g_post, mods, l, grp_c, tm_oc)
            act_c = _ffn_in(c1, g_pre_f, mods, l, grp_c, w_ffn_in_b[l], tm_c, SEC)
            cs = _ffn_out(act_c, w_ffn_out_b[l], c1, g_post_f, mods, l, grp_c, tm_oc, SEC)
    return xs.reshape(n_batch, seq, d)
```

```python
import functools
import math

import jax
import jax.numpy as jnp
from jax import lax
from jax.experimental import pallas as pl
from jax.experimental.pallas import tpu as pltpu

f32 = jnp.float32
bf16 = jnp.bfloat16

NORM_EPS = 1e-6
ROPE_BASE = 10000.0
GRID_W = 64
ROT_DIM = 64
CONV_K = 31
S5_GROUP = 16
S5_STATE = 64

LANES = 128
SUBLANES = 8
SEC = 512
N_SEC = 8
HEAD_W = 128
MLA_QK = 256
MLA_KV_RANK = 256
V7X_VMEM_BYTES = 64 * 1024 * 1024
VMEM_LIMIT = V7X_VMEM_BYTES * 7 // 8


def _cp(*sem):
    return pltpu.CompilerParams(dimension_semantics=sem, vmem_limit_bytes=VMEM_LIMIT)


def _row_tile(rows, want):
    t = min(rows, want)
    assert rows % t == 0, (rows, want)
    return t


def _ada_kernel(cc_ref, w_ref, b_ref, o_ref):
    s = jax.nn.silu(cc_ref[...]).astype(bf16)
    o_ref[...] = jnp.dot(s, w_ref[...].astype(bf16), preferred_element_type=f32) + b_ref[...]


def _ada(cc, w_ada, b_ada):
    depth, d, n = w_ada.shape
    g = cc.shape[0]
    tn = 512
    return pl.pallas_call(
        _ada_kernel,
        grid=(depth, n // tn),
        in_specs=[pl.BlockSpec((g, d), lambda l, j: (0, 0)),
                  pl.BlockSpec((None, d, tn), lambda l, j: (l, 0, j)),
                  pl.BlockSpec((None, 1, tn), lambda l, j: (l, 0, j))],
        out_specs=pl.BlockSpec((None, g, tn), lambda l, j: (l, 0, j)),
        out_shape=jax.ShapeDtypeStruct((depth, g, n), f32),
        compiler_params=_cp("arbitrary", "arbitrary"),
        name="ada",
    )(cc, w_ada, b_ada.reshape(depth, 1, n))


def _norm_mod(x, g, shift, scale):
    ms = jnp.mean(x * x, axis=-1, keepdims=True)
    return (x * lax.rsqrt(ms + NORM_EPS) * g) * (1.0 + scale) + shift


def _in_proj_kernel(x_ref, g_ref, sh_ref, sc_ref, w_ref, o_ref, h_ref):
    @pl.when(pl.program_id(1) == 0)
    def _():
        h_ref[...] = _norm_mod(x_ref[...], g_ref[...], sh_ref[...], sc_ref[...]).astype(bf16)

    o_ref[...] = jnp.dot(h_ref[...], w_ref[...], preferred_element_type=f32).astype(o_ref.dtype)


def _ffn_in_kernel(x_ref, g_ref, sh_ref, sc_ref, wg_ref, wu_ref, o_ref, h_ref):
    @pl.when(pl.program_id(1) == 0)
    def _():
        h_ref[...] = _norm_mod(x_ref[...], g_ref[...], sh_ref[...], sc_ref[...]).astype(bf16)

    h = h_ref[...]
    a = jnp.dot(h, wg_ref[...], preferred_element_type=f32)
    b = jnp.dot(h, wu_ref[...], preferred_element_type=f32)
    o_ref[...] = (jax.nn.silu(a) * b).astype(o_ref.dtype)


def _mod_spec(l, grp, k, d):
    return pl.BlockSpec((None, None, 1, d), lambda i, j: (l, grp(i), 0, k))


def _in_proj(x, g, mods, l, grp, w, tm, tn):
    rows, d = x.shape
    n = w.shape[2]
    return pl.pallas_call(
        _in_proj_kernel,
        grid=(rows // tm, n // tn),
        in_specs=[pl.BlockSpec((tm, d), lambda i, j: (i, 0)),
                  pl.BlockSpec((1, d), lambda i, j: (0, 0)),
                  _mod_spec(l, grp, 0, d), _mod_spec(l, grp, 1, d),
                  pl.BlockSpec((None, d, tn), lambda i, j: (l, 0, j))],
        out_specs=pl.BlockSpec((tm, tn), lambda i, j: (i, j)),
        out_shape=jax.ShapeDtypeStruct((rows, n), f32),
        scratch_shapes=[pltpu.VMEM((tm, d), bf16)],
        compiler_params=_cp("arbitrary", "arbitrary"),
        name="in_proj",
    )(x, g, mods, mods, w)


def _ffn_in(x, g, mods, l, grp, w, tm, tn):
    rows, d = x.shape
    dff = w.shape[2] // 2
    nj = dff // tn
    return pl.pallas_call(
        _ffn_in_kernel,
        grid=(rows // tm, nj),
        in_specs=[pl.BlockSpec((tm, d), lambda i, j: (i, 0)),
                  pl.BlockSpec((1, d), lambda i, j: (0, 0)),
                  _mod_spec(l, grp, 3, d), _mod_spec(l, grp, 4, d),
                  pl.BlockSpec((None, d, tn), lambda i, j: (l, 0, j)),
                  pl.BlockSpec((None, d, tn), lambda i, j: (l, 0, j + nj))],
        out_specs=pl.BlockSpec((tm, tn), lambda i, j: (i, j)),
        out_shape=jax.ShapeDtypeStruct((rows, dff), bf16),
        scratch_shapes=[pltpu.VMEM((tm, d), bf16)],
        compiler_params=_cp("arbitrary", "arbitrary"),
        name="ffn_in",
    )(x, g, mods, mods, w, w)


def _residual(acc, x, g, gate):
    ms = jnp.mean(acc * acc, axis=-1, keepdims=True)
    return x + gate * (acc * lax.rsqrt(ms + NORM_EPS) * g)


def _mix_out_kernel(y0_ref, y1_ref, y2_ref, y3_ref, w_ref, x_ref, g_ref, gate_ref, o_ref):
    tk = y0_ref.shape[1]
    acc = None
    for s, y_ref in enumerate((y0_ref, y1_ref, y2_ref, y3_ref)):
        part = jnp.dot(y_ref[...], w_ref[s * tk:(s + 1) * tk, :], preferred_element_type=f32)
        acc = part if acc is None else acc + part
    o_ref[...] = _residual(acc, x_ref[...], g_ref[...], gate_ref[...])


def _ffn_out_kernel(y_ref, w_ref, x_ref, g_ref, gate_ref, o_ref, acc_ref):
    k = pl.program_id(1)

    @pl.when(k == 0)
    def _():
        acc_ref[...] = jnp.zeros_like(acc_ref)

    acc_ref[...] += jnp.dot(y_ref[...], w_ref[...], preferred_element_type=f32)

    @pl.when(k == pl.num_programs(1) - 1)
    def _():
        o_ref[...] = _residual(acc_ref[...], x_ref[...], g_ref[...], gate_ref[...])


def _mix_out(ys, w, x, g, mods, l, grp, tm):
    rows, d = x.shape
    tk = ys[0].shape[1]
    mod = _mod_spec(l, grp, 2, d)
    return pl.pallas_call(
        _mix_out_kernel,
        grid=(rows // tm,),
        in_specs=[pl.BlockSpec((tm, tk), lambda i: (i, 0))] * len(ys) + [
            pl.BlockSpec((None, len(ys) * tk, d), lambda i: (l, 0, 0)),
            pl.BlockSpec((tm, d), lambda i: (i, 0)),
            pl.BlockSpec((1, d), lambda i: (0, 0)),
            pl.BlockSpec(mod.block_shape, lambda i: mod.index_map(i, 0))],
        out_specs=pl.BlockSpec((tm, d), lambda i: (i, 0)),
        out_shape=jax.ShapeDtypeStruct((rows, d), f32),
        compiler_params=_cp("arbitrary"),
        name="mix_out",
    )(*ys, w, x, g, mods)


def _ffn_out(y, w, x, g, mods, l, grp, tm, tk):
    rows, d = x.shape
    return pl.pallas_call(
        _ffn_out_kernel,
        grid=(rows // tm, y.shape[1] // tk),
        in_specs=[pl.BlockSpec((tm, tk), lambda i, k: (i, k)),
                  pl.BlockSpec((None, tk, d), lambda i, k: (l, k, 0)),
                  pl.BlockSpec((tm, d), lambda i, k: (i, 0)),
                  pl.BlockSpec((1, d), lambda i, k: (0, 0)),
                  _mod_spec(l, grp, 5, d)],
        out_specs=pl.BlockSpec((tm, d), lambda i, k: (i, 0)),
        out_shape=jax.ShapeDtypeStruct((rows, d), f32),
        scratch_shapes=[pltpu.VMEM((tm, d), f32)],
        compiler_params=_cp("arbitrary", "arbitrary"),
        name="ffn_out",
    )(y, w, x, g, mods)


def _rope_tables(n_tokens):
    n_rows = n_tokens // GRID_W
    row = jnp.repeat(jnp.arange(n_rows, dtype=f32), GRID_W)
    col = jnp.tile(jnp.arange(GRID_W, dtype=f32), n_rows)
    n_freq = ROT_DIM // 4
    inv = ROPE_BASE ** (-jnp.arange(n_freq, dtype=f32) / n_freq)
    ang = jnp.concatenate([row[:, None] * inv, col[:, None] * inv], axis=-1)
    cos, sin = jnp.cos(ang), jnp.sin(ang)
    return jnp.tile(cos, (1, 4)), jnp.tile(jnp.concatenate([-sin, sin], axis=-1), (1, 2))


def _rope_slab(x, cos, sin):
    lane = lax.broadcasted_iota(jnp.int32, x.shape, 1)
    partner = jnp.where(lane % ROT_DIM < ROT_DIM // 2,
                        pltpu.roll(x, LANES - ROT_DIM // 2, 1), pltpu.roll(x, ROT_DIM // 2, 1))
    return x * cos + partner * sin


def _da_prep_kernel(q_ref, k_ref, v_ref, cos_ref, sin_ref, qo_ref, ko_ref, vo_ref, *, scale):
    cos, sin = cos_ref[...], sin_ref[...]
    for h in range(SEC // LANES):
        sl = slice(h * LANES, (h + 1) * LANES)
        qo_ref[:, sl] = (_rope_slab(q_ref[:, sl], cos, sin) * scale).astype(bf16)
        ko_ref[:, sl] = _rope_slab(k_ref[:, sl], cos, sin).astype(bf16)
    vo_ref[...] = v_ref[...].astype(bf16)


def _da_prep(px, cos, sin, pos_tiles, tm, scale):
    rows = px.shape[0]
    sec = lambda s: pl.BlockSpec((tm, SEC), lambda i: (i, s))
    tab = pl.BlockSpec((tm, LANES), lambda i: (i % pos_tiles, 0))
    out = jax.ShapeDtypeStruct((rows, SEC), bf16)
    return pl.pallas_call(
        functools.partial(_da_prep_kernel, scale=scale),
        grid=(rows // tm,),
        in_specs=[sec(0), sec(1), sec(2), tab, tab],
        out_specs=[pl.BlockSpec((tm, SEC), lambda i: (i, 0))] * 3,
        out_shape=[out, out, out],
        compiler_params=_cp("arbitrary"),
        name="da_prep",
    )(px, px, px, cos, sin)


def _rms_g(x, g):
    ms = jnp.mean(x * x, axis=-1, keepdims=True)
    return x * lax.rsqrt(ms + NORM_EPS) * g


def _mla_prep_kernel(cq_ref, ckv_ref, gq_ref, gkv_ref, wq_ref, wkv_ref, cos_ref, sin_ref,
                     qo_ref, ko_ref, vo_ref, *, scale):
    cos, sin = cos_ref[...], sin_ref[...]
    n_heads = SEC // HEAD_W
    q = jnp.dot(_rms_g(cq_ref[...], gq_ref[...]).astype(bf16), wq_ref[...], preferred_element_type=f32)
    kv = jnp.dot(_rms_g(ckv_ref[:, :MLA_KV_RANK], gkv_ref[...]).astype(bf16), wkv_ref[...],
                 preferred_element_type=f32)
    kr = _rope_slab(ckv_ref[:, MLA_KV_RANK:MLA_KV_RANK + LANES], cos, sin).astype(bf16)
    for h in range(n_heads):
        base = h * MLA_QK
        qo_ref[:, base:base + LANES] = (q[:, base:base + LANES] * scale).astype(bf16)
        qo_ref[:, base + LANES:base + MLA_QK] = (
            _rope_slab(q[:, base + LANES:base + MLA_QK], cos, sin) * scale).astype(bf16)
        ko_ref[:, base:base + LANES] = kv[:, h * LANES:(h + 1) * LANES].astype(bf16)
        ko_ref[:, base + LANES:base + MLA_QK] = kr
    vo_ref[...] = kv[:, n_heads * LANES:].astype(bf16)


def _mla_prep(px, gq, gkv, wq, wkv, cos, sin, pos_tiles, tm, scale):
    rows = px.shape[0]
    n_heads = SEC // HEAD_W
    sec = lambda s: pl.BlockSpec((tm, SEC), lambda i: (i, s))
    tab = pl.BlockSpec((tm, LANES), lambda i: (i % pos_tiles, 0))
    full = lambda a: pl.BlockSpec(a.shape, lambda i: (0,) * a.ndim)
    return pl.pallas_call(
        functools.partial(_mla_prep_kernel, scale=scale),
        grid=(rows // tm,),
        in_specs=[sec(4), sec(5), full(gq), full(gkv), full(wq), full(wkv), tab, tab],
        out_specs=[pl.BlockSpec((tm, n_heads * MLA_QK), lambda i: (i, 0)),
                   pl.BlockSpec((tm, n_heads * MLA_QK), lambda i: (i, 0)),
                   pl.BlockSpec((tm, SEC), lambda i: (i, 0))],
        out_shape=[jax.ShapeDtypeStruct((rows, n_heads * MLA_QK), bf16),
                   jax.ShapeDtypeStruct((rows, n_heads * MLA_QK), bf16),
                   jax.ShapeDtypeStruct((rows, SEC), bf16)],
        compiler_params=_cp("arbitrary"),
        name="mla_prep",
    )(px, px, gq, gkv, wq, wkv, cos, sin)


def _attn_kernel(*refs, diff, has_x_keys, tk, lambda_init):
    refs = list(refs)
    p_ref = refs.pop()
    s_ref = refs.pop()
    o_ref = refs.pop()
    q_ref, kc_ref, vc_ref = refs[:3]
    refs = refs[3:]
    chunks = [(kc_ref, vc_ref, 0, kc_ref.shape[0])]
    if has_x_keys:
        kx_ref, vx_ref = refs[:2]
        refs = refs[2:]
        chunks += [(kx_ref, vx_ref, c * tk, tk) for c in range(kx_ref.shape[0] // tk)]
    q = q_ref[...]
    tq = q.shape[0]
    if diff:
        lane = lax.broadcasted_iota(jnp.int32, q.shape, 1)
        zero = jnp.zeros_like(q)
        qs = [jnp.where(lane < ROT_DIM, q, zero), jnp.where(lane >= ROT_DIM, q, zero)]
    else:
        qs = [q]
    lane_tiles = lambda a: [a[:, j * LANES:(j + 1) * LANES] for j in range(a.shape[1] // LANES)]

    m_acc = [jnp.full((tq, LANES), -jnp.inf, f32) for _ in qs]
    col = 0
    for k_ref, _, start, size in chunks:
        k = k_ref[pl.ds(start, size), :]
        for mi, qm in enumerate(qs):
            s = lax.dot_general(qm, k, (((1,), (1,)), ((), ())), preferred_element_type=f32)
            s_ref[mi, :, col:col + size] = s
            for t in lane_tiles(s):
                m_acc[mi] = jnp.maximum(m_acc[mi], t)
        col += size
    m_row = [jnp.broadcast_to(jnp.max(m, axis=-1, keepdims=True), (tq, LANES)) for m in m_acc]

    l_acc = [jnp.zeros((tq, LANES), f32) for _ in qs]
    col = 0
    for _, _, _, size in chunks:
        for mi in range(len(qs)):
            p = jnp.exp2(s_ref[mi, :, col:col + size] - jnp.tile(m_row[mi], (1, size // LANES)))
            for t in lane_tiles(p):
                l_acc[mi] = l_acc[mi] + t
            p_ref[mi, :, col:col + size] = p.astype(bf16)
        col += size
    n_ctx = kc_ref.shape[0]
    acc = []
    for mi in range(len(qs)):
        a = jnp.dot(p_ref[mi, :, :n_ctx], vc_ref[...], preferred_element_type=f32)
        if has_x_keys:
            a = a + jnp.dot(p_ref[mi, :, n_ctx:], vx_ref[...], preferred_element_type=f32)
        acc.append(a)

    outs = [a / jnp.sum(l, axis=-1, keepdims=True) for a, l in zip(acc, l_acc)]
    if diff:
        lq1, lk1, lq2, lk2, subln = (r[...] for r in refs)
        lam = (jnp.exp(jnp.sum(lq1 * lk1, axis=-1, keepdims=True))
               - jnp.exp(jnp.sum(lq2 * lk2, axis=-1, keepdims=True)) + lambda_init)
        dlt = outs[0] - lam * outs[1]
        o_ref[...] = (_rms_g(dlt, subln) * (1.0 - lambda_init)).astype(o_ref.dtype)
    else:
        o_ref[...] = outs[0].astype(o_ref.dtype)


def _attention(q, kc, vc, kx, vx, params, n_batch, ctx_len, tq, tk, lambda_init):
    n_heads = vc.shape[1] // HEAD_W
    dq = q.shape[1] // n_heads
    lq = q.shape[0] // n_batch
    nq = lq // tq
    diff = params is not None
    in_specs = [pl.BlockSpec((tq, dq), lambda b, h, i: (b * nq + i, h)),
                pl.BlockSpec((ctx_len, dq), lambda b, h, i: (b, h)),
                pl.BlockSpec((ctx_len, HEAD_W), lambda b, h, i: (b, h))]
    args = [q, kc, vc]
    if kx is not None:
        lx = kx.shape[0] // n_batch
        in_specs += [pl.BlockSpec((lx, dq), lambda b, h, i: (b, h)),
                     pl.BlockSpec((lx, HEAD_W), lambda b, h, i: (b, h))]
        args += [kx, vx]
    n_keys = ctx_len + (kx.shape[0] // n_batch if kx is not None else 0)
    if diff:
        in_specs += [pl.BlockSpec(p.shape, lambda b, h, i: (0, 0)) for p in params]
        args += list(params)
    return pl.pallas_call(
        functools.partial(_attn_kernel, diff=diff, has_x_keys=kx is not None, tk=tk, lambda_init=lambda_init),
        grid=(n_batch, n_heads, nq),
        in_specs=in_specs,
        out_specs=pl.BlockSpec((tq, HEAD_W), lambda b, h, i: (b * nq + i, h)),
        out_shape=jax.ShapeDtypeStruct((q.shape[0], n_heads * HEAD_W), bf16),
        scratch_shapes=[pltpu.VMEM((2 if diff else 1, tq, n_keys), f32),
                        pltpu.VMEM((2 if diff else 1, tq, n_keys), bf16)],
        compiler_params=_cp("arbitrary", "arbitrary", "arbitrary"),
        name="attention",
    )(*args)


def _s5_kernel(uc_ref, ux_ref, lre_ref, lim_ref, lst_ref, btr_ref, bti_ref, ctr_ref, cti_ref,
               yc_ref, yx_ref, wb_re, wb_im, wc_re, wc_im, a_re, a_im, h_re, h_im, s_re, s_im):
    d = pl.program_id(0)
    c = pl.program_id(1)
    n_batch, t_len, width = uc_ref.shape
    n_state = lre_ref.shape[1]
    half = n_state // 2
    n_groups = width // S5_GROUP
    n_blk = half // LANES

    @pl.when(c == 0)
    def _():
        lre, lim = lre_ref[...], lim_ref[...]
        step = jnp.exp(lst_ref[...])
        mag = jnp.exp(lre * step)
        are = mag * jnp.cos(lim * step)
        aim = mag * jnp.sin(lim * step)
        den = lre * lre + lim * lim
        fre = ((are - 1.0) * lre + aim * lim) / den
        fim = (aim * lre - (are - 1.0) * lim) / den
        row = lax.broadcasted_iota(jnp.int32, (width, n_state), 0) // S5_GROUP
        col = lax.broadcasted_iota(jnp.int32, (width, n_state), 1) // S5_STATE
        blk = row == col
        tile = lambda r: jnp.broadcast_to(r[...][None], (n_groups, S5_GROUP, n_state)).reshape(width, n_state)
        btr, bti = tile(btr_ref), tile(bti_ref)
        wb_re[...] = jnp.where(blk, fre * btr - fim * bti, 0.0).astype(bf16)
        wb_im[...] = jnp.where(blk, fre * bti + fim * btr, 0.0).astype(bf16)
        wc_re[...] = jnp.where(blk, tile(ctr_ref), 0.0).astype(bf16)
        wc_im[...] = jnp.where(blk, -tile(cti_ref), 0.0).astype(bf16)
        for hf in range(2):
            rows = slice(hf * n_batch, (hf + 1) * n_batch)
            for j in range(n_blk):
                cols = slice(hf * half + j * LANES, hf * half + (j + 1) * LANES)
                a_re[j, rows, :] = jnp.broadcast_to(are[:, cols], (n_batch, LANES))
                a_im[j, rows, :] = jnp.broadcast_to(aim[:, cols], (n_batch, LANES))
        h_re[...] = jnp.zeros_like(h_re)
        h_im[...] = jnp.zeros_like(h_im)

    pitch = s_re.shape[1] // (2 * n_batch)
    kb = LANES
    sb = kb // S5_GROUP * S5_STATE
    nt = (((1,), (1,)), ((), ()))

    def slab(col0, b):
        hf, rem = divmod(col0, half)
        return rem // LANES, pl.ds((hf * n_batch + b) * pitch, t_len)

    def run(u_ref, y_ref):
        for b in range(n_batch):
            u = u_ref[b].astype(bf16)
            for bb in range(width // kb):
                ub = u[:, bb * kb:(bb + 1) * kb]
                bre = jnp.dot(ub, wb_re[bb * kb:(bb + 1) * kb, bb * sb:(bb + 1) * sb], preferred_element_type=f32)
                bim = jnp.dot(ub, wb_im[bb * kb:(bb + 1) * kb, bb * sb:(bb + 1) * sb], preferred_element_type=f32)
                for jj in range(sb // LANES):
                    j, rows = slab(bb * sb + jj * LANES, b)
                    s_re[j, rows, :] = bre[:, jj * LANES:(jj + 1) * LANES]
                    s_im[j, rows, :] = bim[:, jj * LANES:(jj + 1) * LANES]

        ar = [a_re[j] for j in range(n_blk)]
        ai = [a_im[j] for j in range(n_blk)]

        def step(k, carry):
            hr, hi = carry
            t = k + d * (t_len - 1 - 2 * k)
            rows = pl.ds(t, 2 * n_batch, stride=pitch)
            nr, ni = [], []
            for j in range(n_blk):
                nr.append(ar[j] * hr[j] - ai[j] * hi[j] + s_re[j, rows, :])
                ni.append(ar[j] * hi[j] + ai[j] * hr[j] + s_im[j, rows, :])
                s_re[j, rows, :] = nr[j]
                s_im[j, rows, :] = ni[j]
            return nr, ni

        init = ([h_re[j] for j in range(n_blk)], [h_im[j] for j in range(n_blk)])
        hr, hi = lax.fori_loop(0, t_len, step, init)
        for j in range(n_blk):
            h_re[j] = hr[j]
            h_im[j] = hi[j]

        for b in range(n_batch):
            ys = []
            for bb in range(width // kb):
                slabs = [slab(bb * sb + jj * LANES, b) for jj in range(sb // LANES)]
                sr = jnp.concatenate([s_re[j, rows, :] for j, rows in slabs], axis=1).astype(bf16)
                si = jnp.concatenate([s_im[j, rows, :] for j, rows in slabs], axis=1).astype(bf16)
                ys.append(
                    lax.dot_general(sr, wc_re[bb * kb:(bb + 1) * kb, bb * sb:(bb + 1) * sb], nt,
                                    preferred_element_type=f32)
                    + lax.dot_general(si, wc_im[bb * kb:(bb + 1) * kb, bb * sb:(bb + 1) * sb], nt,
                                      preferred_element_type=f32))
            y_ref[b] = jnp.concatenate(ys, axis=1)

    @pl.when(c == 0)
    def _():
        run(uc_ref, yc_ref)

    @pl.when(c > 0)
    def _():
        run(ux_ref, yx_ref)


def _s5_scan(px_c, px_x, lam_re, lam_im, log_step, bt_re, bt_im, ct_re, ct_im, n_batch):
    t_len = px_c.shape[1]
    seq = px_x.shape[1]
    assert seq % t_len == 0
    nch = seq // t_len
    n_state = lam_re.shape[-1]
    half = n_state // 2
    pitch = t_len + SUBLANES
    assert t_len % (2 * SUBLANES) == 0

    def x_chunk(d, c):
        return jnp.where(d == 0, jnp.maximum(c - 1, 0), jnp.minimum(nch - c, nch - 1))

    vec = pl.BlockSpec((None, 1, n_state), lambda d, c: (d, 0, 0))
    mat = pl.BlockSpec((None, S5_GROUP, n_state), lambda d, c: (d, 0, 0))
    return pl.pallas_call(
        _s5_kernel,
        grid=(2, nch + 1),
        in_specs=[pl.BlockSpec((n_batch, t_len, SEC), lambda d, c: (0, 0, 3)),
                  pl.BlockSpec((n_batch, t_len, SEC), lambda d, c: (0, x_chunk(d, c), 3)),
                  vec, vec, vec, mat, mat, mat, mat],
        out_specs=[pl.BlockSpec((None, n_batch, t_len, SEC), lambda d, c: (d, 0, 0, 0)),
                   pl.BlockSpec((None, n_batch, t_len, SEC), lambda d, c: (d, 0, x_chunk(d, c), 0))],
        out_shape=[jax.ShapeDtypeStruct((2, n_batch, t_len, SEC), f32),
                   jax.ShapeDtypeStruct((2, n_batch, seq, SEC), f32)],
        scratch_shapes=[pltpu.VMEM((SEC, n_state), bf16)] * 4
        + [pltpu.VMEM((half // LANES, 2 * n_batch, LANES), f32)] * 4
        + [pltpu.VMEM((half // LANES, 2 * n_batch * pitch, LANES), f32)] * 2,
        compiler_params=_cp("arbitrary", "arbitrary"),
        name="s5_scan",
    )(px_c, px_x, lam_re, lam_im, log_step, bt_re, bt_im, ct_re, ct_im)


def _s5_glu_kernel(yf_ref, yb_ref, u_ref, d_ref, w_ref, o_ref):
    y = d_ref[...] * u_ref[...] + yf_ref[...] + yb_ref[...]
    g = jax.nn.gelu(y)
    z = jnp.dot(g.astype(bf16), w_ref[...], preferred_element_type=f32)
    o_ref[...] = (g * jax.nn.sigmoid(z)).astype(o_ref.dtype)


def _s5_glu(y, px, s5_d, w_glu, tm):
    rows = px.shape[0]
    return pl.pallas_call(
        _s5_glu_kernel,
        grid=(rows // tm,),
        in_specs=[pl.BlockSpec((None, tm, SEC), lambda i: (0, i, 0)),
                  pl.BlockSpec((None, tm, SEC), lambda i: (1, i, 0)),
                  pl.BlockSpec((tm, SEC), lambda i: (i, 3)),
                  pl.BlockSpec((1, SEC), lambda i: (0, 0)),
                  pl.BlockSpec((SEC, SEC), lambda i: (0, 0))],
        out_specs=pl.BlockSpec((tm, SEC), lambda i: (i, 0)),
        out_shape=jax.ShapeDtypeStruct((rows, SEC), bf16),
        compiler_params=_cp("arbitrary"),
        name="s5_glu",
    )(y, y, px, s5_d, w_glu)


CONV_HALO = 16


def _conv_kernel(vp_ref, vc_ref, vn_ref, gp_ref, gc_ref, gn_ref, w_ref, b_ref, lg_ref, lb_ref, o_ref, ext_ref):
    j = pl.program_id(1)
    tc = vc_ref.shape[0]
    glu = lambda v, g: v[...] * jax.nn.sigmoid(g[...])
    ext_ref[0, pl.ds(0, CONV_HALO), :] = jnp.where(j > 0, glu(vp_ref, gp_ref), 0.0)
    ext_ref[0, pl.ds(CONV_HALO, tc), :] = glu(vc_ref, gc_ref)
    ext_ref[0, pl.ds(CONV_HALO + tc, CONV_HALO), :] = jnp.where(j < pl.num_programs(1) - 1, glu(vn_ref, gn_ref), 0.0)
    n_keep = tc + 2 * CONV_HALO - SUBLANES
    for s in range(1, SUBLANES):
        ext_ref[s, pl.ds(0, n_keep), :] = ext_ref[0, pl.ds(s, n_keep), :]
    y = jnp.zeros((tc, vc_ref.shape[1]), f32)
    for k in range(CONV_K):
        off = CONV_HALO - CONV_K // 2 + k
        y += w_ref[pl.ds(k, 1), :] * ext_ref[off % SUBLANES, pl.ds(off - off % SUBLANES, tc), :]
    y = y + b_ref[...]
    mu = jnp.mean(y, axis=-1, keepdims=True)
    var = jnp.mean(jnp.square(y - mu), axis=-1, keepdims=True)
    z = (y - mu) * lax.rsqrt(var + NORM_EPS) * lg_ref[...] + lb_ref[...]
    o_ref[...] = jax.nn.silu(z).astype(o_ref.dtype)


def _conv(px, conv_w, conv_b, ln_g, ln_b, n_batch, tc):
    rows = px.shape[0]
    seq = rows // n_batch
    nt = seq // tc
    hpt = tc // CONV_HALO
    cur = lambda s: pl.BlockSpec((tc, SEC), lambda b, j: (b * nt + j, s))
    prev = lambda s: pl.BlockSpec((CONV_HALO, SEC), lambda b, j: (jnp.maximum((b * nt + j) * hpt - 1, 0), s))
    nxt = lambda s: pl.BlockSpec(
        (CONV_HALO, SEC), lambda b, j: (jnp.minimum((b * nt + j + 1) * hpt, rows // CONV_HALO - 1), s))
    full = lambda a: pl.BlockSpec(a.shape, lambda b, j: (0,) * a.ndim)
    return pl.pallas_call(
        _conv_kernel,
        grid=(n_batch, nt),
        in_specs=[prev(6), cur(6), nxt(6), prev(7), cur(7), nxt(7),
                  full(conv_w), full(conv_b), full(ln_g), full(ln_b)],
        out_specs=pl.BlockSpec((tc, SEC), lambda b, j: (b * nt + j, 0)),
        out_shape=jax.ShapeDtypeStruct((rows, SEC), bf16),
        scratch_shapes=[pltpu.VMEM((SUBLANES, tc + 2 * CONV_HALO, SEC), f32)],
        compiler_params=_cp("arbitrary", "arbitrary"),
        name="conv",
    )(px, px, px, px, px, px, conv_w, conv_b, ln_g, ln_b)


def kernel(x, c, ctx, c_ctx, w_ada, b_ada, norm_pre_mix, norm_post_mix, norm_pre_ffn, norm_post_ffn, w_in, w_out, da_lam_q1, da_lam_k1, da_lam_q2, da_lam_k2, da_subln, s5_lam_re, s5_lam_im, s5_log_step, s5_b_re, s5_b_im, s5_c_re, s5_c_im, s5_d, s5_w_glu, mla_q_norm, mla_kv_norm, mla_w_uq, mla_w_ukv, conv_w, conv_b, conv_ln_g, conv_ln_b, w_ffn_in, w_ffn_out):
    n_batch, seq, d = x.shape
    ctx_len = ctx.shape[1]
    depth = w_in.shape[0]
    n_state = s5_lam_re.shape[2] * s5_lam_re.shape[3]
    mla_end = 4 * SEC + SEC + MLA_KV_RANK + ROT_DIM
    assert d == 4 * SEC and w_in.shape[2] == mla_end + 2 * SEC

    da_scale = ROT_DIM ** -0.5 * math.log2(math.e)
    mla_scale = (HEAD_W + ROT_DIM) ** -0.5 * math.log2(math.e)

    tm_x = _row_tile(seq, 1024)
    tm_c = _row_tile(n_batch * ctx_len, 1024)
    tpb = seq // tm_x
    grp_x = lambda i: i // tpb
    grp_c = lambda i: n_batch
    tq_da = _row_tile(seq, 512)
    tq_mla = _row_tile(seq, 256)
    tk = _row_tile(seq, 512)
    tk_f = w_ffn_out.shape[1] // 4
    assert tk_f % LANES == 0

    n_grp = -(-(n_batch + 1) // SUBLANES) * SUBLANES
    cc = jnp.concatenate([c, c_ctx[None], jnp.zeros((n_grp - n_batch - 1, d), f32)], axis=0)
    mods = _ada(cc, w_ada, b_ada).reshape(depth, n_grp, 1, 6 * d)

    cos_x, sin_x = _rope_tables(seq)
    cos_c, sin_c = jnp.ones((tm_c, LANES), f32), jnp.zeros((tm_c, LANES), f32)

    w_in_p = jnp.concatenate([w_in[:, :, :mla_end], jnp.zeros((depth, d, N_SEC * SEC - w_in.shape[2]), f32),
                              w_in[:, :, mla_end:]], axis=2).astype(bf16)
    w_out_b = w_out.astype(bf16)
    w_ffn_in_b = w_ffn_in.astype(bf16)
    w_ffn_out_b = w_ffn_out.astype(bf16)
    w_glu_b = s5_w_glu.astype(bf16)
    n_mla = SEC // HEAD_W
    wq = mla_w_uq.reshape(depth, SEC, n_mla, HEAD_W + ROT_DIM)
    wq = jnp.pad(wq, ((0, 0), (0, 0), (0, 0), (0, MLA_QK - HEAD_W - ROT_DIM))).reshape(depth, SEC, n_mla * MLA_QK)
    wq = wq.astype(bf16)
    wkv = mla_w_ukv.reshape(depth, MLA_KV_RANK, n_mla, 2, HEAD_W).transpose(0, 1, 3, 2, 4)
    wkv = wkv.reshape(depth, MLA_KV_RANK, 2 * n_mla * HEAD_W).astype(bf16)

    flat = lambda a: a.reshape(depth, 2, 1, n_state)
    s5_lre, s5_lim = flat(s5_lam_re), flat(s5_lam_im)
    s5_lst = flat(jnp.broadcast_to(s5_log_step[..., None], s5_lam_re.shape))
    s5_btr = s5_b_re.transpose(0, 1, 4, 2, 3).reshape(depth, 2, S5_GROUP, n_state)
    s5_bti = s5_b_im.transpose(0, 1, 4, 2, 3).reshape(depth, 2, S5_GROUP, n_state)
    s5_ctr = s5_c_re.transpose(0, 1, 3, 2, 4).reshape(depth, 2, S5_GROUP, n_state)
    s5_cti = s5_c_im.transpose(0, 1, 3, 2, 4).reshape(depth, 2, S5_GROUP, n_state)

    row = lambda a, l: a[l][None]
    conv_w_p = jnp.pad(conv_w, ((0, 0), (0, 1), (0, 0)))

    xs = x.reshape(n_batch * seq, d)
    cs = ctx.reshape(n_batch * ctx_len, d)
    for l in range(depth):
        last = l == depth - 1
        lambda_init = 0.8 - 0.6 * math.exp(-0.3 * l)
        g_pre = row(norm_pre_mix, l)

        px = _in_proj(xs, g_pre, mods, l, grp_x, w_in_p, tm_x, 2 * SEC)
        pc = _in_proj(cs, g_pre, mods, l, grp_c, w_in_p, tm_c, 2 * SEC)

        qx, kx, vx = _da_prep(px, cos_x, sin_x, tpb, tm_x, da_scale)
        qc, kc, vc = _da_prep(pc, cos_c, sin_c, 1, tm_c, da_scale)
        da_params = (row(da_lam_q1, l), row(da_lam_k1, l), row(da_lam_q2, l), row(da_lam_k2, l), row(da_subln, l))
        y_da_x = _attention(qx, kc, vc, kx, vx, da_params, n_batch, ctx_len, tq_da, tk, lambda_init)

        mla_w = (row(mla_q_norm, l), row(mla_kv_norm, l), wq[l], wkv[l])
        mqx, mkx, mvx = _mla_prep(px, *mla_w, cos_x, sin_x, tpb, tm_x, mla_scale)
        mqc, mkc, mvc = _mla_prep(pc, *mla_w, cos_c, sin_c, 1, tm_c, mla_scale)
        y_mla_x = _attention(mqx, mkc, mvc, mkx, mvx, None, n_batch, ctx_len, tq_mla, tk, lambda_init)

        ys_c, ys_x = _s5_scan(pc.reshape(n_batch, ctx_len, -1), px.reshape(n_batch, seq, -1),
                              s5_lre[l], s5_lim[l], s5_lst[l], s5_btr[l], s5_bti[l], s5_ctr[l], s5_cti[l], n_batch)
        y_s5_x = _s5_glu(ys_x.reshape(2, n_batch * seq, SEC), px, row(s5_d, l), w_glu_b[l], tm_x)

        cv = (conv_w_p[l], row(conv_b, l), row(conv_ln_g, l), row(conv_ln_b, l))
        y_cv_x = _conv(px, *cv, n_batch, _row_tile(seq, 512))

        g_post, g_pre_f, g_post_f = row(norm_post_mix, l), row(norm_pre_ffn, l), row(norm_post_ffn, l)
        tm_o = _row_tile(tm_x, 512)
        x1 = _mix_out((y_da_x, y_s5_x, y_mla_x, y_cv_x), w_out_b, xs, g_post, mods, l,
                      lambda i: i // (seq // tm_o), tm_o)
        act = _ffn_in(x1, g_pre_f, mods, l, grp_x, w_ffn_in_b, tm_x, SEC)
        xs = _ffn_out(act, w_ffn_out_b, x1, g_post_f, mods, l, lambda i: i // (seq // tm_o), tm_o, tk_f)

        if not last:
            y_da_c = _attention(qc, kc, vc, None, None, da_params, n_batch, ctx_len, ctx_len, tk, lambda_init)
            y_mla_c = _attention(mqc, mkc, mvc, None, None, None, n_batch, ctx_len, ctx_len, tk, lambda_init)
            y_s5_c = _s5_glu(ys_c.reshape(2, n_batch * ctx_len, SEC), pc, row(s5_d, l), w_glu_b[l], tm_c)
            y_cv_c = _conv(pc, *cv, n_batch, ctx_len)
            tm_oc = _row_tile(tm_c, 512)
            c1 = _mix_out((y_da_c, y_s5_c, y_mla_c, y_cv_c), w_out_b, cs, g_post, mods, l, grp_c, tm_oc)
            act_c = _ffn_in(c1, g_pre_f, mods, l, grp_c, w_ffn_in_b, tm_c, SEC)
            cs = _ffn_out(act_c, w_ffn_out_b, c1, g_post_f, mods, l, grp_c, tm_oc, tk_f)
    return xs.reshape(n_batch, seq, d)
```

```python
import functools
import math

import jax
import jax.numpy as jnp
from jax import lax
from jax.experimental import pallas as pl
from jax.experimental.pallas import tpu as pltpu

f32 = jnp.float32
bf16 = jnp.bfloat16

NORM_EPS = 1e-6
ROPE_BASE = 10000.0
GRID_W = 64
ROT_DIM = 64
CONV_K = 31
S5_GROUP = 16
S5_STATE = 64

LANES = 128
SUBLANES = 8
SEC = 512
N_SEC = 8
HEAD_W = 128
MLA_QK = 256
MLA_KV_RANK = 256
V7X_VMEM_BYTES = 64 * 1024 * 1024
VMEM_LIMIT = V7X_VMEM_BYTES * 7 // 8


def _cp(*sem):
    return pltpu.CompilerParams(dimension_semantics=sem, vmem_limit_bytes=VMEM_LIMIT)


def _row_tile(rows, want):
    t = min(rows, want)
    assert rows % t == 0, (rows, want)
    return t


def _ada_kernel(cc_ref, w_ref, b_ref, o_ref):
    s = jax.nn.silu(cc_ref[...]).astype(bf16)
    o_ref[...] = jnp.dot(s, w_ref[...].astype(bf16), preferred_element_type=f32) + b_ref[...]


def _ada(cc, w_ada, b_ada):
    depth, d, n = w_ada.shape
    g = cc.shape[0]
    tn = 512
    return pl.pallas_call(
        _ada_kernel,
        grid=(depth, n // tn),
        in_specs=[pl.BlockSpec((g, d), lambda l, j: (0, 0)),
                  pl.BlockSpec((None, d, tn), lambda l, j: (l, 0, j)),
                  pl.BlockSpec((None, 1, tn), lambda l, j: (l, 0, j))],
        out_specs=pl.BlockSpec((None, g, tn), lambda l, j: (l, 0, j)),
        out_shape=jax.ShapeDtypeStruct((depth, g, n), f32),
        compiler_params=_cp("arbitrary", "arbitrary"),
        name="ada",
    )(cc, w_ada, b_ada.reshape(depth, 1, n))


def _norm_mod(x, g, shift, scale):
    ms = jnp.mean(x * x, axis=-1, keepdims=True)
    return (x * lax.rsqrt(ms + NORM_EPS) * g) * (1.0 + scale) + shift


def _in_proj_kernel(x_ref, g_ref, sh_ref, sc_ref, w_ref, o_ref, h_ref):
    @pl.when(pl.program_id(1) == 0)
    def _():
        h_ref[...] = _norm_mod(x_ref[...], g_ref[...], sh_ref[...], sc_ref[...]).astype(bf16)

    o_ref[...] = jnp.dot(h_ref[...], w_ref[...], preferred_element_type=f32).astype(o_ref.dtype)


def _residual(acc, x, g, gate):
    ms = jnp.mean(acc * acc, axis=-1, keepdims=True)
    return x + gate * (acc * lax.rsqrt(ms + NORM_EPS) * g)


def _ffn_kernel(x_ref, g_ref, sh_ref, sc_ref, wg_ref, wu_ref, wo_ref, g2_ref, gate_ref, o_ref, h_ref, acc_ref):
    j = pl.program_id(1)

    @pl.when(j == 0)
    def _():
        h_ref[...] = _norm_mod(x_ref[...], g_ref[...], sh_ref[...], sc_ref[...]).astype(bf16)
        acc_ref[...] = jnp.zeros_like(acc_ref)

    h = h_ref[...]
    a = jnp.dot(h, wg_ref[...], preferred_element_type=f32)
    b = jnp.dot(h, wu_ref[...], preferred_element_type=f32)
    act = (jax.nn.silu(a) * b).astype(bf16)
    acc_ref[...] += jnp.dot(act, wo_ref[...], preferred_element_type=f32)

    @pl.when(j == pl.num_programs(1) - 1)
    def _():
        o_ref[...] = _residual(acc_ref[...], x_ref[...], g2_ref[...], gate_ref[...])


def _mod_spec(l, grp, k, d):
    return pl.BlockSpec((None, None, 1, d), lambda i, j: (l, grp(i), 0, k))


def _in_proj(x, g, mods, l, grp, w, tm, tn):
    rows, d = x.shape
    n = w.shape[2]
    return pl.pallas_call(
        _in_proj_kernel,
        grid=(rows // tm, n // tn),
        in_specs=[pl.BlockSpec((tm, d), lambda i, j: (i, 0)),
                  pl.BlockSpec((1, d), lambda i, j: (0, 0)),
                  _mod_spec(l, grp, 0, d), _mod_spec(l, grp, 1, d),
                  pl.BlockSpec((None, d, tn), lambda i, j: (l, 0, j))],
        out_specs=pl.BlockSpec((tm, tn), lambda i, j: (i, j)),
        out_shape=jax.ShapeDtypeStruct((rows, n), bf16),
        scratch_shapes=[pltpu.VMEM((tm, d), bf16)],
        compiler_params=_cp("arbitrary", "arbitrary"),
        name="in_proj",
    )(x, g, mods, mods, w)


def _ffn(x, g_pre, g_post, mods, l, grp, w_in, w_out, tm, tn):
    rows, d = x.shape
    dff = w_out.shape[1]
    nj = dff // tn
    return pl.pallas_call(
        _ffn_kernel,
        grid=(rows // tm, nj),
        in_specs=[pl.BlockSpec((tm, d), lambda i, j: (i, 0)),
                  pl.BlockSpec((1, d), lambda i, j: (0, 0)),
                  _mod_spec(l, grp, 3, d), _mod_spec(l, grp, 4, d),
                  pl.BlockSpec((None, d, tn), lambda i, j: (l, 0, j)),
                  pl.BlockSpec((None, d, tn), lambda i, j: (l, 0, j + nj)),
                  pl.BlockSpec((None, tn, d), lambda i, j: (l, j, 0)),
                  pl.BlockSpec((1, d), lambda i, j: (0, 0)),
                  _mod_spec(l, grp, 5, d)],
        out_specs=pl.BlockSpec((tm, d), lambda i, j: (i, 0)),
        out_shape=jax.ShapeDtypeStruct((rows, d), f32),
        scratch_shapes=[pltpu.VMEM((tm, d), bf16), pltpu.VMEM((tm, d), f32)],
        compiler_params=_cp("arbitrary", "arbitrary"),
        name="ffn",
    )(x, g_pre, mods, mods, w_in, w_in, w_out, g_post, mods)


def _mix_out_kernel(y0_ref, y1_ref, y2_ref, y3_ref, w_ref, x_ref, g_ref, gate_ref, o_ref):
    tk = y0_ref.shape[1]
    acc = None
    for s, y_ref in enumerate((y0_ref, y1_ref, y2_ref, y3_ref)):
        part = jnp.dot(y_ref[...], w_ref[s * tk:(s + 1) * tk, :], preferred_element_type=f32)
        acc = part if acc is None else acc + part
    o_ref[...] = _residual(acc, x_ref[...], g_ref[...], gate_ref[...])


def _mix_out(ys, w, x, g, mods, l, grp, tm):
    rows, d = x.shape
    tk = ys[0].shape[1]
    mod = _mod_spec(l, grp, 2, d)
    return pl.pallas_call(
        _mix_out_kernel,
        grid=(rows // tm,),
        in_specs=[pl.BlockSpec((tm, tk), lambda i: (i, 0))] * len(ys) + [
            pl.BlockSpec((None, len(ys) * tk, d), lambda i: (l, 0, 0)),
            pl.BlockSpec((tm, d), lambda i: (i, 0)),
            pl.BlockSpec((1, d), lambda i: (0, 0)),
            pl.BlockSpec(mod.block_shape, lambda i: mod.index_map(i, 0))],
        out_specs=pl.BlockSpec((tm, d), lambda i: (i, 0)),
        out_shape=jax.ShapeDtypeStruct((rows, d), f32),
        compiler_params=_cp("arbitrary"),
        name="mix_out",
    )(*ys, w, x, g, mods)


def _rope_tables(n_tokens):
    n_rows = n_tokens // GRID_W
    row = jnp.repeat(jnp.arange(n_rows, dtype=f32), GRID_W)
    col = jnp.tile(jnp.arange(GRID_W, dtype=f32), n_rows)
    n_freq = ROT_DIM // 4
    inv = ROPE_BASE ** (-jnp.arange(n_freq, dtype=f32) / n_freq)
    ang = jnp.concatenate([row[:, None] * inv, col[:, None] * inv], axis=-1)
    cos, sin = jnp.cos(ang), jnp.sin(ang)
    return jnp.tile(cos, (1, 4)), jnp.tile(jnp.concatenate([-sin, sin], axis=-1), (1, 2))


def _rope_slab(x, cos, sin):
    lane = lax.broadcasted_iota(jnp.int32, x.shape, 1)
    partner = jnp.where(lane % ROT_DIM < ROT_DIM // 2,
                        pltpu.roll(x, LANES - ROT_DIM // 2, 1), pltpu.roll(x, ROT_DIM // 2, 1))
    return x * cos + partner * sin


def _da_prep_kernel(q_ref, k_ref, cos_ref, sin_ref, qo_ref, ko_ref, *, scale):
    cos, sin = cos_ref[...], sin_ref[...]
    for h in range(SEC // LANES):
        sl = slice(h * LANES, (h + 1) * LANES)
        qo_ref[:, sl] = (_rope_slab(q_ref[:, sl].astype(f32), cos, sin) * scale).astype(bf16)
        ko_ref[:, sl] = _rope_slab(k_ref[:, sl].astype(f32), cos, sin).astype(bf16)


def _da_prep(px, cos, sin, pos_tiles, tm, scale):
    rows = px.shape[0]
    sec = lambda s: pl.BlockSpec((tm, SEC), lambda i: (i, s))
    tab = pl.BlockSpec((tm, LANES), lambda i: (i % pos_tiles, 0))
    out = jax.ShapeDtypeStruct((rows, SEC), bf16)
    return pl.pallas_call(
        functools.partial(_da_prep_kernel, scale=scale),
        grid=(rows // tm,),
        in_specs=[sec(0), sec(1), tab, tab],
        out_specs=[pl.BlockSpec((tm, SEC), lambda i: (i, 0))] * 2,
        out_shape=[out, out],
        compiler_params=_cp("arbitrary"),
        name="da_prep",
    )(px, px, cos, sin)


def _rms_g(x, g):
    ms = jnp.mean(x * x, axis=-1, keepdims=True)
    return x * lax.rsqrt(ms + NORM_EPS) * g


def _mla_prep_kernel(cq_ref, ckv_ref, gq_ref, gkv_ref, wq_ref, wkv_ref, cos_ref, sin_ref,
                     qo_ref, ko_ref, vo_ref, *, scale):
    cos, sin = cos_ref[...], sin_ref[...]
    n_heads = SEC // HEAD_W
    q = jnp.dot(_rms_g(cq_ref[...].astype(f32), gq_ref[...]).astype(bf16), wq_ref[...],
                preferred_element_type=f32)
    kv = jnp.dot(_rms_g(ckv_ref[:, :MLA_KV_RANK].astype(f32), gkv_ref[...]).astype(bf16), wkv_ref[...],
                 preferred_element_type=f32)
    kr = _rope_slab(ckv_ref[:, MLA_KV_RANK:MLA_KV_RANK + LANES].astype(f32), cos, sin).astype(bf16)
    for h in range(n_heads):
        base = h * MLA_QK
        qo_ref[:, base:base + LANES] = (q[:, base:base + LANES] * scale).astype(bf16)
        qo_ref[:, base + LANES:base + MLA_QK] = (
            _rope_slab(q[:, base + LANES:base + MLA_QK], cos, sin) * scale).astype(bf16)
        ko_ref[:, base:base + LANES] = kv[:, h * LANES:(h + 1) * LANES].astype(bf16)
        ko_ref[:, base + LANES:base + MLA_QK] = kr
    vo_ref[...] = kv[:, n_heads * LANES:].astype(bf16)


def _mla_prep(px, gq, gkv, wq, wkv, cos, sin, pos_tiles, tm, scale):
    rows = px.shape[0]
    n_heads = SEC // HEAD_W
    sec = lambda s: pl.BlockSpec((tm, SEC), lambda i: (i, s))
    tab = pl.BlockSpec((tm, LANES), lambda i: (i % pos_tiles, 0))
    full = lambda a: pl.BlockSpec(a.shape, lambda i: (0,) * a.ndim)
    return pl.pallas_call(
        functools.partial(_mla_prep_kernel, scale=scale),
        grid=(rows // tm,),
        in_specs=[sec(4), sec(5), full(gq), full(gkv), full(wq), full(wkv), tab, tab],
        out_specs=[pl.BlockSpec((tm, n_heads * MLA_QK), lambda i: (i, 0)),
                   pl.BlockSpec((tm, n_heads * MLA_QK), lambda i: (i, 0)),
                   pl.BlockSpec((tm, SEC), lambda i: (i, 0))],
        out_shape=[jax.ShapeDtypeStruct((rows, n_heads * MLA_QK), bf16),
                   jax.ShapeDtypeStruct((rows, n_heads * MLA_QK), bf16),
                   jax.ShapeDtypeStruct((rows, SEC), bf16)],
        compiler_params=_cp("arbitrary"),
        name="mla_prep",
    )(px, px, gq, gkv, wq, wkv, cos, sin)


def _attn_kernel(*refs, diff, has_x_keys, tk, lambda_init):
    refs = list(refs)
    p_ref = refs.pop()
    s_ref = refs.pop()
    o_ref = refs.pop()
    q_ref, kc_ref, vc_ref = refs[:3]
    refs = refs[3:]
    chunks = [(kc_ref, vc_ref, 0, kc_ref.shape[0])]
    if has_x_keys:
        kx_ref, vx_ref = refs[:2]
        refs = refs[2:]
        chunks += [(kx_ref, vx_ref, c * tk, tk) for c in range(kx_ref.shape[0] // tk)]
    q = q_ref[...]
    tq = q.shape[0]
    if diff:
        lane = lax.broadcasted_iota(jnp.int32, q.shape, 1)
        zero = jnp.zeros_like(q)
        qs = [jnp.where(lane < ROT_DIM, q, zero), jnp.where(lane >= ROT_DIM, q, zero)]
    else:
        qs = [q]
    lane_tiles = lambda a: [a[:, j * LANES:(j + 1) * LANES] for j in range(a.shape[1] // LANES)]

    m_acc = [jnp.full((tq, LANES), -jnp.inf, f32) for _ in qs]
    col = 0
    for k_ref, _, start, size in chunks:
        k = k_ref[pl.ds(start, size), :]
        for mi, qm in enumerate(qs):
            s = lax.dot_general(qm, k, (((1,), (1,)), ((), ())), preferred_element_type=f32)
            s_ref[mi, :, col:col + size] = s
            for t in lane_tiles(s):
                m_acc[mi] = jnp.maximum(m_acc[mi], t)
        col += size
    m_row = [jnp.broadcast_to(jnp.max(m, axis=-1, keepdims=True), (tq, LANES)) for m in m_acc]

    l_acc = [jnp.zeros((tq, LANES), f32) for _ in qs]
    col = 0
    for _, _, _, size in chunks:
        for mi in range(len(qs)):
            p = jnp.exp2(s_ref[mi, :, col:col + size] - jnp.tile(m_row[mi], (1, size // LANES)))
            for t in lane_tiles(p):
                l_acc[mi] = l_acc[mi] + t
            p_ref[mi, :, col:col + size] = p.astype(bf16)
        col += size
    n_ctx = kc_ref.shape[0]
    acc = []
    for mi in range(len(qs)):
        a = jnp.dot(p_ref[mi, :, :n_ctx], vc_ref[...], preferred_element_type=f32)
        if has_x_keys:
            a = a + jnp.dot(p_ref[mi, :, n_ctx:], vx_ref[...], preferred_element_type=f32)
        acc.append(a)

    outs = [a / jnp.sum(l, axis=-1, keepdims=True) for a, l in zip(acc, l_acc)]
    if diff:
        lq1, lk1, lq2, lk2, subln = (r[...] for r in refs)
        lam = (jnp.exp(jnp.sum(lq1 * lk1, axis=-1, keepdims=True))
               - jnp.exp(jnp.sum(lq2 * lk2, axis=-1, keepdims=True)) + lambda_init)
        dlt = outs[0] - lam * outs[1]
        o_ref[...] = (_rms_g(dlt, subln) * (1.0 - lambda_init)).astype(o_ref.dtype)
    else:
        o_ref[...] = outs[0].astype(o_ref.dtype)


def _attention(q, kc, vc, kx, vx, params, n_batch, ctx_len, tq, tk, lambda_init, v_sec=0):
    n_heads = SEC // HEAD_W
    v_off = v_sec * n_heads
    dq = q.shape[1] // n_heads
    lq = q.shape[0] // n_batch
    nq = lq // tq
    diff = params is not None
    in_specs = [pl.BlockSpec((tq, dq), lambda b, h, i: (b * nq + i, h)),
                pl.BlockSpec((ctx_len, dq), lambda b, h, i: (b, h)),
                pl.BlockSpec((ctx_len, HEAD_W), lambda b, h, i: (b, v_off + h))]
    args = [q, kc, vc]
    if kx is not None:
        lx = kx.shape[0] // n_batch
        in_specs += [pl.BlockSpec((lx, dq), lambda b, h, i: (b, h)),
                     pl.BlockSpec((lx, HEAD_W), lambda b, h, i: (b, v_off + h))]
        args += [kx, vx]
    n_keys = ctx_len + (kx.shape[0] // n_batch if kx is not None else 0)
    if diff:
        in_specs += [pl.BlockSpec(p.shape, lambda b, h, i: (0, 0)) for p in params]
        args += list(params)
    return pl.pallas_call(
        functools.partial(_attn_kernel, diff=diff, has_x_keys=kx is not None, tk=tk, lambda_init=lambda_init),
        grid=(n_batch, n_heads, nq),
        in_specs=in_specs,
        out_specs=pl.BlockSpec((tq, HEAD_W), lambda b, h, i: (b * nq + i, h)),
        out_shape=jax.ShapeDtypeStruct((q.shape[0], n_heads * HEAD_W), bf16),
        scratch_shapes=[pltpu.VMEM((2 if diff else 1, tq, n_keys), f32),
                        pltpu.VMEM((2 if diff else 1, tq, n_keys), bf16)],
        compiler_params=_cp("arbitrary", "arbitrary", "arbitrary"),
        name="attention",
    )(*args)


def _s5_kernel(uc_ref, ux_ref, lre_ref, lim_ref, lst_ref, btr_ref, bti_ref, ctr_ref, cti_ref,
               yc_ref, yx_ref, wb_re, wb_im, wc_re, wc_im, a_re, a_im, h_re, h_im, s_re, s_im):
    d = pl.program_id(0)
    c = pl.program_id(1)
    n_batch, t_len, width = uc_ref.shape
    n_state = lre_ref.shape[1]
    half = n_state // 2
    n_groups = width // S5_GROUP
    n_blk = half // LANES

    @pl.when(c == 0)
    def _():
        lre, lim = lre_ref[...], lim_ref[...]
        step = jnp.exp(lst_ref[...])
        mag = jnp.exp(lre * step)
        are = mag * jnp.cos(lim * step)
        aim = mag * jnp.sin(lim * step)
        den = lre * lre + lim * lim
        fre = ((are - 1.0) * lre + aim * lim) / den
        fim = (aim * lre - (are - 1.0) * lim) / den
        row = lax.broadcasted_iota(jnp.int32, (width, n_state), 0) // S5_GROUP
        col = lax.broadcasted_iota(jnp.int32, (width, n_state), 1) // S5_STATE
        blk = row == col
        tile = lambda r: jnp.broadcast_to(r[...][None], (n_groups, S5_GROUP, n_state)).reshape(width, n_state)
        btr, bti = tile(btr_ref), tile(bti_ref)
        wb_re[...] = jnp.where(blk, fre * btr - fim * bti, 0.0).astype(bf16)
        wb_im[...] = jnp.where(blk, fre * bti + fim * btr, 0.0).astype(bf16)
        wc_re[...] = jnp.where(blk, tile(ctr_ref), 0.0).astype(bf16)
        wc_im[...] = jnp.where(blk, -tile(cti_ref), 0.0).astype(bf16)
        for hf in range(2):
            rows = slice(hf * n_batch, (hf + 1) * n_batch)
            for j in range(n_blk):
                cols = slice(hf * half + j * LANES, hf * half + (j + 1) * LANES)
                a_re[j, rows, :] = jnp.broadcast_to(are[:, cols], (n_batch, LANES))
                a_im[j, rows, :] = jnp.broadcast_to(aim[:, cols], (n_batch, LANES))
        h_re[...] = jnp.zeros_like(h_re)
        h_im[...] = jnp.zeros_like(h_im)

    pitch = s_re.shape[1] // (2 * n_batch)
    kb = LANES
    sb = kb // S5_GROUP * S5_STATE
    nt = (((1,), (1,)), ((), ()))

    def slab(col0, b):
        hf, rem = divmod(col0, half)
        return rem // LANES, pl.ds((hf * n_batch + b) * pitch, t_len)

    def run(u_ref, y_ref):
        for b in range(n_batch):
            u = u_ref[b]
            for bb in range(width // kb):
                ub = u[:, bb * kb:(bb + 1) * kb]
                bre = jnp.dot(ub, wb_re[bb * kb:(bb + 1) * kb, bb * sb:(bb + 1) * sb], preferred_element_type=f32)
                bim = jnp.dot(ub, wb_im[bb * kb:(bb + 1) * kb, bb * sb:(bb + 1) * sb], preferred_element_type=f32)
                for jj in range(sb // LANES):
                    j, rows = slab(bb * sb + jj * LANES, b)
                    s_re[j, rows, :] = bre[:, jj * LANES:(jj + 1) * LANES]
                    s_im[j, rows, :] = bim[:, jj * LANES:(jj + 1) * LANES]

        ar = [a_re[j] for j in range(n_blk)]
        ai = [a_im[j] for j in range(n_blk)]

        def step(k, carry):
            hr, hi = carry
            t = k + d * (t_len - 1 - 2 * k)
            rows = pl.ds(t, 2 * n_batch, stride=pitch)
            nr, ni = [], []
            for j in range(n_blk):
                nr.append(ar[j] * hr[j] - ai[j] * hi[j] + s_re[j, rows, :])
                ni.append(ar[j] * hi[j] + ai[j] * hr[j] + s_im[j, rows, :])
                s_re[j, rows, :] = nr[j]
                s_im[j, rows, :] = ni[j]
            return nr, ni

        init = ([h_re[j] for j in range(n_blk)], [h_im[j] for j in range(n_blk)])
        hr, hi = lax.fori_loop(0, t_len, step, init)
        for j in range(n_blk):
            h_re[j] = hr[j]
            h_im[j] = hi[j]

        for b in range(n_batch):
            ys = []
            for bb in range(width // kb):
                slabs = [slab(bb * sb + jj * LANES, b) for jj in range(sb // LANES)]
                sr = jnp.concatenate([s_re[j, rows, :] for j, rows in slabs], axis=1).astype(bf16)
                si = jnp.concatenate([s_im[j, rows, :] for j, rows in slabs], axis=1).astype(bf16)
                ys.append(
                    lax.dot_general(sr, wc_re[bb * kb:(bb + 1) * kb, bb * sb:(bb + 1) * sb], nt,
                                    preferred_element_type=f32)
                    + lax.dot_general(si, wc_im[bb * kb:(bb + 1) * kb, bb * sb:(bb + 1) * sb], nt,
                                      preferred_element_type=f32))
            y_ref[b] = jnp.concatenate(ys, axis=1)

    @pl.when(c == 0)
    def _():
        run(uc_ref, yc_ref)

    @pl.when(c > 0)
    def _():
        run(ux_ref, yx_ref)


def _s5_scan(px_c, px_x, lam_re, lam_im, log_step, bt_re, bt_im, ct_re, ct_im, n_batch):
    t_len = px_c.shape[1]
    seq = px_x.shape[1]
    assert seq % t_len == 0
    nch = seq // t_len
    n_state = lam_re.shape[-1]
    half = n_state // 2
    pitch = t_len + SUBLANES
    assert t_len % (2 * SUBLANES) == 0

    def x_chunk(d, c):
        return jnp.where(d == 0, jnp.maximum(c - 1, 0), jnp.minimum(nch - c, nch - 1))

    vec = pl.BlockSpec((None, 1, n_state), lambda d, c: (d, 0, 0))
    mat = pl.BlockSpec((None, S5_GROUP, n_state), lambda d, c: (d, 0, 0))
    return pl.pallas_call(
        _s5_kernel,
        grid=(2, nch + 1),
        in_specs=[pl.BlockSpec((n_batch, t_len, SEC), lambda d, c: (0, 0, 3)),
                  pl.BlockSpec((n_batch, t_len, SEC), lambda d, c: (0, x_chunk(d, c), 3)),
                  vec, vec, vec, mat, mat, mat, mat],
        out_specs=[pl.BlockSpec((None, n_batch, t_len, SEC), lambda d, c: (d, 0, 0, 0)),
                   pl.BlockSpec((None, n_batch, t_len, SEC), lambda d, c: (d, 0, x_chunk(d, c), 0))],
        out_shape=[jax.ShapeDtypeStruct((2, n_batch, t_len, SEC), f32),
                   jax.ShapeDtypeStruct((2, n_batch, seq, SEC), f32)],
        scratch_shapes=[pltpu.VMEM((SEC, n_state), bf16)] * 4
        + [pltpu.VMEM((half // LANES, 2 * n_batch, LANES), f32)] * 4
        + [pltpu.VMEM((half // LANES, 2 * n_batch * pitch, LANES), f32)] * 2,
        compiler_params=_cp("arbitrary", "arbitrary"),
        name="s5_scan",
    )(px_c, px_x, lam_re, lam_im, log_step, bt_re, bt_im, ct_re, ct_im)


def _s5_glu_kernel(yf_ref, yb_ref, u_ref, d_ref, w_ref, o_ref):
    y = d_ref[...] * u_ref[...].astype(f32) + yf_ref[...] + yb_ref[...]
    g = jax.nn.gelu(y)
    z = jnp.dot(g.astype(bf16), w_ref[...], preferred_element_type=f32)
    o_ref[...] = (g * jax.nn.sigmoid(z)).astype(o_ref.dtype)


def _s5_glu(y, px, s5_d, w_glu, tm):
    rows = px.shape[0]
    return pl.pallas_call(
        _s5_glu_kernel,
        grid=(rows // tm,),
        in_specs=[pl.BlockSpec((None, tm, SEC), lambda i: (0, i, 0)),
                  pl.BlockSpec((None, tm, SEC), lambda i: (1, i, 0)),
                  pl.BlockSpec((tm, SEC), lambda i: (i, 3)),
                  pl.BlockSpec((1, SEC), lambda i: (0, 0)),
                  pl.BlockSpec((SEC, SEC), lambda i: (0, 0))],
        out_specs=pl.BlockSpec((tm, SEC), lambda i: (i, 0)),
        out_shape=jax.ShapeDtypeStruct((rows, SEC), bf16),
        compiler_params=_cp("arbitrary"),
        name="s5_glu",
    )(y, y, px, s5_d, w_glu)


CONV_HALO = 16


def _conv_kernel(vp_ref, vc_ref, vn_ref, gp_ref, gc_ref, gn_ref, w_ref, b_ref, lg_ref, lb_ref, o_ref, ext_ref):
    j = pl.program_id(1)
    tc = vc_ref.shape[0]
    glu = lambda v, g: v[...].astype(f32) * jax.nn.sigmoid(g[...].astype(f32))
    ext_ref[0, pl.ds(0, CONV_HALO), :] = jnp.where(j > 0, glu(vp_ref, gp_ref), 0.0)
    ext_ref[0, pl.ds(CONV_HALO, tc), :] = glu(vc_ref, gc_ref)
    ext_ref[0, pl.ds(CONV_HALO + tc, CONV_HALO), :] = jnp.where(j < pl.num_programs(1) - 1, glu(vn_ref, gn_ref), 0.0)
    n_keep = tc + 2 * CONV_HALO - SUBLANES
    for s in range(1, SUBLANES):
        ext_ref[s, pl.ds(0, n_keep), :] = ext_ref[0, pl.ds(s, n_keep), :]
    y = jnp.zeros((tc, vc_ref.shape[1]), f32)
    for k in range(CONV_K):
        off = CONV_HALO - CONV_K // 2 + k
        y += w_ref[pl.ds(k, 1), :] * ext_ref[off % SUBLANES, pl.ds(off - off % SUBLANES, tc), :]
    y = y + b_ref[...]
    mu = jnp.mean(y, axis=-1, keepdims=True)
    var = jnp.mean(jnp.square(y - mu), axis=-1, keepdims=True)
    z = (y - mu) * lax.rsqrt(var + NORM_EPS) * lg_ref[...] + lb_ref[...]
    o_ref[...] = jax.nn.silu(z).astype(o_ref.dtype)


def _conv(px, conv_w, conv_b, ln_g, ln_b, n_batch, tc):
    rows = px.shape[0]
    seq = rows // n_batch
    nt = seq // tc
    hpt = tc // CONV_HALO
    cur = lambda s: pl.BlockSpec((tc, SEC), lambda b, j: (b * nt + j, s))
    prev = lambda s: pl.BlockSpec((CONV_HALO, SEC), lambda b, j: (jnp.maximum((b * nt + j) * hpt - 1, 0), s))
    nxt = lambda s: pl.BlockSpec(
        (CONV_HALO, SEC), lambda b, j: (jnp.minimum((b * nt + j + 1) * hpt, rows // CONV_HALO - 1), s))
    full = lambda a: pl.BlockSpec(a.shape, lambda b, j: (0,) * a.ndim)
    return pl.pallas_call(
        _conv_kernel,
        grid=(n_batch, nt),
        in_specs=[prev(6), cur(6), nxt(6), prev(7), cur(7), nxt(7),
                  full(conv_w), full(conv_b), full(ln_g), full(ln_b)],
        out_specs=pl.BlockSpec((tc, SEC), lambda b, j: (b * nt + j, 0)),
        out_shape=jax.ShapeDtypeStruct((rows, SEC), bf16),
        scratch_shapes=[pltpu.VMEM((SUBLANES, tc + 2 * CONV_HALO, SEC), f32)],
        compiler_params=_cp("arbitrary", "arbitrary"),
        name="conv",
    )(px, px, px, px, px, px, conv_w, conv_b, ln_g, ln_b)


def kernel(x, c, ctx, c_ctx, w_ada, b_ada, norm_pre_mix, norm_post_mix, norm_pre_ffn, norm_post_ffn, w_in, w_out, da_lam_q1, da_lam_k1, da_lam_q2, da_lam_k2, da_subln, s5_lam_re, s5_lam_im, s5_log_step, s5_b_re, s5_b_im, s5_c_re, s5_c_im, s5_d, s5_w_glu, mla_q_norm, mla_kv_norm, mla_w_uq, mla_w_ukv, conv_w, conv_b, conv_ln_g, conv_ln_b, w_ffn_in, w_ffn_out):
    n_batch, seq, d = x.shape
    ctx_len = ctx.shape[1]
    depth = w_in.shape[0]
    n_state = s5_lam_re.shape[2] * s5_lam_re.shape[3]
    mla_end = 4 * SEC + SEC + MLA_KV_RANK + ROT_DIM
    assert d == 4 * SEC and w_in.shape[2] == mla_end + 2 * SEC

    da_scale = ROT_DIM ** -0.5 * math.log2(math.e)
    mla_scale = (HEAD_W + ROT_DIM) ** -0.5 * math.log2(math.e)

    tm_x = _row_tile(seq, 1024)
    tm_c = _row_tile(n_batch * ctx_len, 1024)
    tpb = seq // tm_x
    grp_x = lambda i: i // tpb
    grp_c = lambda i: n_batch
    tq_da = _row_tile(seq, 512)
    tq_mla = _row_tile(seq, 256)
    tk = _row_tile(seq, 512)

    n_grp = -(-(n_batch + 1) // SUBLANES) * SUBLANES
    cc = jnp.concatenate([c, c_ctx[None], jnp.zeros((n_grp - n_batch - 1, d), f32)], axis=0)
    mods = _ada(cc, w_ada, b_ada).reshape(depth, n_grp, 1, 6 * d)

    cos_x, sin_x = _rope_tables(seq)
    cos_c, sin_c = jnp.ones((tm_c, LANES), f32), jnp.zeros((tm_c, LANES), f32)

    w_in_p = jnp.concatenate([w_in[:, :, :mla_end], jnp.zeros((depth, d, N_SEC * SEC - w_in.shape[2]), f32),
                              w_in[:, :, mla_end:]], axis=2).astype(bf16)
    w_out_b = w_out.astype(bf16)
    w_ffn_in_b = w_ffn_in.astype(bf16)
    w_ffn_out_b = w_ffn_out.astype(bf16)
    w_glu_b = s5_w_glu.astype(bf16)
    n_mla = SEC // HEAD_W
    wq = mla_w_uq.reshape(depth, SEC, n_mla, HEAD_W + ROT_DIM)
    wq = jnp.pad(wq, ((0, 0), (0, 0), (0, 0), (0, MLA_QK - HEAD_W - ROT_DIM))).reshape(depth, SEC, n_mla * MLA_QK)
    wq = wq.astype(bf16)
    wkv = mla_w_ukv.reshape(depth, MLA_KV_RANK, n_mla, 2, HEAD_W).transpose(0, 1, 3, 2, 4)
    wkv = wkv.reshape(depth, MLA_KV_RANK, 2 * n_mla * HEAD_W).astype(bf16)

    flat = lambda a: a.reshape(depth, 2, 1, n_state)
    s5_lre, s5_lim = flat(s5_lam_re), flat(s5_lam_im)
    s5_lst = flat(jnp.broadcast_to(s5_log_step[..., None], s5_lam_re.shape))
    s5_btr = s5_b_re.transpose(0, 1, 4, 2, 3).reshape(depth, 2, S5_GROUP, n_state)
    s5_bti = s5_b_im.transpose(0, 1, 4, 2, 3).reshape(depth, 2, S5_GROUP, n_state)
    s5_ctr = s5_c_re.transpose(0, 1, 3, 2, 4).reshape(depth, 2, S5_GROUP, n_state)
    s5_cti = s5_c_im.transpose(0, 1, 3, 2, 4).reshape(depth, 2, S5_GROUP, n_state)

    row = lambda a, l: a[l][None]
    conv_w_p = jnp.pad(conv_w, ((0, 0), (0, 1), (0, 0)))

    xs = x.reshape(n_batch * seq, d)
    cs = ctx.reshape(n_batch * ctx_len, d)
    for l in range(depth):
        last = l == depth - 1
        lambda_init = 0.8 - 0.6 * math.exp(-0.3 * l)
        g_pre = row(norm_pre_mix, l)

        px = _in_proj(xs, g_pre, mods, l, grp_x, w_in_p, tm_x, 2 * SEC)
        pc = _in_proj(cs, g_pre, mods, l, grp_c, w_in_p, tm_c, 2 * SEC)

        qx, kx = _da_prep(px, cos_x, sin_x, tpb, tm_x, da_scale)
        qc, kc = _da_prep(pc, cos_c, sin_c, 1, tm_c, da_scale)
        da_params = (row(da_lam_q1, l), row(da_lam_k1, l), row(da_lam_q2, l), row(da_lam_k2, l), row(da_subln, l))
        y_da_x = _attention(qx, kc, pc, kx, px, da_params, n_batch, ctx_len, tq_da, tk, lambda_init, v_sec=2)

        mla_w = (row(mla_q_norm, l), row(mla_kv_norm, l), wq[l], wkv[l])
        mqx, mkx, mvx = _mla_prep(px, *mla_w, cos_x, sin_x, tpb, tm_x, mla_scale)
        mqc, mkc, mvc = _mla_prep(pc, *mla_w, cos_c, sin_c, 1, tm_c, mla_scale)
        y_mla_x = _attention(mqx, mkc, mvc, mkx, mvx, None, n_batch, ctx_len, tq_mla, tk, lambda_init)

        ys_c, ys_x = _s5_scan(pc.reshape(n_batch, ctx_len, -1), px.reshape(n_batch, seq, -1),
                              s5_lre[l], s5_lim[l], s5_lst[l], s5_btr[l], s5_bti[l], s5_ctr[l], s5_cti[l], n_batch)
        y_s5_x = _s5_glu(ys_x.reshape(2, n_batch * seq, SEC), px, row(s5_d, l), w_glu_b[l], tm_x)

        cv = (conv_w_p[l], row(conv_b, l), row(conv_ln_g, l), row(conv_ln_b, l))
        y_cv_x = _conv(px, *cv, n_batch, _row_tile(seq, 512))

        g_post, g_pre_f, g_post_f = row(norm_post_mix, l), row(norm_pre_ffn, l), row(norm_post_ffn, l)
        tm_o = _row_tile(tm_x, 512)
        x1 = _mix_out((y_da_x, y_s5_x, y_mla_x, y_cv_x), w_out_b, xs, g_post, mods, l,
                      lambda i: i // (seq // tm_o), tm_o)
        xs = _ffn(x1, g_pre_f, g_post_f, mods, l, lambda i: i // (seq // tm_o), w_ffn_in_b, w_ffn_out_b, tm_o, SEC)

        if not last:
            y_da_c = _attention(qc, kc, pc, None, None, da_params, n_batch, ctx_len, ctx_len, tk, lambda_init, v_sec=2)
            y_mla_c = _attention(mqc, mkc, mvc, None, None, None, n_batch, ctx_len, ctx_len, tk, lambda_init)
            y_s5_c = _s5_glu(ys_c.reshape(2, n_batch * ctx_len, SEC), pc, row(s5_d, l), w_glu_b[l], tm_c)
            y_cv_c = _conv(pc, *cv, n_batch, ctx_len)
            tm_oc = _row_tile(tm_c, 512)
            c1 = _mix_out((y_da_c, y_s5_c, y_mla_c, y_cv_c), w_out_b, cs, g_post, mods, l, grp_c, tm_oc)
            cs = _ffn(c1, g_pre_f, g_post_f, mods, l, grp_c, w_ffn_in_b, w_ffn_out_b, tm_oc, SEC)
    return xs.reshape(n_batch, seq, d)
```

```python
import functools
import math

import jax
import jax.numpy as jnp
from jax import lax
from jax.experimental import pallas as pl
from jax.experimental.pallas import tpu as pltpu

f32 = jnp.float32
bf16 = jnp.bfloat16

NORM_EPS = 1e-6
ROPE_BASE = 10000.0
GRID_W = 64
ROT_DIM = 64
CONV_K = 31
S5_GROUP = 16
S5_STATE = 64

LANES = 128
SUBLANES = 8
SEC = 512
N_SEC = 8
HEAD_W = 128
MLA_QK = 256
MLA_KV_RANK = 256
ATTN_UNIT = 256
V7X_VMEM_BYTES = 64 * 1024 * 1024
VMEM_LIMIT = V7X_VMEM_BYTES * 7 // 8


def _cp(*sem):
    return pltpu.CompilerParams(dimension_semantics=sem, vmem_limit_bytes=VMEM_LIMIT)


def _row_tile(rows, want):
    t = min(rows, want)
    assert rows % t == 0, (rows, want)
    return t


def _ada_kernel(cc_ref, w_ref, b_ref, o_ref):
    s = jax.nn.silu(cc_ref[...]).astype(bf16)
    o_ref[...] = jnp.dot(s, w_ref[...].astype(bf16), preferred_element_type=f32) + b_ref[...]


def _ada(cc, w_ada, b_ada):
    depth, d, n = w_ada.shape
    g = cc.shape[0]
    tn = 512
    return pl.pallas_call(
        _ada_kernel,
        grid=(depth, n // tn),
        in_specs=[pl.BlockSpec((g, d), lambda l, j: (0, 0)),
                  pl.BlockSpec((None, d, tn), lambda l, j: (l, 0, j)),
                  pl.BlockSpec((None, 1, tn), lambda l, j: (l, 0, j))],
        out_specs=pl.BlockSpec((None, g, tn), lambda l, j: (l, 0, j)),
        out_shape=jax.ShapeDtypeStruct((depth, g, n), f32),
        compiler_params=_cp("arbitrary", "arbitrary"),
        name="ada",
    )(cc, w_ada, b_ada.reshape(depth, 1, n))


def _norm_mod(x, g, shift, scale):
    ms = jnp.mean(x * x, axis=-1, keepdims=True)
    return (x * lax.rsqrt(ms + NORM_EPS) * g) * (1.0 + scale) + shift


def _in_proj_kernel(x_ref, g_ref, sh_ref, sc_ref, w_ref, o_ref, h_ref):
    @pl.when(pl.program_id(1) == 0)
    def _():
        h_ref[...] = _norm_mod(x_ref[...], g_ref[...], sh_ref[...], sc_ref[...]).astype(bf16)

    o_ref[...] = jnp.dot(h_ref[...], w_ref[...], preferred_element_type=f32).astype(o_ref.dtype)


def _residual(acc, x, g, gate):
    ms = jnp.mean(acc * acc, axis=-1, keepdims=True)
    return x + gate * (acc * lax.rsqrt(ms + NORM_EPS) * g)


def _ffn_kernel(x_ref, g_ref, sh_ref, sc_ref, wg_ref, wu_ref, wo_ref, g2_ref, gate_ref, o_ref, h_ref, acc_ref):
    j = pl.program_id(1)

    @pl.when(j == 0)
    def _():
        h_ref[...] = _norm_mod(x_ref[...], g_ref[...], sh_ref[...], sc_ref[...]).astype(bf16)
        acc_ref[...] = jnp.zeros_like(acc_ref)

    h = h_ref[...]
    a = jnp.dot(h, wg_ref[...], preferred_element_type=f32)
    b = jnp.dot(h, wu_ref[...], preferred_element_type=f32)
    act = (jax.nn.silu(a) * b).astype(bf16)
    acc_ref[...] += jnp.dot(act, wo_ref[...], preferred_element_type=f32)

    @pl.when(j == pl.num_programs(1) - 1)
    def _():
        o_ref[...] = _residual(acc_ref[...], x_ref[...], g2_ref[...], gate_ref[...])


def _mod_spec(l, grp, k, d):
    return pl.BlockSpec((None, None, 1, d), lambda i, j: (l, grp(i), 0, k))


def _in_proj(x, g, mods, l, grp, w, tm, tn):
    rows, d = x.shape
    n = w.shape[2]
    return pl.pallas_call(
        _in_proj_kernel,
        grid=(rows // tm, n // tn),
        in_specs=[pl.BlockSpec((tm, d), lambda i, j: (i, 0)),
                  pl.BlockSpec((1, d), lambda i, j: (0, 0)),
                  _mod_spec(l, grp, 0, d), _mod_spec(l, grp, 1, d),
                  pl.BlockSpec((None, d, tn), lambda i, j: (l, 0, j))],
        out_specs=pl.BlockSpec((tm, tn), lambda i, j: (i, j)),
        out_shape=jax.ShapeDtypeStruct((rows, n), bf16),
        scratch_shapes=[pltpu.VMEM((tm, d), bf16)],
        compiler_params=_cp("arbitrary", "arbitrary"),
        name="in_proj",
    )(x, g, mods, mods, w)


def _ffn(x, g_pre, g_post, mods, l, grp, w_in, w_out, tm, tn):
    rows, d = x.shape
    dff = w_out.shape[1]
    nj = dff // tn
    return pl.pallas_call(
        _ffn_kernel,
        grid=(rows // tm, nj),
        in_specs=[pl.BlockSpec((tm, d), lambda i, j: (i, 0)),
                  pl.BlockSpec((1, d), lambda i, j: (0, 0)),
                  _mod_spec(l, grp, 3, d), _mod_spec(l, grp, 4, d),
                  pl.BlockSpec((None, d, tn), lambda i, j: (l, 0, j)),
                  pl.BlockSpec((None, d, tn), lambda i, j: (l, 0, j + nj)),
                  pl.BlockSpec((None, tn, d), lambda i, j: (l, j, 0)),
                  pl.BlockSpec((1, d), lambda i, j: (0, 0)),
                  _mod_spec(l, grp, 5, d)],
        out_specs=pl.BlockSpec((tm, d), lambda i, j: (i, 0)),
        out_shape=jax.ShapeDtypeStruct((rows, d), f32),
        scratch_shapes=[pltpu.VMEM((tm, d), bf16), pltpu.VMEM((tm, d), f32)],
        compiler_params=_cp("arbitrary", "arbitrary"),
        name="ffn",
    )(x, g_pre, mods, mods, w_in, w_in, w_out, g_post, mods)


def _mix_out_kernel(y0_ref, y1_ref, y2_ref, y3_ref, w_ref, x_ref, g_ref, gate_ref, o_ref):
    tk = y0_ref.shape[1]
    acc = None
    for s, y_ref in enumerate((y0_ref, y1_ref, y2_ref, y3_ref)):
        part = jnp.dot(y_ref[...], w_ref[s * tk:(s + 1) * tk, :], preferred_element_type=f32)
        acc = part if acc is None else acc + part
    o_ref[...] = _residual(acc, x_ref[...], g_ref[...], gate_ref[...])


def _mix_out(ys, w, x, g, mods, l, grp, tm):
    rows, d = x.shape
    tk = ys[0].shape[1]
    mod = _mod_spec(l, grp, 2, d)
    return pl.pallas_call(
        _mix_out_kernel,
        grid=(rows // tm,),
        in_specs=[pl.BlockSpec((tm, tk), lambda i: (i, 0))] * len(ys) + [
            pl.BlockSpec((None, len(ys) * tk, d), lambda i: (l, 0, 0)),
            pl.BlockSpec((tm, d), lambda i: (i, 0)),
            pl.BlockSpec((1, d), lambda i: (0, 0)),
            pl.BlockSpec(mod.block_shape, lambda i: mod.index_map(i, 0))],
        out_specs=pl.BlockSpec((tm, d), lambda i: (i, 0)),
        out_shape=jax.ShapeDtypeStruct((rows, d), f32),
        compiler_params=_cp("arbitrary"),
        name="mix_out",
    )(*ys, w, x, g, mods)


def _rope_tables(n_tokens):
    n_rows = n_tokens // GRID_W
    row = jnp.repeat(jnp.arange(n_rows, dtype=f32), GRID_W)
    col = jnp.tile(jnp.arange(GRID_W, dtype=f32), n_rows)
    n_freq = ROT_DIM // 4
    inv = ROPE_BASE ** (-jnp.arange(n_freq, dtype=f32) / n_freq)
    ang = jnp.concatenate([row[:, None] * inv, col[:, None] * inv], axis=-1)
    cos, sin = jnp.cos(ang), jnp.sin(ang)
    return jnp.tile(cos, (1, 4)), jnp.tile(jnp.concatenate([-sin, sin], axis=-1), (1, 2))


def _rope_slab(x, cos, sin):
    lane = lax.broadcasted_iota(jnp.int32, x.shape, 1)
    partner = jnp.where(lane % ROT_DIM < ROT_DIM // 2,
                        pltpu.roll(x, LANES - ROT_DIM // 2, 1), pltpu.roll(x, ROT_DIM // 2, 1))
    return x * cos + partner * sin


def _da_prep_kernel(q_ref, k_ref, v_ref, cos_ref, sin_ref, qo_ref, ko_ref, vo_ref, *, scale):
    cos, sin = cos_ref[...], sin_ref[...]
    for h in range(SEC // LANES):
        sl = slice(h * LANES, (h + 1) * LANES)
        qo_ref[sl, :] = (_rope_slab(q_ref[:, sl].astype(f32), cos, sin) * scale).T.astype(bf16)
        ko_ref[:, sl] = _rope_slab(k_ref[:, sl].astype(f32), cos, sin).astype(bf16)
        vo_ref[sl, :] = v_ref[:, sl].astype(f32).T.astype(bf16)


def _da_prep(px, cos, sin, pos_tiles, tm, scale):
    rows = px.shape[0]
    sec = lambda s: pl.BlockSpec((tm, SEC), lambda i: (i, s))
    tab = pl.BlockSpec((tm, LANES), lambda i: (i % pos_tiles, 0))
    nat = pl.BlockSpec((tm, SEC), lambda i: (i, 0))
    tr = pl.BlockSpec((SEC, tm), lambda i: (0, i))
    return pl.pallas_call(
        functools.partial(_da_prep_kernel, scale=scale),
        grid=(rows // tm,),
        in_specs=[sec(0), sec(1), sec(2), tab, tab],
        out_specs=[tr, nat, tr],
        out_shape=[jax.ShapeDtypeStruct((SEC, rows), bf16), jax.ShapeDtypeStruct((rows, SEC), bf16),
                   jax.ShapeDtypeStruct((SEC, rows), bf16)],
        compiler_params=_cp("arbitrary"),
        name="da_prep",
    )(px, px, px, cos, sin)


def _rms_g(x, g):
    ms = jnp.mean(x * x, axis=-1, keepdims=True)
    return x * lax.rsqrt(ms + NORM_EPS) * g


def _mla_prep_kernel(cq_ref, ckv_ref, gq_ref, gkv_ref, wq_ref, wkv_ref, cos_ref, sin_ref,
                     qo_ref, ko_ref, vo_ref, *, scale):
    cos, sin = cos_ref[...], sin_ref[...]
    n_heads = SEC // HEAD_W
    q = jnp.dot(_rms_g(cq_ref[...].astype(f32), gq_ref[...]).astype(bf16), wq_ref[...],
                preferred_element_type=f32)
    kv = jnp.dot(_rms_g(ckv_ref[:, :MLA_KV_RANK].astype(f32), gkv_ref[...]).astype(bf16), wkv_ref[...],
                 preferred_element_type=f32)
    kr = _rope_slab(ckv_ref[:, MLA_KV_RANK:MLA_KV_RANK + LANES].astype(f32), cos, sin).astype(bf16)
    for h in range(n_heads):
        base = h * MLA_QK
        qo_ref[base:base + LANES, :] = (q[:, base:base + LANES] * scale).T.astype(bf16)
        qo_ref[base + LANES:base + MLA_QK, :] = (
            _rope_slab(q[:, base + LANES:base + MLA_QK], cos, sin) * scale).T.astype(bf16)
        ko_ref[:, base:base + LANES] = kv[:, h * LANES:(h + 1) * LANES].astype(bf16)
        ko_ref[:, base + LANES:base + MLA_QK] = kr
        vo_ref[h * LANES:(h + 1) * LANES, :] = kv[:, (n_heads + h) * LANES:(n_heads + h + 1) * LANES].T.astype(bf16)


def _mla_prep(px, gq, gkv, wq, wkv, cos, sin, pos_tiles, tm, scale):
    rows = px.shape[0]
    n_heads = SEC // HEAD_W
    sec = lambda s: pl.BlockSpec((tm, SEC), lambda i: (i, s))
    tab = pl.BlockSpec((tm, LANES), lambda i: (i % pos_tiles, 0))
    full = lambda a: pl.BlockSpec(a.shape, lambda i: (0,) * a.ndim)
    return pl.pallas_call(
        functools.partial(_mla_prep_kernel, scale=scale),
        grid=(rows // tm,),
        in_specs=[sec(4), sec(5), full(gq), full(gkv), full(wq), full(wkv), tab, tab],
        out_specs=[pl.BlockSpec((n_heads * MLA_QK, tm), lambda i: (0, i)),
                   pl.BlockSpec((tm, n_heads * MLA_QK), lambda i: (i, 0)),
                   pl.BlockSpec((SEC, tm), lambda i: (0, i))],
        out_shape=[jax.ShapeDtypeStruct((n_heads * MLA_QK, rows), bf16),
                   jax.ShapeDtypeStruct((rows, n_heads * MLA_QK), bf16),
                   jax.ShapeDtypeStruct((SEC, rows), bf16)],
        compiler_params=_cp("arbitrary"),
        name="mla_prep",
    )(px, px, gq, gkv, wq, wkv, cos, sin)


def _attn_kernel(*refs, diff, has_x_keys, tk, lambda_init):
    refs = list(refs)
    p_ref = refs.pop()
    s_ref = refs.pop()
    o_ref = refs.pop()
    q_ref, kc_ref, vc_ref = refs[:3]
    refs = refs[3:]
    n_ctx = kc_ref.shape[0]
    chunks = [(kc_ref, 0, 0, n_ctx)]
    if has_x_keys:
        kx_ref, vx_ref = refs[:2]
        refs = refs[2:]
        chunks += [(kx_ref, c * tk, n_ctx + c * tk, tk) for c in range(kx_ref.shape[0] // tk)]
    qt = q_ref[...]
    if diff:
        feat = lax.broadcasted_iota(jnp.int32, qt.shape, 0)
        zero = jnp.zeros_like(qt)
        maps = [jnp.where(feat < ROT_DIM, qt, zero), jnp.where(feat >= ROT_DIM, qt, zero)]
    else:
        maps = [qt]
    tq = s_ref.shape[2]
    n_col = qt.shape[1] // tq
    qs = [qm[:, c * tq:(c + 1) * tq] for qm in maps for c in range(n_col)]
    fold = lambda a: a.reshape(a.shape[0] // SUBLANES, SUBLANES, a.shape[1])

    m_acc = [jnp.full((SUBLANES, tq), -jnp.inf, f32) for _ in qs]
    for k_ref, start, row0, size in chunks:
        k = k_ref[pl.ds(start, size), :]
        for mi, qm in enumerate(qs):
            s = jnp.dot(k, qm, preferred_element_type=f32)
            s_ref[mi, pl.ds(row0, size), :] = s
            m_acc[mi] = jnp.maximum(m_acc[mi], jnp.max(fold(s), axis=0))
    m_col = [jnp.max(m, axis=0, keepdims=True) for m in m_acc]

    l_acc = [jnp.zeros((SUBLANES, tq), f32) for _ in qs]
    for _, _, row0, size in chunks:
        for mi in range(len(qs)):
            p = jnp.exp2(s_ref[mi, pl.ds(row0, size), :] - m_col[mi])
            l_acc[mi] = l_acc[mi] + jnp.sum(fold(p), axis=0)
            p_ref[mi, pl.ds(row0, size), :] = p.astype(bf16)

    outs = []
    for mi in range(len(qs)):
        ot = jnp.dot(vc_ref[...], p_ref[mi, pl.ds(0, n_ctx), :], preferred_element_type=f32)
        if has_x_keys:
            ot = ot + jnp.dot(vx_ref[...], p_ref[mi, pl.ds(n_ctx, vx_ref.shape[1]), :], preferred_element_type=f32)
        outs.append((ot / jnp.sum(l_acc[mi], axis=0, keepdims=True)).T)
    if diff:
        lq1, lk1, lq2, lk2, subln = (r[...] for r in refs)
        lam = (jnp.exp(jnp.sum(lq1 * lk1, axis=-1, keepdims=True))
               - jnp.exp(jnp.sum(lq2 * lk2, axis=-1, keepdims=True)) + lambda_init)
    for c in range(n_col):
        rows = pl.ds(c * tq, tq)
        if diff:
            dlt = outs[c] - lam * outs[n_col + c]
            o_ref[rows, :] = (_rms_g(dlt, subln) * (1.0 - lambda_init)).astype(o_ref.dtype)
        else:
            o_ref[rows, :] = outs[c].astype(o_ref.dtype)


def _attention(qt, kc, vct, kx, vxt, params, n_batch, ctx_len, tq, tk, lambda_init):
    n_heads = SEC // HEAD_W
    dq = qt.shape[0] // n_heads
    lq = qt.shape[1] // n_batch
    nq = lq // tq
    diff = params is not None
    in_specs = [pl.BlockSpec((dq, tq), lambda b, h, i: (h, b * nq + i)),
                pl.BlockSpec((ctx_len, dq), lambda b, h, i: (b, h)),
                pl.BlockSpec((HEAD_W, ctx_len), lambda b, h, i: (h, b))]
    args = [qt, kc, vct]
    n_keys = ctx_len
    if kx is not None:
        lx = kx.shape[0] // n_batch
        n_keys += lx
        in_specs += [pl.BlockSpec((lx, dq), lambda b, h, i: (b, h)),
                     pl.BlockSpec((HEAD_W, lx), lambda b, h, i: (h, b))]
        args += [kx, vxt]
    if diff:
        in_specs += [pl.BlockSpec(p.shape, lambda b, h, i: (0, 0)) for p in params]
        args += list(params)
    unit = min(tq, ATTN_UNIT)
    n_units = (2 if diff else 1) * (tq // unit)
    return pl.pallas_call(
        functools.partial(_attn_kernel, diff=diff, has_x_keys=kx is not None, tk=tk, lambda_init=lambda_init),
        grid=(n_batch, n_heads, nq),
        in_specs=in_specs,
        out_specs=pl.BlockSpec((tq, HEAD_W), lambda b, h, i: (b * nq + i, h)),
        out_shape=jax.ShapeDtypeStruct((qt.shape[1], n_heads * HEAD_W), bf16),
        scratch_shapes=[pltpu.VMEM((n_units, n_keys, unit), f32), pltpu.VMEM((n_units, n_keys, unit), bf16)],
        compiler_params=_cp("arbitrary", "arbitrary", "arbitrary"),
        name="attention",
    )(*args)


def _s5_kernel(uc_ref, ux_ref, lre_ref, lim_ref, lst_ref, btr_ref, bti_ref, ctr_ref, cti_ref,
               yc_ref, yx_ref, wb_re, wb_im, wc_re, wc_im, a_re, a_im, h_re, h_im, s_re, s_im):
    d = pl.program_id(0)
    c = pl.program_id(1)
    n_batch, t_len, width = uc_ref.shape
    n_state = lre_ref.shape[1]
    half = n_state // 2
    n_groups = width // S5_GROUP
    n_blk = half // LANES

    @pl.when(c == 0)
    def _():
        lre, lim = lre_ref[...], lim_ref[...]
        step = jnp.exp(lst_ref[...])
        mag = jnp.exp(lre * step)
        are = mag * jnp.cos(lim * step)
        aim = mag * jnp.sin(lim * step)
        den = lre * lre + lim * lim
        fre = ((are - 1.0) * lre + aim * lim) / den
        fim = (aim * lre - (are - 1.0) * lim) / den
        row = lax.broadcasted_iota(jnp.int32, (width, n_state), 0) // S5_GROUP
        col = lax.broadcasted_iota(jnp.int32, (width, n_state), 1) // S5_STATE
        blk = row == col
        tile = lambda r: jnp.broadcast_to(r[...][None], (n_groups, S5_GROUP, n_state)).reshape(width, n_state)
        btr, bti = tile(btr_ref), tile(bti_ref)
        wb_re[...] = jnp.where(blk, fre * btr - fim * bti, 0.0).astype(bf16)
        wb_im[...] = jnp.where(blk, fre * bti + fim * btr, 0.0).astype(bf16)
        wc_re[...] = jnp.where(blk, tile(ctr_ref), 0.0).astype(bf16)
        wc_im[...] = jnp.where(blk, -tile(cti_ref), 0.0).astype(bf16)
        for hf in range(2):
            rows = slice(hf * n_batch, (hf + 1) * n_batch)
            for j in range(n_blk):
                cols = slice(hf * half + j * LANES, hf * half + (j + 1) * LANES)
                a_re[j, rows, :] = jnp.broadcast_to(are[:, cols], (n_batch, LANES))
                a_im[j, rows, :] = jnp.broadcast_to(aim[:, cols], (n_batch, LANES))
        h_re[...] = jnp.zeros_like(h_re)
        h_im[...] = jnp.zeros_like(h_im)

    pitch = s_re.shape[1] // (2 * n_batch)
    kb = LANES
    sb = kb // S5_GROUP * S5_STATE
    nt = (((1,), (1,)), ((), ()))

    def slab(col0, b):
        hf, rem = divmod(col0, half)
        return rem // LANES, pl.ds((hf * n_batch + b) * pitch, t_len)

    def run(u_ref, y_ref):
        for b in range(n_batch):
            u = u_ref[b]
            for bb in range(width // kb):
                ub = u[:, bb * kb:(bb + 1) * kb]
                bre = jnp.dot(ub, wb_re[bb * kb:(bb + 1) * kb, bb * sb:(bb + 1) * sb], preferred_element_type=f32)
                bim = jnp.dot(ub, wb_im[bb * kb:(bb + 1) * kb, bb * sb:(bb + 1) * sb], preferred_element_type=f32)
                for jj in range(sb // LANES):
                    j, rows = slab(bb * sb + jj * LANES, b)
                    s_re[j, rows, :] = bre[:, jj * LANES:(jj + 1) * LANES]
                    s_im[j, rows, :] = bim[:, jj * LANES:(jj + 1) * LANES]

        ar = [a_re[j] for j in range(n_blk)]
        ai = [a_im[j] for j in range(n_blk)]

        def step(k, carry):
            hr, hi = carry
            t = k + d * (t_len - 1 - 2 * k)
            rows = pl.ds(t, 2 * n_batch, stride=pitch)
            nr, ni = [], []
            for j in range(n_blk):
                nr.append(ar[j] * hr[j] - ai[j] * hi[j] + s_re[j, rows, :])
                ni.append(ar[j] * hi[j] + ai[j] * hr[j] + s_im[j, rows, :])
                s_re[j, rows, :] = nr[j]
                s_im[j, rows, :] = ni[j]
            return nr, ni

        init = ([h_re[j] for j in range(n_blk)], [h_im[j] for j in range(n_blk)])
        hr, hi = lax.fori_loop(0, t_len, step, init)
        for j in range(n_blk):
            h_re[j] = hr[j]
            h_im[j] = hi[j]

        for b in range(n_batch):
            ys = []
            for bb in range(width // kb):
                slabs = [slab(bb * sb + jj * LANES, b) for jj in range(sb // LANES)]
                sr = jnp.concatenate([s_re[j, rows, :] for j, rows in slabs], axis=1).astype(bf16)
                si = jnp.concatenate([s_im[j, rows, :] for j, rows in slabs], axis=1).astype(bf16)
                ys.append(
                    lax.dot_general(sr, wc_re[bb * kb:(bb + 1) * kb, bb * sb:(bb + 1) * sb], nt,
                                    preferred_element_type=f32)
                    + lax.dot_general(si, wc_im[bb * kb:(bb + 1) * kb, bb * sb:(bb + 1) * sb], nt,
                                      preferred_element_type=f32))
            y_ref[b] = jnp.concatenate(ys, axis=1)

    @pl.when(c == 0)
    def _():
        run(uc_ref, yc_ref)

    @pl.when(c > 0)
    def _():
        run(ux_ref, yx_ref)


def _s5_scan(px_c, px_x, lam_re, lam_im, log_step, bt_re, bt_im, ct_re, ct_im, n_batch):
    t_len = px_c.shape[1]
    seq = px_x.shape[1]
    assert seq % t_len == 0
    nch = seq // t_len
    n_state = lam_re.shape[-1]
    half = n_state // 2
    pitch = t_len + SUBLANES
    assert t_len % (2 * SUBLANES) == 0

    def x_chunk(d, c):
        return jnp.where(d == 0, jnp.maximum(c - 1, 0), jnp.minimum(nch - c, nch - 1))

    vec = pl.BlockSpec((None, 1, n_state), lambda d, c: (d, 0, 0))
    mat = pl.BlockSpec((None, S5_GROUP, n_state), lambda d, c: (d, 0, 0))
    return pl.pallas_call(
        _s5_kernel,
        grid=(2, nch + 1),
        in_specs=[pl.BlockSpec((n_batch, t_len, SEC), lambda d, c: (0, 0, 3)),
                  pl.BlockSpec((n_batch, t_len, SEC), lambda d, c: (0, x_chunk(d, c), 3)),
                  vec, vec, vec, mat, mat, mat, mat],
        out_specs=[pl.BlockSpec((None, n_batch, t_len, SEC), lambda d, c: (d, 0, 0, 0)),
                   pl.BlockSpec((None, n_batch, t_len, SEC), lambda d, c: (d, 0, x_chunk(d, c), 0))],
        out_shape=[jax.ShapeDtypeStruct((2, n_batch, t_len, SEC), f32),
                   jax.ShapeDtypeStruct((2, n_batch, seq, SEC), f32)],
        scratch_shapes=[pltpu.VMEM((SEC, n_state), bf16)] * 4
        + [pltpu.VMEM((half // LANES, 2 * n_batch, LANES), f32)] * 4
        + [pltpu.VMEM((half // LANES, 2 * n_batch * pitch, LANES), f32)] * 2,
        compiler_params=_cp("arbitrary", "arbitrary"),
        name="s5_scan",
    )(px_c, px_x, lam_re, lam_im, log_step, bt_re, bt_im, ct_re, ct_im)


def _s5_glu_kernel(yf_ref, yb_ref, u_ref, d_ref, w_ref, o_ref):
    y = d_ref[...] * u_ref[...].astype(f32) + yf_ref[...] + yb_ref[...]
    g = jax.nn.gelu(y)
    z = jnp.dot(g.astype(bf16), w_ref[...], preferred_element_type=f32)
    o_ref[...] = (g * jax.nn.sigmoid(z)).astype(o_ref.dtype)


def _s5_glu(y, px, s5_d, w_glu, tm):
    rows = px.shape[0]
    return pl.pallas_call(
        _s5_glu_kernel,
        grid=(rows // tm,),
        in_specs=[pl.BlockSpec((None, tm, SEC), lambda i: (0, i, 0)),
                  pl.BlockSpec((None, tm, SEC), lambda i: (1, i, 0)),
                  pl.BlockSpec((tm, SEC), lambda i: (i, 3)),
                  pl.BlockSpec((1, SEC), lambda i: (0, 0)),
                  pl.BlockSpec((SEC, SEC), lambda i: (0, 0))],
        out_specs=pl.BlockSpec((tm, SEC), lambda i: (i, 0)),
        out_shape=jax.ShapeDtypeStruct((rows, SEC), bf16),
        compiler_params=_cp("arbitrary"),
        name="s5_glu",
    )(y, y, px, s5_d, w_glu)


CONV_HALO = 16


def _conv_kernel(vp_ref, vc_ref, vn_ref, gp_ref, gc_ref, gn_ref, w_ref, b_ref, lg_ref, lb_ref, o_ref, ext_ref):
    j = pl.program_id(1)
    tc = vc_ref.shape[0]
    glu = lambda v, g: v[...].astype(f32) * jax.nn.sigmoid(g[...].astype(f32))
    ext_ref[0, pl.ds(0, CONV_HALO), :] = jnp.where(j > 0, glu(vp_ref, gp_ref), 0.0)
    ext_ref[0, pl.ds(CONV_HALO, tc), :] = glu(vc_ref, gc_ref)
    ext_ref[0, pl.ds(CONV_HALO + tc, CONV_HALO), :] = jnp.where(j < pl.num_programs(1) - 1, glu(vn_ref, gn_ref), 0.0)
    n_keep = tc + 2 * CONV_HALO - SUBLANES
    for s in range(1, SUBLANES):
        ext_ref[s, pl.ds(0, n_keep), :] = ext_ref[0, pl.ds(s, n_keep), :]
    y = jnp.zeros((tc, vc_ref.shape[1]), f32)
    for k in range(CONV_K):
        off = CONV_HALO - CONV_K // 2 + k
        y += w_ref[pl.ds(k, 1), :] * ext_ref[off % SUBLANES, pl.ds(off - off % SUBLANES, tc), :]
    y = y + b_ref[...]
    mu = jnp.mean(y, axis=-1, keepdims=True)
    var = jnp.mean(jnp.square(y - mu), axis=-1, keepdims=True)
    z = (y - mu) * lax.rsqrt(var + NORM_EPS) * lg_ref[...] + lb_ref[...]
    o_ref[...] = jax.nn.silu(z).astype(o_ref.dtype)


def _conv(px, conv_w, conv_b, ln_g, ln_b, n_batch, tc):
    rows = px.shape[0]
    seq = rows // n_batch
    nt = seq // tc
    hpt = tc // CONV_HALO
    cur = lambda s: pl.BlockSpec((tc, SEC), lambda b, j: (b * nt + j, s))
    prev = lambda s: pl.BlockSpec((CONV_HALO, SEC), lambda b, j: (jnp.maximum((b * nt + j) * hpt - 1, 0), s))
    nxt = lambda s: pl.BlockSpec(
        (CONV_HALO, SEC), lambda b, j: (jnp.minimum((b * nt + j + 1) * hpt, rows // CONV_HALO - 1), s))
    full = lambda a: pl.BlockSpec(a.shape, lambda b, j: (0,) * a.ndim)
    return pl.pallas_call(
        _conv_kernel,
        grid=(n_batch, nt),
        in_specs=[prev(6), cur(6), nxt(6), prev(7), cur(7), nxt(7),
                  full(conv_w), full(conv_b), full(ln_g), full(ln_b)],
        out_specs=pl.BlockSpec((tc, SEC), lambda b, j: (b * nt + j, 0)),
        out_shape=jax.ShapeDtypeStruct((rows, SEC), bf16),
        scratch_shapes=[pltpu.VMEM((SUBLANES, tc + 2 * CONV_HALO, SEC), f32)],
        compiler_params=_cp("arbitrary", "arbitrary"),
        name="conv",
    )(px, px, px, px, px, px, conv_w, conv_b, ln_g, ln_b)


def kernel(x, c, ctx, c_ctx, w_ada, b_ada, norm_pre_mix, norm_post_mix, norm_pre_ffn, norm_post_ffn, w_in, w_out, da_lam_q1, da_lam_k1, da_lam_q2, da_lam_k2, da_subln, s5_lam_re, s5_lam_im, s5_log_step, s5_b_re, s5_b_im, s5_c_re, s5_c_im, s5_d, s5_w_glu, mla_q_norm, mla_kv_norm, mla_w_uq, mla_w_ukv, conv_w, conv_b, conv_ln_g, conv_ln_b, w_ffn_in, w_ffn_out):
    n_batch, seq, d = x.shape
    ctx_len = ctx.shape[1]
    depth = w_in.shape[0]
    n_state = s5_lam_re.shape[2] * s5_lam_re.shape[3]
    mla_end = 4 * SEC + SEC + MLA_KV_RANK + ROT_DIM
    assert d == 4 * SEC and w_in.shape[2] == mla_end + 2 * SEC

    da_scale = ROT_DIM ** -0.5 * math.log2(math.e)
    mla_scale = (HEAD_W + ROT_DIM) ** -0.5 * math.log2(math.e)

    tm_x = _row_tile(seq, 1024)
    tm_c = _row_tile(n_batch * ctx_len, 1024)
    tpb = seq // tm_x
    grp_x = lambda i: i // tpb
    grp_c = lambda i: n_batch
    tq_da = _row_tile(seq, 512)
    tq_mla = _row_tile(seq, 1024)
    tk = _row_tile(seq, 512)

    n_grp = -(-(n_batch + 1) // SUBLANES) * SUBLANES
    cc = jnp.concatenate([c, c_ctx[None], jnp.zeros((n_grp - n_batch - 1, d), f32)], axis=0)
    mods = _ada(cc, w_ada, b_ada).reshape(depth, n_grp, 1, 6 * d)

    cos_x, sin_x = _rope_tables(seq)
    cos_c, sin_c = jnp.ones((tm_c, LANES), f32), jnp.zeros((tm_c, LANES), f32)

    w_in_p = jnp.concatenate([w_in[:, :, :mla_end], jnp.zeros((depth, d, N_SEC * SEC - w_in.shape[2]), f32),
                              w_in[:, :, mla_end:]], axis=2).astype(bf16)
    w_out_b = w_out.astype(bf16)
    w_ffn_in_b = w_ffn_in.astype(bf16)
    w_ffn_out_b = w_ffn_out.astype(bf16)
    w_glu_b = s5_w_glu.astype(bf16)
    n_mla = SEC // HEAD_W
    wq = mla_w_uq.reshape(depth, SEC, n_mla, HEAD_W + ROT_DIM)
    wq = jnp.pad(wq, ((0, 0), (0, 0), (0, 0), (0, MLA_QK - HEAD_W - ROT_DIM))).reshape(depth, SEC, n_mla * MLA_QK)
    wq = wq.astype(bf16)
    wkv = mla_w_ukv.reshape(depth, MLA_KV_RANK, n_mla, 2, HEAD_W).transpose(0, 1, 3, 2, 4)
    wkv = wkv.reshape(depth, MLA_KV_RANK, 2 * n_mla * HEAD_W).astype(bf16)

    flat = lambda a: a.reshape(depth, 2, 1, n_state)
    s5_lre, s5_lim = flat(s5_lam_re), flat(s5_lam_im)
    s5_lst = flat(jnp.broadcast_to(s5_log_step[..., None], s5_lam_re.shape))
    s5_btr = s5_b_re.transpose(0, 1, 4, 2, 3).reshape(depth, 2, S5_GROUP, n_state)
    s5_bti = s5_b_im.transpose(0, 1, 4, 2, 3).reshape(depth, 2, S5_GROUP, n_state)
    s5_ctr = s5_c_re.transpose(0, 1, 3, 2, 4).reshape(depth, 2, S5_GROUP, n_state)
    s5_cti = s5_c_im.transpose(0, 1, 3, 2, 4).reshape(depth, 2, S5_GROUP, n_state)

    row = lambda a, l: a[l][None]
    conv_w_p = jnp.pad(conv_w, ((0, 0), (0, 1), (0, 0)))

    xs = x.reshape(n_batch * seq, d)
    cs = ctx.reshape(n_batch * ctx_len, d)
    for l in range(depth):
        last = l == depth - 1
        lambda_init = 0.8 - 0.6 * math.exp(-0.3 * l)
        g_pre = row(norm_pre_mix, l)

        px = _in_proj(xs, g_pre, mods, l, grp_x, w_in_p, tm_x, 2 * SEC)
        pc = _in_proj(cs, g_pre, mods, l, grp_c, w_in_p, tm_c, 2 * SEC)

        qx, kx, vx = _da_prep(px, cos_x, sin_x, tpb, tm_x, da_scale)
        qc, kc, vc = _da_prep(pc, cos_c, sin_c, 1, tm_c, da_scale)
        da_params = (row(da_lam_q1, l), row(da_lam_k1, l), row(da_lam_q2, l), row(da_lam_k2, l), row(da_subln, l))
        y_da_x = _attention(qx, kc, vc, kx, vx, da_params, n_batch, ctx_len, tq_da, tk, lambda_init)

        mla_w = (row(mla_q_norm, l), row(mla_kv_norm, l), wq[l], wkv[l])
        mqx, mkx, mvx = _mla_prep(px, *mla_w, cos_x, sin_x, tpb, tm_x, mla_scale)
        mqc, mkc, mvc = _mla_prep(pc, *mla_w, cos_c, sin_c, 1, tm_c, mla_scale)
        y_mla_x = _attention(mqx, mkc, mvc, mkx, mvx, None, n_batch, ctx_len, tq_mla, tk, lambda_init)

        ys_c, ys_x = _s5_scan(pc.reshape(n_batch, ctx_len, -1), px.reshape(n_batch, seq, -1),
                              s5_lre[l], s5_lim[l], s5_lst[l], s5_btr[l], s5_bti[l], s5_ctr[l], s5_cti[l], n_batch)
        y_s5_x = _s5_glu(ys_x.reshape(2, n_batch * seq, SEC), px, row(s5_d, l), w_glu_b[l], tm_x)

        cv = (conv_w_p[l], row(conv_b, l), row(conv_ln_g, l), row(conv_ln_b, l))
        y_cv_x = _conv(px, *cv, n_batch, _row_tile(seq, 512))

        g_post, g_pre_f, g_post_f = row(norm_post_mix, l), row(norm_pre_ffn, l), row(norm_post_ffn, l)
        tm_o = _row_tile(tm_x, 512)
        x1 = _mix_out((y_da_x, y_s5_x, y_mla_x, y_cv_x), w_out_b, xs, g_post, mods, l,
                      lambda i: i // (seq // tm_o), tm_o)
        xs = _ffn(x1, g_pre_f, g_post_f, mods, l, lambda i: i // (seq // tm_o), w_ffn_in_b, w_ffn_out_b, tm_o, SEC)

        if not last:
            y_da_c = _attention(qc, kc, vc, None, None, da_params, n_batch, ctx_len, ctx_len, tk, lambda_init)
            y_mla_c = _attention(mqc, mkc, mvc, None, None, None, n_batch, ctx_len, ctx_len, tk, lambda_init)
            y_s5_c = _s5_glu(ys_c.reshape(2, n_batch * ctx_len, SEC), pc, row(s5_d, l), w_glu_b[l], tm_c)
            y_cv_c = _conv(pc, *cv, n_batch, ctx_len)
            tm_oc = _row_tile(tm_c, 512)
            c1 = _mix_out((y_da_c, y_s5_c, y_mla_c, y_cv_c), w_out_b, cs, g_post, mods, l, grp_c, tm_oc)
            cs = _ffn(c1, g_pre_f, g_post_f, mods, l, grp_c, w_ffn_in_b, w_ffn_out_b, tm_oc, SEC)
    return xs.reshape(n_batch, seq, d)
```

```python
import functools
import math

import jax
import jax.numpy as jnp
from jax import lax
from jax.experimental import pallas as pl
from jax.experimental.pallas import tpu as pltpu

f32 = jnp.float32
bf16 = jnp.bfloat16

NORM_EPS = 1e-6
ROPE_BASE = 10000.0
GRID_W = 64
ROT_DIM = 64
CONV_K = 31
S5_GROUP = 16
S5_STATE = 64

LANES = 128
SUBLANES = 8
SEC = 512
N_SEC = 8
HEAD_W = 128
MLA_QK = 256
MLA_KV_RANK = 256
ATTN_UNIT = 256
ACC_ROWS = SUBLANES
V7X_VMEM_BYTES = 64 * 1024 * 1024
VMEM_LIMIT = V7X_VMEM_BYTES * 7 // 8


def _cp(*sem):
    return pltpu.CompilerParams(dimension_semantics=sem, vmem_limit_bytes=VMEM_LIMIT)


def _row_tile(rows, want):
    t = min(rows, want)
    assert rows % t == 0, (rows, want)
    return t


def _ada_kernel(cc_ref, w_ref, b_ref, o_ref):
    s = jax.nn.silu(cc_ref[...]).astype(bf16)
    o_ref[...] = jnp.dot(s, w_ref[...].astype(bf16), preferred_element_type=f32) + b_ref[...]


def _ada(cc, w_ada, b_ada):
    depth, d, n = w_ada.shape
    g = cc.shape[0]
    tn = 512
    return pl.pallas_call(
        _ada_kernel,
        grid=(depth, n // tn),
        in_specs=[pl.BlockSpec((g, d), lambda l, j: (0, 0)),
                  pl.BlockSpec((None, d, tn), lambda l, j: (l, 0, j)),
                  pl.BlockSpec((None, 1, tn), lambda l, j: (l, 0, j))],
        out_specs=pl.BlockSpec((None, g, tn), lambda l, j: (l, 0, j)),
        out_shape=jax.ShapeDtypeStruct((depth, g, n), f32),
        compiler_params=_cp("arbitrary", "arbitrary"),
        name="ada",
    )(cc, w_ada, b_ada.reshape(depth, 1, n))


def _row_chunks(n_rows, chunk, body):
    def step(i, carry):
        body(pl.ds(pl.multiple_of(i * chunk, chunk), chunk))
        return carry

    lax.fori_loop(0, n_rows // chunk, step, 0, unroll=8)


def _norm_mod_rows(x_ref, g_ref, sh_ref, sc_ref, h_ref):
    g, shift, scale1 = g_ref[...], sh_ref[...], 1.0 + sc_ref[...]

    def body(rows):
        x = x_ref[rows, :]
        ms = jnp.mean(x * x, axis=-1, keepdims=True)
        h_ref[rows, :] = ((x * lax.rsqrt(ms + NORM_EPS) * g) * scale1 + shift).astype(bf16)

    _row_chunks(x_ref.shape[0], 2 * SUBLANES, body)


def _residual_rows(acc_ref, x_ref, g_ref, gate_ref, o_ref):
    g, gate = g_ref[...], gate_ref[...]

    def body(rows):
        acc = acc_ref[rows, :]
        ms = jnp.mean(acc * acc, axis=-1, keepdims=True)
        o_ref[rows, :] = x_ref[rows, :] + gate * (acc * lax.rsqrt(ms + NORM_EPS) * g)

    _row_chunks(x_ref.shape[0], SUBLANES, body)


def _in_proj_kernel(x_ref, g_ref, sh_ref, sc_ref, w_ref, o_ref, h_ref):
    @pl.when(pl.program_id(1) == 0)
    def _():
        _norm_mod_rows(x_ref, g_ref, sh_ref, sc_ref, h_ref)

    o_ref[...] = jnp.dot(h_ref[...], w_ref[...], preferred_element_type=f32).astype(o_ref.dtype)


def _ffn_kernel(x_ref, g_ref, sh_ref, sc_ref, wg_ref, wu_ref, wo_ref, g2_ref, gate_ref, o_ref, h_ref, acc_ref):
    j = pl.program_id(1)

    @pl.when(j == 0)
    def _():
        _norm_mod_rows(x_ref, g_ref, sh_ref, sc_ref, h_ref)
        acc_ref[...] = jnp.zeros_like(acc_ref)

    h = h_ref[...]
    a = jnp.dot(h, wg_ref[...], preferred_element_type=f32)
    b = jnp.dot(h, wu_ref[...], preferred_element_type=f32)
    act = (jax.nn.silu(a) * b).astype(bf16)
    acc_ref[...] += jnp.dot(act, wo_ref[...], preferred_element_type=f32)

    @pl.when(j == pl.num_programs(1) - 1)
    def _():
        _residual_rows(acc_ref, x_ref, g2_ref, gate_ref, o_ref)


def _mod_spec(l, grp, k, d):
    return pl.BlockSpec((None, None, 1, d), lambda i, j: (l, grp(i), 0, k))


def _in_proj(x, g, mods, l, grp, w, tm, tn):
    rows, d = x.shape
    n = w.shape[2]
    return pl.pallas_call(
        _in_proj_kernel,
        grid=(rows // tm, n // tn),
        in_specs=[pl.BlockSpec((tm, d), lambda i, j: (i, 0)),
                  pl.BlockSpec((1, d), lambda i, j: (0, 0)),
                  _mod_spec(l, grp, 0, d), _mod_spec(l, grp, 1, d),
                  pl.BlockSpec((None, d, tn), lambda i, j: (l, 0, j))],
        out_specs=pl.BlockSpec((tm, tn), lambda i, j: (i, j)),
        out_shape=jax.ShapeDtypeStruct((rows, n), bf16),
        scratch_shapes=[pltpu.VMEM((tm, d), bf16)],
        compiler_params=_cp("arbitrary", "arbitrary"),
        name="in_proj",
    )(x, g, mods, mods, w)


def _ffn(x, g_pre, g_post, mods, l, grp, w_in, w_out, tm, tn):
    rows, d = x.shape
    dff = w_out.shape[1]
    nj = dff // tn
    return pl.pallas_call(
        _ffn_kernel,
        grid=(rows // tm, nj),
        in_specs=[pl.BlockSpec((tm, d), lambda i, j: (i, 0)),
                  pl.BlockSpec((1, d), lambda i, j: (0, 0)),
                  _mod_spec(l, grp, 3, d), _mod_spec(l, grp, 4, d),
                  pl.BlockSpec((None, d, tn), lambda i, j: (l, 0, j)),
                  pl.BlockSpec((None, d, tn), lambda i, j: (l, 0, j + nj)),
                  pl.BlockSpec((None, tn, d), lambda i, j: (l, j, 0)),
                  pl.BlockSpec((1, d), lambda i, j: (0, 0)),
                  _mod_spec(l, grp, 5, d)],
        out_specs=pl.BlockSpec((tm, d), lambda i, j: (i, 0)),
        out_shape=jax.ShapeDtypeStruct((rows, d), f32),
        scratch_shapes=[pltpu.VMEM((tm, d), bf16), pltpu.VMEM((tm, d), f32)],
        compiler_params=_cp("arbitrary", "arbitrary"),
        name="ffn",
    )(x, g_pre, mods, mods, w_in, w_in, w_out, g_post, mods)


def _mix_out_kernel(y0_ref, y1_ref, y2_ref, y3_ref, w_ref, x_ref, g_ref, gate_ref, o_ref, acc_ref):
    tk = y0_ref.shape[1]
    acc = None
    for s, y_ref in enumerate((y0_ref, y1_ref, y2_ref, y3_ref)):
        part = jnp.dot(y_ref[...], w_ref[s * tk:(s + 1) * tk, :], preferred_element_type=f32)
        acc = part if acc is None else acc + part
    acc_ref[...] = acc
    _residual_rows(acc_ref, x_ref, g_ref, gate_ref, o_ref)


def _mix_out(ys, w, x, g, mods, l, grp, tm):
    rows, d = x.shape
    tk = ys[0].shape[1]
    mod = _mod_spec(l, grp, 2, d)
    return pl.pallas_call(
        _mix_out_kernel,
        grid=(rows // tm,),
        in_specs=[pl.BlockSpec((tm, tk), lambda i: (i, 0))] * len(ys) + [
            pl.BlockSpec((None, len(ys) * tk, d), lambda i: (l, 0, 0)),
            pl.BlockSpec((tm, d), lambda i: (i, 0)),
            pl.BlockSpec((1, d), lambda i: (0, 0)),
            pl.BlockSpec(mod.block_shape, lambda i: mod.index_map(i, 0))],
        out_specs=pl.BlockSpec((tm, d), lambda i: (i, 0)),
        out_shape=jax.ShapeDtypeStruct((rows, d), f32),
        scratch_shapes=[pltpu.VMEM((tm, d), f32)],
        compiler_params=_cp("arbitrary"),
        name="mix_out",
    )(*ys, w, x, g, mods)


def _rope_tables(n_tokens):
    n_rows = n_tokens // GRID_W
    row = jnp.repeat(jnp.arange(n_rows, dtype=f32), GRID_W)
    col = jnp.tile(jnp.arange(GRID_W, dtype=f32), n_rows)
    n_freq = ROT_DIM // 4
    inv = ROPE_BASE ** (-jnp.arange(n_freq, dtype=f32) / n_freq)
    ang = jnp.concatenate([row[:, None] * inv, col[:, None] * inv], axis=-1)
    cos, sin = jnp.cos(ang), jnp.sin(ang)
    return jnp.tile(cos, (1, 4)), jnp.tile(jnp.concatenate([-sin, sin], axis=-1), (1, 2))


def _rope_slab(x, cos, sin):
    lane = lax.broadcasted_iota(jnp.int32, x.shape, 1)
    partner = jnp.where(lane % ROT_DIM < ROT_DIM // 2,
                        pltpu.roll(x, LANES - ROT_DIM // 2, 1), pltpu.roll(x, ROT_DIM // 2, 1))
    return x * cos + partner * sin


def _da_prep_kernel(q_ref, k_ref, v_ref, cos_ref, sin_ref, qo_ref, ko_ref, vo_ref, *, scale):
    cos, sin = cos_ref[...], sin_ref[...]
    for h in range(SEC // LANES):
        sl = slice(h * LANES, (h + 1) * LANES)
        qo_ref[sl, :] = (_rope_slab(q_ref[:, sl].astype(f32), cos, sin) * scale).T.astype(bf16)
        ko_ref[:, sl] = _rope_slab(k_ref[:, sl].astype(f32), cos, sin).astype(bf16)
        vo_ref[sl, :] = v_ref[:, sl].astype(f32).T.astype(bf16)


def _da_prep(px, cos, sin, pos_tiles, tm, scale):
    rows = px.shape[0]
    sec = lambda s: pl.BlockSpec((tm, SEC), lambda i: (i, s))
    tab = pl.BlockSpec((tm, LANES), lambda i: (i % pos_tiles, 0))
    nat = pl.BlockSpec((tm, SEC), lambda i: (i, 0))
    tr = pl.BlockSpec((SEC, tm), lambda i: (0, i))
    return pl.pallas_call(
        functools.partial(_da_prep_kernel, scale=scale),
        grid=(rows // tm,),
        in_specs=[sec(0), sec(1), sec(2), tab, tab],
        out_specs=[tr, nat, tr],
        out_shape=[jax.ShapeDtypeStruct((SEC, rows), bf16), jax.ShapeDtypeStruct((rows, SEC), bf16),
                   jax.ShapeDtypeStruct((SEC, rows), bf16)],
        compiler_params=_cp("arbitrary"),
        name="da_prep",
    )(px, px, px, cos, sin)


def _rms_g(x, g):
    ms = jnp.mean(x * x, axis=-1, keepdims=True)
    return x * lax.rsqrt(ms + NORM_EPS) * g


def _mla_prep_kernel(cq_ref, ckv_ref, gq_ref, gkv_ref, wq_ref, wkv_ref, cos_ref, sin_ref,
                     qo_ref, ko_ref, vo_ref, *, scale):
    cos, sin = cos_ref[...], sin_ref[...]
    n_heads = SEC // HEAD_W
    q = jnp.dot(_rms_g(cq_ref[...].astype(f32), gq_ref[...]).astype(bf16), wq_ref[...],
                preferred_element_type=f32)
    kv = jnp.dot(_rms_g(ckv_ref[:, :MLA_KV_RANK].astype(f32), gkv_ref[...]).astype(bf16), wkv_ref[...],
                 preferred_element_type=f32)
    kr = _rope_slab(ckv_ref[:, MLA_KV_RANK:MLA_KV_RANK + LANES].astype(f32), cos, sin).astype(bf16)
    for h in range(n_heads):
        base = h * MLA_QK
        qo_ref[base:base + LANES, :] = (q[:, base:base + LANES] * scale).T.astype(bf16)
        qo_ref[base + LANES:base + MLA_QK, :] = (
            _rope_slab(q[:, base + LANES:base + MLA_QK], cos, sin) * scale).T.astype(bf16)
        ko_ref[:, base:base + LANES] = kv[:, h * LANES:(h + 1) * LANES].astype(bf16)
        ko_ref[:, base + LANES:base + MLA_QK] = kr
        vo_ref[h * LANES:(h + 1) * LANES, :] = kv[:, (n_heads + h) * LANES:(n_heads + h + 1) * LANES].T.astype(bf16)


def _mla_prep(px, gq, gkv, wq, wkv, cos, sin, pos_tiles, tm, scale):
    rows = px.shape[0]
    n_heads = SEC // HEAD_W
    sec = lambda s: pl.BlockSpec((tm, SEC), lambda i: (i, s))
    tab = pl.BlockSpec((tm, LANES), lambda i: (i % pos_tiles, 0))
    full = lambda a: pl.BlockSpec(a.shape, lambda i: (0,) * a.ndim)
    return pl.pallas_call(
        functools.partial(_mla_prep_kernel, scale=scale),
        grid=(rows // tm,),
        in_specs=[sec(4), sec(5), full(gq), full(gkv), full(wq), full(wkv), tab, tab],
        out_specs=[pl.BlockSpec((n_heads * MLA_QK, tm), lambda i: (0, i)),
                   pl.BlockSpec((tm, n_heads * MLA_QK), lambda i: (i, 0)),
                   pl.BlockSpec((SEC, tm), lambda i: (0, i))],
        out_shape=[jax.ShapeDtypeStruct((n_heads * MLA_QK, rows), bf16),
                   jax.ShapeDtypeStruct((rows, n_heads * MLA_QK), bf16),
                   jax.ShapeDtypeStruct((SEC, rows), bf16)],
        compiler_params=_cp("arbitrary"),
        name="mla_prep",
    )(px, px, gq, gkv, wq, wkv, cos, sin)


def _attn_kernel(*refs, diff, has_x_keys, tk, lambda_init):
    refs = list(refs)
    p_ref = refs.pop()
    s_ref = refs.pop()
    o_ref = refs.pop()
    q_ref, kc_ref, vc_ref = refs[:3]
    refs = refs[3:]
    n_ctx = kc_ref.shape[0]
    chunks = [(kc_ref, 0, 0, n_ctx)]
    if has_x_keys:
        kx_ref, vx_ref = refs[:2]
        refs = refs[2:]
        chunks += [(kx_ref, c * tk, n_ctx + c * tk, tk) for c in range(kx_ref.shape[0] // tk)]
    qt = q_ref[...]
    if diff:
        feat = lax.broadcasted_iota(jnp.int32, qt.shape, 0)
        zero = jnp.zeros_like(qt)
        maps = [jnp.where(feat < ROT_DIM, qt, zero), jnp.where(feat >= ROT_DIM, qt, zero)]
    else:
        maps = [qt]
    tq = s_ref.shape[2]
    n_col = qt.shape[1] // tq
    qs = [qm[:, c * tq:(c + 1) * tq] for qm in maps for c in range(n_col)]
    fold = lambda a: a.reshape(a.shape[0] // ACC_ROWS, ACC_ROWS, a.shape[1])

    m_acc = [jnp.full((ACC_ROWS, tq), -jnp.inf, f32) for _ in qs]
    for k_ref, start, row0, size in chunks:
        k = k_ref[pl.ds(start, size), :]
        for mi, qm in enumerate(qs):
            s = jnp.dot(k, qm, preferred_element_type=f32)
            s_ref[mi, pl.ds(row0, size), :] = s
            m_acc[mi] = jnp.maximum(m_acc[mi], jnp.max(fold(s), axis=0))
    m_col = [jnp.max(m, axis=0, keepdims=True) for m in m_acc]

    l_acc = [jnp.zeros((ACC_ROWS, tq), f32) for _ in qs]
    for _, _, row0, size in chunks:
        for mi in range(len(qs)):
            p = jnp.exp2(s_ref[mi, pl.ds(row0, size), :] - m_col[mi])
            l_acc[mi] = l_acc[mi] + jnp.sum(fold(p), axis=0)
            p_ref[mi, pl.ds(row0, size), :] = p.astype(bf16)

    outs = []
    for mi in range(len(qs)):
        ot = jnp.dot(vc_ref[...], p_ref[mi, pl.ds(0, n_ctx), :], preferred_element_type=f32)
        if has_x_keys:
            ot = ot + jnp.dot(vx_ref[...], p_ref[mi, pl.ds(n_ctx, vx_ref.shape[1]), :], preferred_element_type=f32)
        outs.append((ot / jnp.sum(l_acc[mi], axis=0, keepdims=True)).T)
    if diff:
        lq1, lk1, lq2, lk2, subln = (r[...] for r in refs)
        lam = (jnp.exp(jnp.sum(lq1 * lk1, axis=-1, keepdims=True))
               - jnp.exp(jnp.sum(lq2 * lk2, axis=-1, keepdims=True)) + lambda_init)
    for c in range(n_col):
        rows = pl.ds(c * tq, tq)
        if diff:
            dlt = outs[c] - lam * outs[n_col + c]
            o_ref[rows, :] = (_rms_g(dlt, subln) * (1.0 - lambda_init)).astype(o_ref.dtype)
        else:
            o_ref[rows, :] = outs[c].astype(o_ref.dtype)


def _attention(qt, kc, vct, kx, vxt, params, n_batch, ctx_len, tq, tk, lambda_init):
    n_heads = SEC // HEAD_W
    dq = qt.shape[0] // n_heads
    lq = qt.shape[1] // n_batch
    nq = lq // tq
    diff = params is not None
    in_specs = [pl.BlockSpec((dq, tq), lambda b, h, i: (h, b * nq + i)),
                pl.BlockSpec((ctx_len, dq), lambda b, h, i: (b, h)),
                pl.BlockSpec((HEAD_W, ctx_len), lambda b, h, i: (h, b))]
    args = [qt, kc, vct]
    n_keys = ctx_len
    if kx is not None:
        lx = kx.shape[0] // n_batch
        n_keys += lx
        in_specs += [pl.BlockSpec((lx, dq), lambda b, h, i: (b, h)),
                     pl.BlockSpec((HEAD_W, lx), lambda b, h, i: (h, b))]
        args += [kx, vxt]
    if diff:
        in_specs += [pl.BlockSpec(p.shape, lambda b, h, i: (0, 0)) for p in params]
        args += list(params)
    unit = min(tq, ATTN_UNIT)
    n_units = (2 if diff else 1) * (tq // unit)
    return pl.pallas_call(
        functools.partial(_attn_kernel, diff=diff, has_x_keys=kx is not None, tk=tk, lambda_init=lambda_init),
        grid=(n_batch, n_heads, nq),
        in_specs=in_specs,
        out_specs=pl.BlockSpec((tq, HEAD_W), lambda b, h, i: (b * nq + i, h)),
        out_shape=jax.ShapeDtypeStruct((qt.shape[1], n_heads * HEAD_W), bf16),
        scratch_shapes=[pltpu.VMEM((n_units, n_keys, unit), f32), pltpu.VMEM((n_units, n_keys, unit), bf16)],
        compiler_params=_cp("arbitrary", "arbitrary", "arbitrary"),
        name="attention",
    )(*args)


def _s5_kernel(uc_ref, ux_ref, lre_ref, lim_ref, lst_ref, btr_ref, bti_ref, ctr_ref, cti_ref,
               yc_ref, yx_ref, wb_re, wb_im, wc_re, wc_im, a_re, a_im, h_re, h_im, s_re, s_im):
    d = pl.program_id(0)
    c = pl.program_id(1)
    n_batch, t_len, width = uc_ref.shape
    n_state = lre_ref.shape[1]
    half = n_state // 2
    n_groups = width // S5_GROUP
    n_blk = half // LANES

    @pl.when(c == 0)
    def _():
        lre, lim = lre_ref[...], lim_ref[...]
        step = jnp.exp(lst_ref[...])
        mag = jnp.exp(lre * step)
        are = mag * jnp.cos(lim * step)
        aim = mag * jnp.sin(lim * step)
        den = lre * lre + lim * lim
        fre = ((are - 1.0) * lre + aim * lim) / den
        fim = (aim * lre - (are - 1.0) * lim) / den
        row = lax.broadcasted_iota(jnp.int32, (width, n_state), 0) // S5_GROUP
        col = lax.broadcasted_iota(jnp.int32, (width, n_state), 1) // S5_STATE
        blk = row == col
        tile = lambda r: jnp.broadcast_to(r[...][None], (n_groups, S5_GROUP, n_state)).reshape(width, n_state)
        btr, bti = tile(btr_ref), tile(bti_ref)
        wb_re[...] = jnp.where(blk, fre * btr - fim * bti, 0.0).astype(bf16)
        wb_im[...] = jnp.where(blk, fre * bti + fim * btr, 0.0).astype(bf16)
        wc_re[...] = jnp.where(blk, tile(ctr_ref), 0.0).astype(bf16)
        wc_im[...] = jnp.where(blk, -tile(cti_ref), 0.0).astype(bf16)
        for hf in range(2):
            rows = slice(hf * n_batch, (hf + 1) * n_batch)
            for j in range(n_blk):
                cols = slice(hf * half + j * LANES, hf * half + (j + 1) * LANES)
                a_re[j, rows, :] = jnp.broadcast_to(are[:, cols], (n_batch, LANES))
                a_im[j, rows, :] = jnp.broadcast_to(aim[:, cols], (n_batch, LANES))
        h_re[...] = jnp.zeros_like(h_re)
        h_im[...] = jnp.zeros_like(h_im)

    pitch = s_re.shape[1] // (2 * n_batch)
    kb = LANES
    sb = kb // S5_GROUP * S5_STATE
    nt = (((1,), (1,)), ((), ()))

    def slab(col0, b):
        hf, rem = divmod(col0, half)
        return rem // LANES, pl.ds((hf * n_batch + b) * pitch, t_len)

    def run(u_ref, y_ref):
        for b in range(n_batch):
            u = u_ref[b]
            for bb in range(width // kb):
                ub = u[:, bb * kb:(bb + 1) * kb]
                bre = jnp.dot(ub, wb_re[bb * kb:(bb + 1) * kb, bb * sb:(bb + 1) * sb], preferred_element_type=f32)
                bim = jnp.dot(ub, wb_im[bb * kb:(bb + 1) * kb, bb * sb:(bb + 1) * sb], preferred_element_type=f32)
                for jj in range(sb // LANES):
                    j, rows = slab(bb * sb + jj * LANES, b)
                    s_re[j, rows, :] = bre[:, jj * LANES:(jj + 1) * LANES]
                    s_im[j, rows, :] = bim[:, jj * LANES:(jj + 1) * LANES]

        ar = [a_re[j] for j in range(n_blk)]
        ai = [a_im[j] for j in range(n_blk)]

        def step(k, carry):
            hr, hi = carry
            t = k + d * (t_len - 1 - 2 * k)
            rows = pl.ds(t, 2 * n_batch, stride=pitch)
            nr, ni = [], []
            for j in range(n_blk):
                nr.append(ar[j] * hr[j] - ai[j] * hi[j] + s_re[j, rows, :])
                ni.append(ar[j] * hi[j] + ai[j] * hr[j] + s_im[j, rows, :])
                s_re[j, rows, :] = nr[j]
                s_im[j, rows, :] = ni[j]
            return nr, ni

        init = ([h_re[j] for j in range(n_blk)], [h_im[j] for j in range(n_blk)])
        hr, hi = lax.fori_loop(0, t_len, step, init)
        for j in range(n_blk):
            h_re[j] = hr[j]
            h_im[j] = hi[j]

        for b in range(n_batch):
            ys = []
            for bb in range(width // kb):
                slabs = [slab(bb * sb + jj * LANES, b) for jj in range(sb // LANES)]
                sr = jnp.concatenate([s_re[j, rows, :] for j, rows in slabs], axis=1).astype(bf16)
                si = jnp.concatenate([s_im[j, rows, :] for j, rows in slabs], axis=1).astype(bf16)
                ys.append(
                    lax.dot_general(sr, wc_re[bb * kb:(bb + 1) * kb, bb * sb:(bb + 1) * sb], nt,
                                    preferred_element_type=f32)
                    + lax.dot_general(si, wc_im[bb * kb:(bb + 1) * kb, bb * sb:(bb + 1) * sb], nt,
                                      preferred_element_type=f32))
            y_ref[b] = jnp.concatenate(ys, axis=1)

    @pl.when(c == 0)
    def _():
        run(uc_ref, yc_ref)

    @pl.when(c > 0)
    def _():
        run(ux_ref, yx_ref)


def _s5_scan(px_c, px_x, lam_re, lam_im, log_step, bt_re, bt_im, ct_re, ct_im, n_batch):
    t_len = px_c.shape[1]
    seq = px_x.shape[1]
    assert seq % t_len == 0
    nch = seq // t_len
    n_state = lam_re.shape[-1]
    half = n_state // 2
    pitch = t_len + SUBLANES
    assert t_len % (2 * SUBLANES) == 0

    def x_chunk(d, c):
        return jnp.where(d == 0, jnp.maximum(c - 1, 0), jnp.minimum(nch - c, nch - 1))

    vec = pl.BlockSpec((None, 1, n_state), lambda d, c: (d, 0, 0))
    mat = pl.BlockSpec((None, S5_GROUP, n_state), lambda d, c: (d, 0, 0))
    return pl.pallas_call(
        _s5_kernel,
        grid=(2, nch + 1),
        in_specs=[pl.BlockSpec((n_batch, t_len, SEC), lambda d, c: (0, 0, 3)),
                  pl.BlockSpec((n_batch, t_len, SEC), lambda d, c: (0, x_chunk(d, c), 3)),
                  vec, vec, vec, mat, mat, mat, mat],
        out_specs=[pl.BlockSpec((None, n_batch, t_len, SEC), lambda d, c: (d, 0, 0, 0)),
                   pl.BlockSpec((None, n_batch, t_len, SEC), lambda d, c: (d, 0, x_chunk(d, c), 0))],
        out_shape=[jax.ShapeDtypeStruct((2, n_batch, t_len, SEC), f32),
                   jax.ShapeDtypeStruct((2, n_batch, seq, SEC), f32)],
        scratch_shapes=[pltpu.VMEM((SEC, n_state), bf16)] * 4
        + [pltpu.VMEM((half // LANES, 2 * n_batch, LANES), f32)] * 4
        + [pltpu.VMEM((half // LANES, 2 * n_batch * pitch, LANES), f32)] * 2,
        compiler_params=_cp("arbitrary", "arbitrary"),
        name="s5_scan",
    )(px_c, px_x, lam_re, lam_im, log_step, bt_re, bt_im, ct_re, ct_im)


def _s5_glu_kernel(yf_ref, yb_ref, u_ref, d_ref, w_ref, o_ref):
    y = d_ref[...] * u_ref[...].astype(f32) + yf_ref[...] + yb_ref[...]
    g = jax.nn.gelu(y)
    z = jnp.dot(g.astype(bf16), w_ref[...], preferred_element_type=f32)
    o_ref[...] = (g * jax.nn.sigmoid(z)).astype(o_ref.dtype)


def _s5_glu(y, px, s5_d, w_glu, tm):
    rows = px.shape[0]
    return pl.pallas_call(
        _s5_glu_kernel,
        grid=(rows // tm,),
        in_specs=[pl.BlockSpec((None, tm, SEC), lambda i: (0, i, 0)),
                  pl.BlockSpec((None, tm, SEC), lambda i: (1, i, 0)),
                  pl.BlockSpec((tm, SEC), lambda i: (i, 3)),
                  pl.BlockSpec((1, SEC), lambda i: (0, 0)),
                  pl.BlockSpec((SEC, SEC), lambda i: (0, 0))],
        out_specs=pl.BlockSpec((tm, SEC), lambda i: (i, 0)),
        out_shape=jax.ShapeDtypeStruct((rows, SEC), bf16),
        compiler_params=_cp("arbitrary"),
        name="s5_glu",
    )(y, y, px, s5_d, w_glu)


CONV_HALO = 16


def _conv_kernel(vp_ref, vc_ref, vn_ref, gp_ref, gc_ref, gn_ref, w_ref, b_ref, lg_ref, lb_ref, o_ref, ext_ref):
    j = pl.program_id(1)
    tc = vc_ref.shape[0]
    glu = lambda v, g: v[...].astype(f32) * jax.nn.sigmoid(g[...].astype(f32))
    ext_ref[0, pl.ds(0, CONV_HALO), :] = jnp.where(j > 0, glu(vp_ref, gp_ref), 0.0)
    ext_ref[0, pl.ds(CONV_HALO, tc), :] = glu(vc_ref, gc_ref)
    ext_ref[0, pl.ds(CONV_HALO + tc, CONV_HALO), :] = jnp.where(j < pl.num_programs(1) - 1, glu(vn_ref, gn_ref), 0.0)
    n_keep = tc + 2 * CONV_HALO - SUBLANES
    for s in range(1, SUBLANES):
        ext_ref[s, pl.ds(0, n_keep), :] = ext_ref[0, pl.ds(s, n_keep), :]
    y = jnp.zeros((tc, vc_ref.shape[1]), f32)
    for k in range(CONV_K):
        off = CONV_HALO - CONV_K // 2 + k
        y += w_ref[pl.ds(k, 1), :] * ext_ref[off % SUBLANES, pl.ds(off - off % SUBLANES, tc), :]
    y = y + b_ref[...]
    mu = jnp.mean(y, axis=-1, keepdims=True)
    var = jnp.mean(jnp.square(y - mu), axis=-1, keepdims=True)
    z = (y - mu) * lax.rsqrt(var + NORM_EPS) * lg_ref[...] + lb_ref[...]
    o_ref[...] = jax.nn.silu(z).astype(o_ref.dtype)


def _conv(px, conv_w, conv_b, ln_g, ln_b, n_batch, tc):
    rows = px.shape[0]
    seq = rows // n_batch
    nt = seq // tc
    hpt = tc // CONV_HALO
    cur = lambda s: pl.BlockSpec((tc, SEC), lambda b, j: (b * nt + j, s))
    prev = lambda s: pl.BlockSpec((CONV_HALO, SEC), lambda b, j: (jnp.maximum((b * nt + j) * hpt - 1, 0), s))
    nxt = lambda s: pl.BlockSpec(
        (CONV_HALO, SEC), lambda b, j: (jnp.minimum((b * nt + j + 1) * hpt, rows // CONV_HALO - 1), s))
    full = lambda a: pl.BlockSpec(a.shape, lambda b, j: (0,) * a.ndim)
    return pl.pallas_call(
        _conv_kernel,
        grid=(n_batch, nt),
        in_specs=[prev(6), cur(6), nxt(6), prev(7), cur(7), nxt(7),
                  full(conv_w), full(conv_b), full(ln_g), full(ln_b)],
        out_specs=pl.BlockSpec((tc, SEC), lambda b, j: (b * nt + j, 0)),
        out_shape=jax.ShapeDtypeStruct((rows, SEC), bf16),
        scratch_shapes=[pltpu.VMEM((SUBLANES, tc + 2 * CONV_HALO, SEC), f32)],
        compiler_params=_cp("arbitrary", "arbitrary"),
        name="conv",
    )(px, px, px, px, px, px, conv_w, conv_b, ln_g, ln_b)


def kernel(x, c, ctx, c_ctx, w_ada, b_ada, norm_pre_mix, norm_post_mix, norm_pre_ffn, norm_post_ffn, w_in, w_out, da_lam_q1, da_lam_k1, da_lam_q2, da_lam_k2, da_subln, s5_lam_re, s5_lam_im, s5_log_step, s5_b_re, s5_b_im, s5_c_re, s5_c_im, s5_d, s5_w_glu, mla_q_norm, mla_kv_norm, mla_w_uq, mla_w_ukv, conv_w, conv_b, conv_ln_g, conv_ln_b, w_ffn_in, w_ffn_out):
    n_batch, seq, d = x.shape
    ctx_len = ctx.shape[1]
    depth = w_in.shape[0]
    n_state = s5_lam_re.shape[2] * s5_lam_re.shape[3]
    mla_end = 4 * SEC + SEC + MLA_KV_RANK + ROT_DIM
    assert d == 4 * SEC and w_in.shape[2] == mla_end + 2 * SEC

    da_scale = ROT_DIM ** -0.5 * math.log2(math.e)
    mla_scale = (HEAD_W + ROT_DIM) ** -0.5 * math.log2(math.e)

    tm_x = _row_tile(seq, 1024)
    tm_c = _row_tile(n_batch * ctx_len, 1024)
    tpb = seq // tm_x
    grp_x = lambda i: i // tpb
    grp_c = lambda i: n_batch
    tq_da = _row_tile(seq, 512)
    tq_mla = _row_tile(seq, 1024)
    tk = _row_tile(seq, 512)

    n_grp = -(-(n_batch + 1) // SUBLANES) * SUBLANES
    cc = jnp.concatenate([c, c_ctx[None], jnp.zeros((n_grp - n_batch - 1, d), f32)], axis=0)
    mods = _ada(cc, w_ada, b_ada).reshape(depth, n_grp, 1, 6 * d)

    cos_x, sin_x = _rope_tables(seq)
    cos_c, sin_c = jnp.ones((tm_c, LANES), f32), jnp.zeros((tm_c, LANES), f32)

    w_in_p = jnp.concatenate([w_in[:, :, :mla_end], jnp.zeros((depth, d, N_SEC * SEC - w_in.shape[2]), f32),
                              w_in[:, :, mla_end:]], axis=2).astype(bf16)
    w_out_b = w_out.astype(bf16)
    w_ffn_in_b = w_ffn_in.astype(bf16)
    w_ffn_out_b = w_ffn_out.astype(bf16)
    w_glu_b = s5_w_glu.astype(bf16)
    n_mla = SEC // HEAD_W
    wq = mla_w_uq.reshape(depth, SEC, n_mla, HEAD_W + ROT_DIM)
    wq = jnp.pad(wq, ((0, 0), (0, 0), (0, 0), (0, MLA_QK - HEAD_W - ROT_DIM))).reshape(depth, SEC, n_mla * MLA_QK)
    wq = wq.astype(bf16)
    wkv = mla_w_ukv.reshape(depth, MLA_KV_RANK, n_mla, 2, HEAD_W).transpose(0, 1, 3, 2, 4)
    wkv = wkv.reshape(depth, MLA_KV_RANK, 2 * n_mla * HEAD_W).astype(bf16)

    flat = lambda a: a.reshape(depth, 2, 1, n_state)
    s5_lre, s5_lim = flat(s5_lam_re), flat(s5_lam_im)
    s5_lst = flat(jnp.broadcast_to(s5_log_step[..., None], s5_lam_re.shape))
    s5_btr = s5_b_re.transpose(0, 1, 4, 2, 3).reshape(depth, 2, S5_GROUP, n_state)
    s5_bti = s5_b_im.transpose(0, 1, 4, 2, 3).reshape(depth, 2, S5_GROUP, n_state)
    s5_ctr = s5_c_re.transpose(0, 1, 3, 2, 4).reshape(depth, 2, S5_GROUP, n_state)
    s5_cti = s5_c_im.transpose(0, 1, 3, 2, 4).reshape(depth, 2, S5_GROUP, n_state)

    row = lambda a, l: a[l][None]
    conv_w_p = jnp.pad(conv_w, ((0, 0), (0, 1), (0, 0)))

    xs = x.reshape(n_batch * seq, d)
    cs = ctx.reshape(n_batch * ctx_len, d)
    for l in range(depth):
        last = l == depth - 1
        lambda_init = 0.8 - 0.6 * math.exp(-0.3 * l)
        g_pre = row(norm_pre_mix, l)

        px = _in_proj(xs, g_pre, mods, l, grp_x, w_in_p, tm_x, 2 * SEC)
        pc = _in_proj(cs, g_pre, mods, l, grp_c, w_in_p, tm_c, 2 * SEC)

        qx, kx, vx = _da_prep(px, cos_x, sin_x, tpb, tm_x, da_scale)
        qc, kc, vc = _da_prep(pc, cos_c, sin_c, 1, tm_c, da_scale)
        da_params = (row(da_lam_q1, l), row(da_lam_k1, l), row(da_lam_q2, l), row(da_lam_k2, l), row(da_subln, l))
        y_da_x = _attention(qx, kc, vc, kx, vx, da_params, n_batch, ctx_len, tq_da, tk, lambda_init)

        mla_w = (row(mla_q_norm, l), row(mla_kv_norm, l), wq[l], wkv[l])
        mqx, mkx, mvx = _mla_prep(px, *mla_w, cos_x, sin_x, tpb, tm_x, mla_scale)
        mqc, mkc, mvc = _mla_prep(pc, *mla_w, cos_c, sin_c, 1, tm_c, mla_scale)
        y_mla_x = _attention(mqx, mkc, mvc, mkx, mvx, None, n_batch, ctx_len, tq_mla, tk, lambda_init)

        ys_c, ys_x = _s5_scan(pc.reshape(n_batch, ctx_len, -1), px.reshape(n_batch, seq, -1),
                              s5_lre[l], s5_lim[l], s5_lst[l], s5_btr[l], s5_bti[l], s5_ctr[l], s5_cti[l], n_batch)
        y_s5_x = _s5_glu(ys_x.reshape(2, n_batch * seq, SEC), px, row(s5_d, l), w_glu_b[l], tm_x)

        cv = (conv_w_p[l], row(conv_b, l), row(conv_ln_g, l), row(conv_ln_b, l))
        y_cv_x = _conv(px, *cv, n_batch, _row_tile(seq, 512))

        g_post, g_pre_f, g_post_f = row(norm_post_mix, l), row(norm_pre_ffn, l), row(norm_post_ffn, l)
        tm_o = _row_tile(tm_x, 512)
        x1 = _mix_out((y_da_x, y_s5_x, y_mla_x, y_cv_x), w_out_b, xs, g_post, mods, l,
                      lambda i: i // (seq // tm_o), tm_o)
        xs = _ffn(x1, g_pre_f, g_post_f, mods, l, lambda i: i // (seq // tm_o), w_ffn_in_b, w_ffn_out_b, tm_o, SEC)

        if not last:
            y_da_c = _attention(qc, kc, vc, None, None, da_params, n_batch, ctx_len, ctx_len, tk, lambda_init)
            y_mla_c = _attention(mqc, mkc, mvc, None, None, None, n_batch, ctx_len, ctx_len, tk, lambda_init)
            y_s5_c = _s5_glu(ys_c.reshape(2, n_batch * ctx_len, SEC), pc, row(s5_d, l), w_glu_b[l], tm_c)
            y_cv_c = _conv(pc, *cv, n_batch, ctx_len)
            tm_oc = _row_tile(tm_c, 512)
            c1 = _mix_out((y_da_c, y_s5_c, y_mla_c, y_cv_c), w_out_b, cs, g_post, mods, l, grp_c, tm_oc)
            cs = _ffn(c1, g_pre_f, g_post_f, mods, l, grp_c, w_ffn_in_b, w_ffn_out_b, tm_oc, SEC)
    return xs.reshape(n_batch, seq, d)
```

```python
import functools
import math

import jax
import jax.numpy as jnp
from jax import lax
from jax.experimental import pallas as pl
from jax.experimental.pallas import tpu as pltpu

f32 = jnp.float32
bf16 = jnp.bfloat16

NORM_EPS = 1e-6
ROPE_BASE = 10000.0
GRID_W = 64
ROT_DIM = 64
CONV_K = 31
S5_GROUP = 16
S5_STATE = 64

LANES = 128
SUBLANES = 8
SEC = 512
N_SEC = 8
HEAD_W = 128
MLA_QK = 256
MLA_KV_RANK = 256
ATTN_UNIT = 256
ACC_ROWS = SUBLANES
V7X_VMEM_BYTES = 64 * 1024 * 1024
VMEM_LIMIT = V7X_VMEM_BYTES * 7 // 8


def _cp(*sem):
    return pltpu.CompilerParams(dimension_semantics=sem, vmem_limit_bytes=VMEM_LIMIT)


def _row_tile(rows, want):
    t = min(rows, want)
    assert rows % t == 0, (rows, want)
    return t


def _ada_kernel(cc_ref, w_ref, b_ref, o_ref):
    s = jax.nn.silu(cc_ref[...]).astype(bf16)
    o_ref[...] = jnp.dot(s, w_ref[...].astype(bf16), preferred_element_type=f32) + b_ref[...]


def _ada(cc, w_ada, b_ada):
    depth, d, n = w_ada.shape
    g = cc.shape[0]
    tn = 512
    return pl.pallas_call(
        _ada_kernel,
        grid=(depth, n // tn),
        in_specs=[pl.BlockSpec((g, d), lambda l, j: (0, 0)),
                  pl.BlockSpec((None, d, tn), lambda l, j: (l, 0, j)),
                  pl.BlockSpec((None, 1, tn), lambda l, j: (l, 0, j))],
        out_specs=pl.BlockSpec((None, g, tn), lambda l, j: (l, 0, j)),
        out_shape=jax.ShapeDtypeStruct((depth, g, n), f32),
        compiler_params=_cp("arbitrary", "arbitrary"),
        name="ada",
    )(cc, w_ada, b_ada.reshape(depth, 1, n))


def _row_chunks(n_rows, chunk, body):
    def step(i, carry):
        body(pl.ds(pl.multiple_of(i * chunk, chunk), chunk))
        return carry

    lax.fori_loop(0, n_rows // chunk, step, 0, unroll=8)


def _norm_mod_rows(x_ref, g_ref, sh_ref, sc_ref, h_ref):
    g, shift, scale1 = g_ref[...], sh_ref[...], 1.0 + sc_ref[...]

    def body(rows):
        x = x_ref[rows, :]
        ms = jnp.mean(x * x, axis=-1, keepdims=True)
        h_ref[rows, :] = ((x * lax.rsqrt(ms + NORM_EPS) * g) * scale1 + shift).astype(bf16)

    _row_chunks(x_ref.shape[0], 2 * SUBLANES, body)


def _residual_rows(acc_ref, x_ref, g_ref, gate_ref, o_ref):
    g, gate = g_ref[...], gate_ref[...]

    def body(rows):
        acc = acc_ref[rows, :]
        ms = jnp.mean(acc * acc, axis=-1, keepdims=True)
        o_ref[rows, :] = x_ref[rows, :] + gate * (acc * lax.rsqrt(ms + NORM_EPS) * g)

    _row_chunks(x_ref.shape[0], SUBLANES, body)


def _in_proj_kernel(x_ref, g_ref, sh_ref, sc_ref, w_ref, o_ref, h_ref):
    @pl.when(pl.program_id(1) == 0)
    def _():
        _norm_mod_rows(x_ref, g_ref, sh_ref, sc_ref, h_ref)

    o_ref[...] = jnp.dot(h_ref[...], w_ref[...], preferred_element_type=f32).astype(o_ref.dtype)


def _ffn_kernel(x_ref, g_ref, sh_ref, sc_ref, wg_ref, wu_ref, wo_ref, g2_ref, gate_ref, o_ref, h_ref, acc_ref):
    j = pl.program_id(1)

    @pl.when(j == 0)
    def _():
        _norm_mod_rows(x_ref, g_ref, sh_ref, sc_ref, h_ref)
        acc_ref[...] = jnp.zeros_like(acc_ref)

    h = h_ref[...]
    a = jnp.dot(h, wg_ref[...], preferred_element_type=f32)
    b = jnp.dot(h, wu_ref[...], preferred_element_type=f32)
    act = (jax.nn.silu(a) * b).astype(bf16)
    acc_ref[...] += jnp.dot(act, wo_ref[...], preferred_element_type=f32)

    @pl.when(j == pl.num_programs(1) - 1)
    def _():
        _residual_rows(acc_ref, x_ref, g2_ref, gate_ref, o_ref)


def _mod_spec(l, grp, k, d):
    return pl.BlockSpec((None, None, 1, d), lambda i, j: (l, grp(i), 0, k))


def _in_proj(x, g, mods, l, grp, w, tm, tn):
    rows, d = x.shape
    n = w.shape[2]
    return pl.pallas_call(
        _in_proj_kernel,
        grid=(rows // tm, n // tn),
        in_specs=[pl.BlockSpec((tm, d), lambda i, j: (i, 0)),
                  pl.BlockSpec((1, d), lambda i, j: (0, 0)),
                  _mod_spec(l, grp, 0, d), _mod_spec(l, grp, 1, d),
                  pl.BlockSpec((None, d, tn), lambda i, j: (l, 0, j))],
        out_specs=pl.BlockSpec((tm, tn), lambda i, j: (i, j)),
        out_shape=jax.ShapeDtypeStruct((rows, n), bf16),
        scratch_shapes=[pltpu.VMEM((tm, d), bf16)],
        compiler_params=_cp("arbitrary", "arbitrary"),
        name="in_proj",
    )(x, g, mods, mods, w)


def _ffn(x, g_pre, g_post, mods, l, grp, w_in, w_out, tm, tn):
    rows, d = x.shape
    dff = w_out.shape[1]
    nj = dff // tn
    return pl.pallas_call(
        _ffn_kernel,
        grid=(rows // tm, nj),
        in_specs=[pl.BlockSpec((tm, d), lambda i, j: (i, 0)),
                  pl.BlockSpec((1, d), lambda i, j: (0, 0)),
                  _mod_spec(l, grp, 3, d), _mod_spec(l, grp, 4, d),
                  pl.BlockSpec((None, d, tn), lambda i, j: (l, 0, j)),
                  pl.BlockSpec((None, d, tn), lambda i, j: (l, 0, j + nj)),
                  pl.BlockSpec((None, tn, d), lambda i, j: (l, j, 0)),
                  pl.BlockSpec((1, d), lambda i, j: (0, 0)),
                  _mod_spec(l, grp, 5, d)],
        out_specs=pl.BlockSpec((tm, d), lambda i, j: (i, 0)),
        out_shape=jax.ShapeDtypeStruct((rows, d), f32),
        scratch_shapes=[pltpu.VMEM((tm, d), bf16), pltpu.VMEM((tm, d), f32)],
        compiler_params=_cp("arbitrary", "arbitrary"),
        name="ffn",
    )(x, g_pre, mods, mods, w_in, w_in, w_out, g_post, mods)


def _mix_out_kernel(y0_ref, y1_ref, y2_ref, y3_ref, w_ref, x_ref, g_ref, gate_ref, o_ref):
    tk = y0_ref.shape[1]
    acc = None
    for s, y_ref in enumerate((y0_ref, y1_ref, y2_ref, y3_ref)):
        part = jnp.dot(y_ref[...], w_ref[s * tk:(s + 1) * tk, :], preferred_element_type=f32)
        acc = part if acc is None else acc + part
    ms = jnp.mean(acc * acc, axis=-1, keepdims=True)
    o_ref[...] = x_ref[...] + gate_ref[...] * (acc * lax.rsqrt(ms + NORM_EPS) * g_ref[...])


def _mix_out(ys, w, x, g, mods, l, grp, tm):
    rows, d = x.shape
    tk = ys[0].shape[1]
    mod = _mod_spec(l, grp, 2, d)
    return pl.pallas_call(
        _mix_out_kernel,
        grid=(rows // tm,),
        in_specs=[pl.BlockSpec((tm, tk), lambda i: (i, 0))] * len(ys) + [
            pl.BlockSpec((None, len(ys) * tk, d), lambda i: (l, 0, 0)),
            pl.BlockSpec((tm, d), lambda i: (i, 0)),
            pl.BlockSpec((1, d), lambda i: (0, 0)),
            pl.BlockSpec(mod.block_shape, lambda i: mod.index_map(i, 0))],
        out_specs=pl.BlockSpec((tm, d), lambda i: (i, 0)),
        out_shape=jax.ShapeDtypeStruct((rows, d), f32),
        compiler_params=_cp("arbitrary"),
        name="mix_out",
    )(*ys, w, x, g, mods)


def _rope_tables(n_tokens):
    n_rows = n_tokens // GRID_W
    row = jnp.repeat(jnp.arange(n_rows, dtype=f32), GRID_W)
    col = jnp.tile(jnp.arange(GRID_W, dtype=f32), n_rows)
    n_freq = ROT_DIM // 4
    inv = ROPE_BASE ** (-jnp.arange(n_freq, dtype=f32) / n_freq)
    ang = jnp.concatenate([row[:, None] * inv, col[:, None] * inv], axis=-1)
    cos, sin = jnp.cos(ang), jnp.sin(ang)
    return jnp.tile(cos, (1, 4)), jnp.tile(jnp.concatenate([-sin, sin], axis=-1), (1, 2))


def _rope_slab(x, cos, sin):
    lane = lax.broadcasted_iota(jnp.int32, x.shape, 1)
    partner = jnp.where(lane % ROT_DIM < ROT_DIM // 2,
                        pltpu.roll(x, LANES - ROT_DIM // 2, 1), pltpu.roll(x, ROT_DIM // 2, 1))
    return x * cos + partner * sin


def _da_prep_kernel(q_ref, k_ref, v_ref, cos_ref, sin_ref, qo_ref, ko_ref, vo_ref, *, scale):
    cos, sin = cos_ref[...], sin_ref[...]
    for h in range(SEC // LANES):
        sl = slice(h * LANES, (h + 1) * LANES)
        qo_ref[sl, :] = (_rope_slab(q_ref[:, sl].astype(f32), cos, sin) * scale).T.astype(bf16)
        ko_ref[:, sl] = _rope_slab(k_ref[:, sl].astype(f32), cos, sin).astype(bf16)
        vo_ref[sl, :] = v_ref[:, sl].astype(f32).T.astype(bf16)


def _da_prep(px, cos, sin, pos_tiles, tm, scale):
    rows = px.shape[0]
    sec = lambda s: pl.BlockSpec((tm, SEC), lambda i: (i, s))
    tab = pl.BlockSpec((tm, LANES), lambda i: (i % pos_tiles, 0))
    nat = pl.BlockSpec((tm, SEC), lambda i: (i, 0))
    tr = pl.BlockSpec((SEC, tm), lambda i: (0, i))
    return pl.pallas_call(
        functools.partial(_da_prep_kernel, scale=scale),
        grid=(rows // tm,),
        in_specs=[sec(0), sec(1), sec(2), tab, tab],
        out_specs=[tr, nat, tr],
        out_shape=[jax.ShapeDtypeStruct((SEC, rows), bf16), jax.ShapeDtypeStruct((rows, SEC), bf16),
                   jax.ShapeDtypeStruct((SEC, rows), bf16)],
        compiler_params=_cp("arbitrary"),
        name="da_prep",
    )(px, px, px, cos, sin)


def _rms_g(x, g):
    ms = jnp.mean(x * x, axis=-1, keepdims=True)
    return x * lax.rsqrt(ms + NORM_EPS) * g


def _mla_prep_kernel(cq_ref, ckv_ref, gq_ref, gkv_ref, wq_ref, wkv_ref, cos_ref, sin_ref,
                     qo_ref, ko_ref, vo_ref, *, scale):
    cos, sin = cos_ref[...], sin_ref[...]
    n_heads = SEC // HEAD_W
    q = jnp.dot(_rms_g(cq_ref[...].astype(f32), gq_ref[...]).astype(bf16), wq_ref[...],
                preferred_element_type=f32)
    kv = jnp.dot(_rms_g(ckv_ref[:, :MLA_KV_RANK].astype(f32), gkv_ref[...]).astype(bf16), wkv_ref[...],
                 preferred_element_type=f32)
    kr = _rope_slab(ckv_ref[:, MLA_KV_RANK:MLA_KV_RANK + LANES].astype(f32), cos, sin).astype(bf16)
    for h in range(n_heads):
        base = h * MLA_QK
        qo_ref[base:base + LANES, :] = (q[:, base:base + LANES] * scale).T.astype(bf16)
        qo_ref[base + LANES:base + MLA_QK, :] = (
            _rope_slab(q[:, base + LANES:base + MLA_QK], cos, sin) * scale).T.astype(bf16)
        ko_ref[:, base:base + LANES] = kv[:, h * LANES:(h + 1) * LANES].astype(bf16)
        ko_ref[:, base + LANES:base + MLA_QK] = kr
        vo_ref[h * LANES:(h + 1) * LANES, :] = kv[:, (n_heads + h) * LANES:(n_heads + h + 1) * LANES].T.astype(bf16)


def _mla_prep(px, gq, gkv, wq, wkv, cos, sin, pos_tiles, tm, scale):
    rows = px.shape[0]
    n_heads = SEC // HEAD_W
    sec = lambda s: pl.BlockSpec((tm, SEC), lambda i: (i, s))
    tab = pl.BlockSpec((tm, LANES), lambda i: (i % pos_tiles, 0))
    full = lambda a: pl.BlockSpec(a.shape, lambda i: (0,) * a.ndim)
    return pl.pallas_call(
        functools.partial(_mla_prep_kernel, scale=scale),
        grid=(rows // tm,),
        in_specs=[sec(4), sec(5), full(gq), full(gkv), full(wq), full(wkv), tab, tab],
        out_specs=[pl.BlockSpec((n_heads * MLA_QK, tm), lambda i: (0, i)),
                   pl.BlockSpec((tm, n_heads * MLA_QK), lambda i: (i, 0)),
                   pl.BlockSpec((SEC, tm), lambda i: (0, i))],
        out_shape=[jax.ShapeDtypeStruct((n_heads * MLA_QK, rows), bf16),
                   jax.ShapeDtypeStruct((rows, n_heads * MLA_QK), bf16),
                   jax.ShapeDtypeStruct((SEC, rows), bf16)],
        compiler_params=_cp("arbitrary"),
        name="mla_prep",
    )(px, px, gq, gkv, wq, wkv, cos, sin)


def _attn_kernel(*refs, diff, has_x_keys, tk, lambda_init):
    refs = list(refs)
    p_ref = refs.pop()
    s_ref = refs.pop()
    o_ref = refs.pop()
    q_ref, kc_ref, vc_ref = refs[:3]
    refs = refs[3:]
    n_ctx = kc_ref.shape[0]
    chunks = [(kc_ref, 0, 0, n_ctx)]
    if has_x_keys:
        kx_ref, vx_ref = refs[:2]
        refs = refs[2:]
        chunks += [(kx_ref, c * tk, n_ctx + c * tk, tk) for c in range(kx_ref.shape[0] // tk)]
    qt = q_ref[...]
    if diff:
        feat = lax.broadcasted_iota(jnp.int32, qt.shape, 0)
        zero = jnp.zeros_like(qt)
        maps = [jnp.where(feat < ROT_DIM, qt, zero), jnp.where(feat >= ROT_DIM, qt, zero)]
    else:
        maps = [qt]
    tq = s_ref.shape[2]
    n_col = qt.shape[1] // tq
    qs = [qm[:, c * tq:(c + 1) * tq] for qm in maps for c in range(n_col)]
    fold = lambda a: a.reshape(a.shape[0] // ACC_ROWS, ACC_ROWS, a.shape[1])

    m_acc = [jnp.full((ACC_ROWS, tq), -jnp.inf, f32) for _ in qs]
    for k_ref, start, row0, size in chunks:
        k = k_ref[pl.ds(start, size), :]
        for mi, qm in enumerate(qs):
            s = jnp.dot(k, qm, preferred_element_type=f32)
            s_ref[mi, pl.ds(row0, size), :] = s
            m_acc[mi] = jnp.maximum(m_acc[mi], jnp.max(fold(s), axis=0))
    m_col = [jnp.max(m, axis=0, keepdims=True) for m in m_acc]

    l_acc = [jnp.zeros((ACC_ROWS, tq), f32) for _ in qs]
    for _, _, row0, size in chunks:
        for mi in range(len(qs)):
            p = jnp.exp2(s_ref[mi, pl.ds(row0, size), :] - m_col[mi])
            l_acc[mi] = l_acc[mi] + jnp.sum(fold(p), axis=0)
            p_ref[mi, pl.ds(row0, size), :] = p.astype(bf16)

    outs = []
    for mi in range(len(qs)):
        ot = jnp.dot(vc_ref[...], p_ref[mi, pl.ds(0, n_ctx), :], preferred_element_type=f32)
        if has_x_keys:
            ot = ot + jnp.dot(vx_ref[...], p_ref[mi, pl.ds(n_ctx, vx_ref.shape[1]), :], preferred_element_type=f32)
        outs.append((ot / jnp.sum(l_acc[mi], axis=0, keepdims=True)).T)
    if diff:
        lq1, lk1, lq2, lk2, subln = (r[...] for r in refs)
        lam = (jnp.exp(jnp.sum(lq1 * lk1, axis=-1, keepdims=True))
               - jnp.exp(jnp.sum(lq2 * lk2, axis=-1, keepdims=True)) + lambda_init)
    for c in range(n_col):
        rows = pl.ds(c * tq, tq)
        if diff:
            dlt = outs[c] - lam * outs[n_col + c]
            o_ref[rows, :] = (_rms_g(dlt, subln) * (1.0 - lambda_init)).astype(o_ref.dtype)
        else:
            o_ref[rows, :] = outs[c].astype(o_ref.dtype)


def _attention(qt, kc, vct, kx, vxt, params, n_batch, ctx_len, tq, tk, lambda_init):
    n_heads = SEC // HEAD_W
    dq = qt.shape[0] // n_heads
    lq = qt.shape[1] // n_batch
    nq = lq // tq
    diff = params is not None
    in_specs = [pl.BlockSpec((dq, tq), lambda b, h, i: (h, b * nq + i)),
                pl.BlockSpec((ctx_len, dq), lambda b, h, i: (b, h)),
                pl.BlockSpec((HEAD_W, ctx_len), lambda b, h, i: (h, b))]
    args = [qt, kc, vct]
    n_keys = ctx_len
    if kx is not None:
        lx = kx.shape[0] // n_batch
        n_keys += lx
        in_specs += [pl.BlockSpec((lx, dq), lambda b, h, i: (b, h)),
                     pl.BlockSpec((HEAD_W, lx), lambda b, h, i: (h, b))]
        args += [kx, vxt]
    if diff:
        in_specs += [pl.BlockSpec(p.shape, lambda b, h, i: (0, 0)) for p in params]
        args += list(params)
    unit = min(tq, ATTN_UNIT)
    n_units = (2 if diff else 1) * (tq // unit)
    return pl.pallas_call(
        functools.partial(_attn_kernel, diff=diff, has_x_keys=kx is not None, tk=tk, lambda_init=lambda_init),
        grid=(n_batch, n_heads, nq),
        in_specs=in_specs,
        out_specs=pl.BlockSpec((tq, HEAD_W), lambda b, h, i: (b * nq + i, h)),
        out_shape=jax.ShapeDtypeStruct((qt.shape[1], n_heads * HEAD_W), bf16),
        scratch_shapes=[pltpu.VMEM((n_units, n_keys, unit), f32), pltpu.VMEM((n_units, n_keys, unit), bf16)],
        compiler_params=_cp("arbitrary", "arbitrary", "arbitrary"),
        name="attention",
    )(*args)


def _s5_kernel(uc_ref, ux_ref, lre_ref, lim_ref, lst_ref, btr_ref, bti_ref, ctr_ref, cti_ref,
               yc_ref, yx_ref, wb_re, wb_im, wc_re, wc_im, a_re, a_im, h_re, h_im, s_re, s_im):
    d = pl.program_id(0)
    c = pl.program_id(1)
    n_batch, t_len, width = uc_ref.shape
    n_state = lre_ref.shape[1]
    half = n_state // 2
    n_groups = width // S5_GROUP
    n_blk = half // LANES

    @pl.when(c == 0)
    def _():
        lre, lim = lre_ref[...], lim_ref[...]
        step = jnp.exp(lst_ref[...])
        mag = jnp.exp(lre * step)
        are = mag * jnp.cos(lim * step)
        aim = mag * jnp.sin(lim * step)
        den = lre * lre + lim * lim
        fre = ((are - 1.0) * lre + aim * lim) / den
        fim = (aim * lre - (are - 1.0) * lim) / den
        row = lax.broadcasted_iota(jnp.int32, (width, n_state), 0) // S5_GROUP
        col = lax.broadcasted_iota(jnp.int32, (width, n_state), 1) // S5_STATE
        blk = row == col
        tile = lambda r: jnp.broadcast_to(r[...][None], (n_groups, S5_GROUP, n_state)).reshape(width, n_state)
        btr, bti = tile(btr_ref), tile(bti_ref)
        wb_re[...] = jnp.where(blk, fre * btr - fim * bti, 0.0).astype(bf16)
        wb_im[...] = jnp.where(blk, fre * bti + fim * btr, 0.0).astype(bf16)
        wc_re[...] = jnp.where(blk, tile(ctr_ref), 0.0).astype(bf16)
        wc_im[...] = jnp.where(blk, -tile(cti_ref), 0.0).astype(bf16)
        for hf in range(2):
            rows = slice(hf * n_batch, (hf + 1) * n_batch)
            for j in range(n_blk):
                cols = slice(hf * half + j * LANES, hf * half + (j + 1) * LANES)
                a_re[j, rows, :] = jnp.broadcast_to(are[:, cols], (n_batch, LANES))
                a_im[j, rows, :] = jnp.broadcast_to(aim[:, cols], (n_batch, LANES))
        h_re[...] = jnp.zeros_like(h_re)
        h_im[...] = jnp.zeros_like(h_im)

    pitch = s_re.shape[1] // (2 * n_batch)
    kb = LANES
    sb = kb // S5_GROUP * S5_STATE
    nt = (((1,), (1,)), ((), ()))

    def slab(col0, b):
        hf, rem = divmod(col0, half)
        return rem // LANES, pl.ds((hf * n_batch + b) * pitch, t_len)

    def run(u_ref, y_ref):
        for b in range(n_batch):
            u = u_ref[b]
            for bb in range(width // kb):
                ub = u[:, bb * kb:(bb + 1) * kb]
                bre = jnp.dot(ub, wb_re[bb * kb:(bb + 1) * kb, bb * sb:(bb + 1) * sb], preferred_element_type=f32)
                bim = jnp.dot(ub, wb_im[bb * kb:(bb + 1) * kb, bb * sb:(bb + 1) * sb], preferred_element_type=f32)
                for jj in range(sb // LANES):
                    j, rows = slab(bb * sb + jj * LANES, b)
                    s_re[j, rows, :] = bre[:, jj * LANES:(jj + 1) * LANES]
                    s_im[j, rows, :] = bim[:, jj * LANES:(jj + 1) * LANES]

        ar = [a_re[j] for j in range(n_blk)]
        ai = [a_im[j] for j in range(n_blk)]

        def step(k, carry):
            hr, hi = carry
            t = k + d * (t_len - 1 - 2 * k)
            rows = pl.ds(t, 2 * n_batch, stride=pitch)
            nr, ni = [], []
            for j in range(n_blk):
                nr.append(ar[j] * hr[j] - ai[j] * hi[j] + s_re[j, rows, :])
                ni.append(ar[j] * hi[j] + ai[j] * hr[j] + s_im[j, rows, :])
                s_re[j, rows, :] = nr[j]
                s_im[j, rows, :] = ni[j]
            return nr, ni

        init = ([h_re[j] for j in range(n_blk)], [h_im[j] for j in range(n_blk)])
        hr, hi = lax.fori_loop(0, t_len, step, init)
        for j in range(n_blk):
            h_re[j] = hr[j]
            h_im[j] = hi[j]

        for b in range(n_batch):
            ys = []
            for bb in range(width // kb):
                slabs = [slab(bb * sb + jj * LANES, b) for jj in range(sb // LANES)]
                sr = jnp.concatenate([s_re[j, rows, :] for j, rows in slabs], axis=1).astype(bf16)
                si = jnp.concatenate([s_im[j, rows, :] for j, rows in slabs], axis=1).astype(bf16)
                ys.append(
                    lax.dot_general(sr, wc_re[bb * kb:(bb + 1) * kb, bb * sb:(bb + 1) * sb], nt,
                                    preferred_element_type=f32)
                    + lax.dot_general(si, wc_im[bb * kb:(bb + 1) * kb, bb * sb:(bb + 1) * sb], nt,
                                      preferred_element_type=f32))
            y_ref[b] = jnp.concatenate(ys, axis=1)

    @pl.when(c == 0)
    def _():
        run(uc_ref, yc_ref)

    @pl.when(c > 0)
    def _():
        run(ux_ref, yx_ref)


def _s5_scan(px_c, px_x, lam_re, lam_im, log_step, bt_re, bt_im, ct_re, ct_im, n_batch):
    t_len = px_c.shape[1]
    seq = px_x.shape[1]
    assert seq % t_len == 0
    nch = seq // t_len
    n_state = lam_re.shape[-1]
    half = n_state // 2
    pitch = t_len + SUBLANES
    assert t_len % (2 * SUBLANES) == 0

    def x_chunk(d, c):
        return jnp.where(d == 0, jnp.maximum(c - 1, 0), jnp.minimum(nch - c, nch - 1))

    vec = pl.BlockSpec((None, 1, n_state), lambda d, c: (d, 0, 0))
    mat = pl.BlockSpec((None, S5_GROUP, n_state), lambda d, c: (d, 0, 0))
    return pl.pallas_call(
        _s5_kernel,
        grid=(2, nch + 1),
        in_specs=[pl.BlockSpec((n_batch, t_len, SEC), lambda d, c: (0, 0, 3)),
                  pl.BlockSpec((n_batch, t_len, SEC), lambda d, c: (0, x_chunk(d, c), 3)),
                  vec, vec, vec, mat, mat, mat, mat],
        out_specs=[pl.BlockSpec((None, n_batch, t_len, SEC), lambda d, c: (d, 0, 0, 0)),
                   pl.BlockSpec((None, n_batch, t_len, SEC), lambda d, c: (d, 0, x_chunk(d, c), 0))],
        out_shape=[jax.ShapeDtypeStruct((2, n_batch, t_len, SEC), f32),
                   jax.ShapeDtypeStruct((2, n_batch, seq, SEC), f32)],
        scratch_shapes=[pltpu.VMEM((SEC, n_state), bf16)] * 4
        + [pltpu.VMEM((half // LANES, 2 * n_batch, LANES), f32)] * 4
        + [pltpu.VMEM((half // LANES, 2 * n_batch * pitch, LANES), f32)] * 2,
        compiler_params=_cp("arbitrary", "arbitrary"),
        name="s5_scan",
    )(px_c, px_x, lam_re, lam_im, log_step, bt_re, bt_im, ct_re, ct_im)


def _s5_glu_kernel(yf_ref, yb_ref, u_ref, d_ref, w_ref, o_ref):
    y = d_ref[...] * u_ref[...].astype(f32) + yf_ref[...] + yb_ref[...]
    g = jax.nn.gelu(y)
    z = jnp.dot(g.astype(bf16), w_ref[...], preferred_element_type=f32)
    o_ref[...] = (g * jax.nn.sigmoid(z)).astype(o_ref.dtype)


def _s5_glu(y, px, s5_d, w_glu, tm):
    rows = px.shape[0]
    return pl.pallas_call(
        _s5_glu_kernel,
        grid=(rows // tm,),
        in_specs=[pl.BlockSpec((None, tm, SEC), lambda i: (0, i, 0)),
                  pl.BlockSpec((None, tm, SEC), lambda i: (1, i, 0)),
                  pl.BlockSpec((tm, SEC), lambda i: (i, 3)),
                  pl.BlockSpec((1, SEC), lambda i: (0, 0)),
                  pl.BlockSpec((SEC, SEC), lambda i: (0, 0))],
        out_specs=pl.BlockSpec((tm, SEC), lambda i: (i, 0)),
        out_shape=jax.ShapeDtypeStruct((rows, SEC), bf16),
        compiler_params=_cp("arbitrary"),
        name="s5_glu",
    )(y, y, px, s5_d, w_glu)


CONV_HALO = 16


def _conv_kernel(vp_ref, vc_ref, vn_ref, gp_ref, gc_ref, gn_ref, w_ref, b_ref, lg_ref, lb_ref, o_ref, ext_ref):
    j = pl.program_id(1)
    tc = vc_ref.shape[0]
    glu = lambda v, g: v[...].astype(f32) * jax.nn.sigmoid(g[...].astype(f32))
    ext_ref[0, pl.ds(0, CONV_HALO), :] = jnp.where(j > 0, glu(vp_ref, gp_ref), 0.0)
    ext_ref[0, pl.ds(CONV_HALO, tc), :] = glu(vc_ref, gc_ref)
    ext_ref[0, pl.ds(CONV_HALO + tc, CONV_HALO), :] = jnp.where(j < pl.num_programs(1) - 1, glu(vn_ref, gn_ref), 0.0)
    n_keep = tc + 2 * CONV_HALO - SUBLANES
    for s in range(1, SUBLANES):
        ext_ref[s, pl.ds(0, n_keep), :] = ext_ref[0, pl.ds(s, n_keep), :]
    y = jnp.zeros((tc, vc_ref.shape[1]), f32)
    for k in range(CONV_K):
        off = CONV_HALO - CONV_K // 2 + k
        y += w_ref[pl.ds(k, 1), :] * ext_ref[off % SUBLANES, pl.ds(off - off % SUBLANES, tc), :]
    y = y + b_ref[...]
    mu = jnp.mean(y, axis=-1, keepdims=True)
    var = jnp.mean(jnp.square(y - mu), axis=-1, keepdims=True)
    z = (y - mu) * lax.rsqrt(var + NORM_EPS) * lg_ref[...] + lb_ref[...]
    o_ref[...] = jax.nn.silu(z).astype(o_ref.dtype)


def _conv(px, conv_w, conv_b, ln_g, ln_b, n_batch, tc):
    rows = px.shape[0]
    seq = rows // n_batch
    nt = seq // tc
    hpt = tc // CONV_HALO
    cur = lambda s: pl.BlockSpec((tc, SEC), lambda b, j: (b * nt + j, s))
    prev = lambda s: pl.BlockSpec((CONV_HALO, SEC), lambda b, j: (jnp.maximum((b * nt + j) * hpt - 1, 0), s))
    nxt = lambda s: pl.BlockSpec(
        (CONV_HALO, SEC), lambda b, j: (jnp.minimum((b * nt + j + 1) * hpt, rows // CONV_HALO - 1), s))
    full = lambda a: pl.BlockSpec(a.shape, lambda b, j: (0,) * a.ndim)
    return pl.pallas_call(
        _conv_kernel,
        grid=(n_batch, nt),
        in_specs=[prev(6), cur(6), nxt(6), prev(7), cur(7), nxt(7),
                  full(conv_w), full(conv_b), full(ln_g), full(ln_b)],
        out_specs=pl.BlockSpec((tc, SEC), lambda b, j: (b * nt + j, 0)),
        out_shape=jax.ShapeDtypeStruct((rows, SEC), bf16),
        scratch_shapes=[pltpu.VMEM((SUBLANES, tc + 2 * CONV_HALO, SEC), f32)],
        compiler_params=_cp("arbitrary", "arbitrary"),
        name="conv",
    )(px, px, px, px, px, px, conv_w, conv_b, ln_g, ln_b)


def kernel(x, c, ctx, c_ctx, w_ada, b_ada, norm_pre_mix, norm_post_mix, norm_pre_ffn, norm_post_ffn, w_in, w_out, da_lam_q1, da_lam_k1, da_lam_q2, da_lam_k2, da_subln, s5_lam_re, s5_lam_im, s5_log_step, s5_b_re, s5_b_im, s5_c_re, s5_c_im, s5_d, s5_w_glu, mla_q_norm, mla_kv_norm, mla_w_uq, mla_w_ukv, conv_w, conv_b, conv_ln_g, conv_ln_b, w_ffn_in, w_ffn_out):
    n_batch, seq, d = x.shape
    ctx_len = ctx.shape[1]
    depth = w_in.shape[0]
    n_state = s5_lam_re.shape[2] * s5_lam_re.shape[3]
    mla_end = 4 * SEC + SEC + MLA_KV_RANK + ROT_DIM
    assert d == 4 * SEC and w_in.shape[2] == mla_end + 2 * SEC

    da_scale = ROT_DIM ** -0.5 * math.log2(math.e)
    mla_scale = (HEAD_W + ROT_DIM) ** -0.5 * math.log2(math.e)

    tm_x = _row_tile(seq, 1024)
    tm_c = _row_tile(n_batch * ctx_len, 1024)
    tpb = seq // tm_x
    grp_x = lambda i: i // tpb
    grp_c = lambda i: n_batch
    tq_da = _row_tile(seq, 512)
    tq_mla = _row_tile(seq, 1024)
    tk = _row_tile(seq, 512)

    n_grp = -(-(n_batch + 1) // SUBLANES) * SUBLANES
    cc = jnp.concatenate([c, c_ctx[None], jnp.zeros((n_grp - n_batch - 1, d), f32)], axis=0)
    mods = _ada(cc, w_ada, b_ada).reshape(depth, n_grp, 1, 6 * d)

    cos_x, sin_x = _rope_tables(seq)
    cos_c, sin_c = jnp.ones((tm_c, LANES), f32), jnp.zeros((tm_c, LANES), f32)

    w_in_p = jnp.concatenate([w_in[:, :, :mla_end], jnp.zeros((depth, d, N_SEC * SEC - w_in.shape[2]), f32),
                              w_in[:, :, mla_end:]], axis=2).astype(bf16)
    w_out_b = w_out.astype(bf16)
    w_ffn_in_b = w_ffn_in.astype(bf16)
    w_ffn_out_b = w_ffn_out.astype(bf16)
    w_glu_b = s5_w_glu.astype(bf16)
    n_mla = SEC // HEAD_W
    wq = mla_w_uq.reshape(depth, SEC, n_mla, HEAD_W + ROT_DIM)
    wq = jnp.pad(wq, ((0, 0), (0, 0), (0, 0), (0, MLA_QK - HEAD_W - ROT_DIM))).reshape(depth, SEC, n_mla * MLA_QK)
    wq = wq.astype(bf16)
    wkv = mla_w_ukv.reshape(depth, MLA_KV_RANK, n_mla, 2, HEAD_W).transpose(0, 1, 3, 2, 4)
    wkv = wkv.reshape(depth, MLA_KV_RANK, 2 * n_mla * HEAD_W).astype(bf16)

    flat = lambda a: a.reshape(depth, 2, 1, n_state)
    s5_lre, s5_lim = flat(s5_lam_re), flat(s5_lam_im)
    s5_lst = flat(jnp.broadcast_to(s5_log_step[..., None], s5_lam_re.shape))
    s5_btr = s5_b_re.transpose(0, 1, 4, 2, 3).reshape(depth, 2, S5_GROUP, n_state)
    s5_bti = s5_b_im.transpose(0, 1, 4, 2, 3).reshape(depth, 2, S5_GROUP, n_state)
    s5_ctr = s5_c_re.transpose(0, 1, 3, 2, 4).reshape(depth, 2, S5_GROUP, n_state)
    s5_cti = s5_c_im.transpose(0, 1, 3, 2, 4).reshape(depth, 2, S5_GROUP, n_state)

    row = lambda a, l: a[l][None]
    conv_w_p = jnp.pad(conv_w, ((0, 0), (0, 1), (0, 0)))

    xs = x.reshape(n_batch * seq, d)
    cs = ctx.reshape(n_batch * ctx_len, d)
    for l in range(depth):
        last = l == depth - 1
        lambda_init = 0.8 - 0.6 * math.exp(-0.3 * l)
        g_pre = row(norm_pre_mix, l)

        px = _in_proj(xs, g_pre, mods, l, grp_x, w_in_p, tm_x, 2 * SEC)
        pc = _in_proj(cs, g_pre, mods, l, grp_c, w_in_p, tm_c, 2 * SEC)

        qx, kx, vx = _da_prep(px, cos_x, sin_x, tpb, tm_x, da_scale)
        qc, kc, vc = _da_prep(pc, cos_c, sin_c, 1, tm_c, da_scale)
        da_params = (row(da_lam_q1, l), row(da_lam_k1, l), row(da_lam_q2, l), row(da_lam_k2, l), row(da_subln, l))
        y_da_x = _attention(qx, kc, vc, kx, vx, da_params, n_batch, ctx_len, tq_da, tk, lambda_init)

        mla_w = (row(mla_q_norm, l), row(mla_kv_norm, l), wq[l], wkv[l])
        mqx, mkx, mvx = _mla_prep(px, *mla_w, cos_x, sin_x, tpb, tm_x, mla_scale)
        mqc, mkc, mvc = _mla_prep(pc, *mla_w, cos_c, sin_c, 1, tm_c, mla_scale)
        y_mla_x = _attention(mqx, mkc, mvc, mkx, mvx, None, n_batch, ctx_len, tq_mla, tk, lambda_init)

        ys_c, ys_x = _s5_scan(pc.reshape(n_batch, ctx_len, -1), px.reshape(n_batch, seq, -1),
                              s5_lre[l], s5_lim[l], s5_lst[l], s5_btr[l], s5_bti[l], s5_ctr[l], s5_cti[l], n_batch)
        y_s5_x = _s5_glu(ys_x.reshape(2, n_batch * seq, SEC), px, row(s5_d, l), w_glu_b[l], tm_x)

        cv = (conv_w_p[l], row(conv_b, l), row(conv_ln_g, l), row(conv_ln_b, l))
        y_cv_x = _conv(px, *cv, n_batch, _row_tile(seq, 512))

        g_post, g_pre_f, g_post_f = row(norm_post_mix, l), row(norm_pre_ffn, l), row(norm_post_ffn, l)
        tm_o = _row_tile(tm_x, 512)
        x1 = _mix_out((y_da_x, y_s5_x, y_mla_x, y_cv_x), w_out_b, xs, g_post, mods, l,
                      lambda i: i // (seq // tm_o), tm_o)
        xs = _ffn(x1, g_pre_f, g_post_f, mods, l, lambda i: i // (seq // tm_o), w_ffn_in_b, w_ffn_out_b, tm_o, SEC)

        if not last:
            y_da_c = _attention(qc, kc, vc, None, None, da_params, n_batch, ctx_len, ctx_len, tk, lambda_init)
            y_mla_c = _attention(mqc, mkc, mvc, None, None, None, n_batch, ctx_len, ctx_len, tk, lambda_init)
            y_s5_c = _s5_glu(ys_c.reshape(2, n_batch * ctx_len, SEC), pc, row(s5_d, l), w_glu_b[l], tm_c)
            y_cv_c = _conv(pc, *cv, n_batch, ctx_len)
            tm_oc = _row_tile(tm_c, 512)
            c1 = _mix_out((y_da_c, y_s5_c, y_mla_c, y_cv_c), w_out_b, cs, g_post, mods, l, grp_c, tm_oc)
            cs = _ffn(c1, g_pre_f, g_post_f, mods, l, grp_c, w_ffn_in_b, w_ffn_out_b, tm_oc, SEC)
    return xs.reshape(n_batch, seq, d)
```

```python
import functools
import math

import jax
import jax.numpy as jnp
from jax import lax
from jax.experimental import pallas as pl
from jax.experimental.pallas import tpu as pltpu

f32 = jnp.float32
bf16 = jnp.bfloat16

NORM_EPS = 1e-6
ROPE_BASE = 10000.0
GRID_W = 64
ROT_DIM = 64
CONV_K = 31
S5_GROUP = 16
S5_STATE = 64

LANES = 128
SUBLANES = 8
SEC = 512
N_SEC = 8
HEAD_W = 128
MLA_QK = 256
MLA_KV_RANK = 256
ATTN_UNIT = 256
ACC_ROWS = SUBLANES
V7X_VMEM_BYTES = 64 * 1024 * 1024
VMEM_LIMIT = V7X_VMEM_BYTES * 7 // 8


def _cp(*sem):
    return pltpu.CompilerParams(dimension_semantics=sem, vmem_limit_bytes=VMEM_LIMIT)


def _row_tile(rows, want):
    t = min(rows, want)
    assert rows % t == 0, (rows, want)
    return t


def _ada_kernel(cc_ref, w_ref, b_ref, o_ref):
    s = jax.nn.silu(cc_ref[...]).astype(bf16)
    o_ref[...] = jnp.dot(s, w_ref[...].astype(bf16), preferred_element_type=f32) + b_ref[...]


def _ada(cc, w_ada, b_ada):
    depth, d, n = w_ada.shape
    g = cc.shape[0]
    tn = 512
    return pl.pallas_call(
        _ada_kernel,
        grid=(depth, n // tn),
        in_specs=[pl.BlockSpec((g, d), lambda l, j: (0, 0)),
                  pl.BlockSpec((None, d, tn), lambda l, j: (l, 0, j)),
                  pl.BlockSpec((None, 1, tn), lambda l, j: (l, 0, j))],
        out_specs=pl.BlockSpec((None, g, tn), lambda l, j: (l, 0, j)),
        out_shape=jax.ShapeDtypeStruct((depth, g, n), f32),
        compiler_params=_cp("arbitrary", "arbitrary"),
        name="ada",
    )(cc, w_ada, b_ada.reshape(depth, 1, n))


def _row_chunks(n_rows, chunk, body):
    def step(i, carry):
        body(pl.ds(pl.multiple_of(i * chunk, chunk), chunk))
        return carry

    lax.fori_loop(0, n_rows // chunk, step, 0, unroll=8)


def _norm_mod_rows(x_ref, g_ref, sh_ref, sc_ref, h_ref):
    g, shift, scale1 = g_ref[...], sh_ref[...], 1.0 + sc_ref[...]

    def body(rows):
        x = x_ref[rows, :]
        ms = jnp.mean(x * x, axis=-1, keepdims=True)
        h_ref[rows, :] = ((x * lax.rsqrt(ms + NORM_EPS) * g) * scale1 + shift).astype(bf16)

    _row_chunks(x_ref.shape[0], 2 * SUBLANES, body)


def _residual_rows(acc_ref, x_ref, g_ref, gate_ref, o_ref):
    g, gate = g_ref[...], gate_ref[...]

    def body(rows):
        acc = acc_ref[rows, :]
        ms = jnp.mean(acc * acc, axis=-1, keepdims=True)
        o_ref[rows, :] = x_ref[rows, :] + gate * (acc * lax.rsqrt(ms + NORM_EPS) * g)

    _row_chunks(x_ref.shape[0], SUBLANES, body)


def _in_proj_kernel(x_ref, g_ref, sh_ref, sc_ref, w_ref, o_ref, h_ref):
    @pl.when(pl.program_id(1) == 0)
    def _():
        _norm_mod_rows(x_ref, g_ref, sh_ref, sc_ref, h_ref)

    o_ref[...] = jnp.dot(h_ref[...], w_ref[...], preferred_element_type=f32).astype(o_ref.dtype)


def _ffn_kernel(x_ref, g_ref, sh_ref, sc_ref, wg_ref, wu_ref, wo_ref, g2_ref, gate_ref, o_ref, h_ref, acc_ref):
    j = pl.program_id(1)

    @pl.when(j == 0)
    def _():
        _norm_mod_rows(x_ref, g_ref, sh_ref, sc_ref, h_ref)
        acc_ref[...] = jnp.zeros_like(acc_ref)

    h = h_ref[...]
    a = jnp.dot(h, wg_ref[...], preferred_element_type=f32)
    b = jnp.dot(h, wu_ref[...], preferred_element_type=f32)
    act = (jax.nn.silu(a) * b).astype(bf16)
    acc_ref[...] += jnp.dot(act, wo_ref[...], preferred_element_type=f32)

    @pl.when(j == pl.num_programs(1) - 1)
    def _():
        _residual_rows(acc_ref, x_ref, g2_ref, gate_ref, o_ref)


def _mod_spec(l, grp, k, d):
    return pl.BlockSpec((None, None, 1, d), lambda i, j: (l, grp(i), 0, k))


def _in_proj(x, g, mods, l, grp, w, tm, tn):
    rows, d = x.shape
    n = w.shape[2]
    return pl.pallas_call(
        _in_proj_kernel,
        grid=(rows // tm, n // tn),
        in_specs=[pl.BlockSpec((tm, d), lambda i, j: (i, 0)),
                  pl.BlockSpec((1, d), lambda i, j: (0, 0)),
                  _mod_spec(l, grp, 0, d), _mod_spec(l, grp, 1, d),
                  pl.BlockSpec((None, d, tn), lambda i, j: (l, 0, j))],
        out_specs=pl.BlockSpec((tm, tn), lambda i, j: (i, j)),
        out_shape=jax.ShapeDtypeStruct((rows, n), bf16),
        scratch_shapes=[pltpu.VMEM((tm, d), bf16)],
        compiler_params=_cp("arbitrary", "arbitrary"),
        name="in_proj",
    )(x, g, mods, mods, w)


def _ffn(x, g_pre, g_post, mods, l, grp, w_in, w_out, tm, tn):
    rows, d = x.shape
    dff = w_out.shape[1]
    nj = dff // tn
    return pl.pallas_call(
        _ffn_kernel,
        grid=(rows // tm, nj),
        in_specs=[pl.BlockSpec((tm, d), lambda i, j: (i, 0)),
                  pl.BlockSpec((1, d), lambda i, j: (0, 0)),
                  _mod_spec(l, grp, 3, d), _mod_spec(l, grp, 4, d),
                  pl.BlockSpec((None, d, tn), lambda i, j: (l, 0, j)),
                  pl.BlockSpec((None, d, tn), lambda i, j: (l, 0, j + nj)),
                  pl.BlockSpec((None, tn, d), lambda i, j: (l, j, 0)),
                  pl.BlockSpec((1, d), lambda i, j: (0, 0)),
                  _mod_spec(l, grp, 5, d)],
        out_specs=pl.BlockSpec((tm, d), lambda i, j: (i, 0)),
        out_shape=jax.ShapeDtypeStruct((rows, d), f32),
        scratch_shapes=[pltpu.VMEM((tm, d), bf16), pltpu.VMEM((tm, d), f32)],
        compiler_params=_cp("arbitrary", "arbitrary"),
        name="ffn",
    )(x, g_pre, mods, mods, w_in, w_in, w_out, g_post, mods)


def _mix_out_kernel(y0_ref, y1_ref, y2_ref, y3_ref, w_ref, x_ref, g_ref, gate_ref, o_ref):
    tk = y0_ref.shape[1]
    acc = None
    for s, y_ref in enumerate((y0_ref, y1_ref, y2_ref, y3_ref)):
        part = jnp.dot(y_ref[...], w_ref[s * tk:(s + 1) * tk, :], preferred_element_type=f32)
        acc = part if acc is None else acc + part
    ms = jnp.mean(acc * acc, axis=-1, keepdims=True)
    o_ref[...] = x_ref[...] + gate_ref[...] * (acc * lax.rsqrt(ms + NORM_EPS) * g_ref[...])


def _mix_out(ys, w, x, g, mods, l, grp, tm):
    rows, d = x.shape
    tk = ys[0].shape[1]
    mod = _mod_spec(l, grp, 2, d)
    return pl.pallas_call(
        _mix_out_kernel,
        grid=(rows // tm,),
        in_specs=[pl.BlockSpec((tm, tk), lambda i: (i, 0))] * len(ys) + [
            pl.BlockSpec((None, len(ys) * tk, d), lambda i: (l, 0, 0)),
            pl.BlockSpec((tm, d), lambda i: (i, 0)),
            pl.BlockSpec((1, d), lambda i: (0, 0)),
            pl.BlockSpec(mod.block_shape, lambda i: mod.index_map(i, 0))],
        out_specs=pl.BlockSpec((tm, d), lambda i: (i, 0)),
        out_shape=jax.ShapeDtypeStruct((rows, d), f32),
        compiler_params=_cp("arbitrary"),
        name="mix_out",
    )(*ys, w, x, g, mods)


def _rope_tables(n_tokens):
    n_rows = n_tokens // GRID_W
    row = jnp.repeat(jnp.arange(n_rows, dtype=f32), GRID_W)
    col = jnp.tile(jnp.arange(GRID_W, dtype=f32), n_rows)
    n_freq = ROT_DIM // 4
    inv = ROPE_BASE ** (-jnp.arange(n_freq, dtype=f32) / n_freq)
    ang = jnp.concatenate([row[:, None] * inv, col[:, None] * inv], axis=-1)
    cos, sin = jnp.cos(ang), jnp.sin(ang)
    return jnp.tile(cos, (1, 4)), jnp.tile(jnp.concatenate([-sin, sin], axis=-1), (1, 2))


def _rope_slab(x, cos, sin):
    lane = lax.broadcasted_iota(jnp.int32, x.shape, 1)
    partner = jnp.where(lane % ROT_DIM < ROT_DIM // 2,
                        pltpu.roll(x, LANES - ROT_DIM // 2, 1), pltpu.roll(x, ROT_DIM // 2, 1))
    return x * cos + partner * sin


def _da_prep_kernel(q_ref, k_ref, v_ref, cos_ref, sin_ref, qo_ref, ko_ref, vo_ref, *, scale):
    cos, sin = cos_ref[...], sin_ref[...]
    for h in range(SEC // LANES):
        sl = slice(h * LANES, (h + 1) * LANES)
        qo_ref[sl, :] = (_rope_slab(q_ref[:, sl].astype(f32), cos, sin) * scale).T.astype(bf16)
        ko_ref[:, sl] = _rope_slab(k_ref[:, sl].astype(f32), cos, sin).astype(bf16)
        vo_ref[sl, :] = v_ref[:, sl].astype(f32).T.astype(bf16)


def _da_prep(px, cos, sin, pos_tiles, tm, scale):
    rows = px.shape[0]
    sec = lambda s: pl.BlockSpec((tm, SEC), lambda i: (i, s))
    tab = pl.BlockSpec((tm, LANES), lambda i: (i % pos_tiles, 0))
    nat = pl.BlockSpec((tm, SEC), lambda i: (i, 0))
    tr = pl.BlockSpec((SEC, tm), lambda i: (0, i))
    return pl.pallas_call(
        functools.partial(_da_prep_kernel, scale=scale),
        grid=(rows // tm,),
        in_specs=[sec(0), sec(1), sec(2), tab, tab],
        out_specs=[tr, nat, tr],
        out_shape=[jax.ShapeDtypeStruct((SEC, rows), bf16), jax.ShapeDtypeStruct((rows, SEC), bf16),
                   jax.ShapeDtypeStruct((SEC, rows), bf16)],
        compiler_params=_cp("arbitrary"),
        name="da_prep",
    )(px, px, px, cos, sin)


def _rms_g(x, g):
    ms = jnp.mean(x * x, axis=-1, keepdims=True)
    return x * lax.rsqrt(ms + NORM_EPS) * g


def _mla_prep_kernel(cq_ref, ckv_ref, gq_ref, gkv_ref, wq_ref, wkv_ref, cos_ref, sin_ref,
                     qo_ref, ko_ref, vo_ref, *, scale):
    cos, sin = cos_ref[...], sin_ref[...]
    n_heads = SEC // HEAD_W
    q = jnp.dot(_rms_g(cq_ref[...].astype(f32), gq_ref[...]).astype(bf16), wq_ref[...],
                preferred_element_type=f32)
    kv = jnp.dot(_rms_g(ckv_ref[:, :MLA_KV_RANK].astype(f32), gkv_ref[...]).astype(bf16), wkv_ref[...],
                 preferred_element_type=f32)
    kr = _rope_slab(ckv_ref[:, MLA_KV_RANK:MLA_KV_RANK + LANES].astype(f32), cos, sin).astype(bf16)
    for h in range(n_heads):
        base = h * MLA_QK
        qo_ref[base:base + LANES, :] = (q[:, base:base + LANES] * scale).T.astype(bf16)
        qo_ref[base + LANES:base + MLA_QK, :] = (
            _rope_slab(q[:, base + LANES:base + MLA_QK], cos, sin) * scale).T.astype(bf16)
        ko_ref[:, base:base + LANES] = kv[:, h * LANES:(h + 1) * LANES].astype(bf16)
        ko_ref[:, base + LANES:base + MLA_QK] = kr
        vo_ref[h * LANES:(h + 1) * LANES, :] = kv[:, (n_heads + h) * LANES:(n_heads + h + 1) * LANES].T.astype(bf16)


def _mla_prep(px, gq, gkv, wq, wkv, cos, sin, pos_tiles, tm, scale):
    rows = px.shape[0]
    n_heads = SEC // HEAD_W
    sec = lambda s: pl.BlockSpec((tm, SEC), lambda i: (i, s))
    tab = pl.BlockSpec((tm, LANES), lambda i: (i % pos_tiles, 0))
    full = lambda a: pl.BlockSpec(a.shape, lambda i: (0,) * a.ndim)
    return pl.pallas_call(
        functools.partial(_mla_prep_kernel, scale=scale),
        grid=(rows // tm,),
        in_specs=[sec(4), sec(5), full(gq), full(gkv), full(wq), full(wkv), tab, tab],
        out_specs=[pl.BlockSpec((n_heads * MLA_QK, tm), lambda i: (0, i)),
                   pl.BlockSpec((tm, n_heads * MLA_QK), lambda i: (i, 0)),
                   pl.BlockSpec((SEC, tm), lambda i: (0, i))],
        out_shape=[jax.ShapeDtypeStruct((n_heads * MLA_QK, rows), bf16),
                   jax.ShapeDtypeStruct((rows, n_heads * MLA_QK), bf16),
                   jax.ShapeDtypeStruct((SEC, rows), bf16)],
        compiler_params=_cp("arbitrary"),
        name="mla_prep",
    )(px, px, gq, gkv, wq, wkv, cos, sin)


def _attn_kernel(*refs, diff, has_x_keys, tk, lambda_init):
    refs = list(refs)
    n_units = (2 if diff else 1) * (refs[0].shape[1] // refs[-1].shape[1])
    p_refs = [refs.pop() for _ in range(n_units)][::-1]
    s_refs = [refs.pop() for _ in range(n_units)][::-1]
    o_ref = refs.pop()
    q_ref, kc_ref, vc_ref = refs[:3]
    refs = refs[3:]
    n_ctx = kc_ref.shape[0]
    chunks = [(kc_ref, 0, 0, n_ctx)]
    if has_x_keys:
        kx_ref, vx_ref = refs[:2]
        refs = refs[2:]
        chunks += [(kx_ref, c * tk, n_ctx + c * tk, tk) for c in range(kx_ref.shape[0] // tk)]
    qt = q_ref[...]
    if diff:
        feat = lax.broadcasted_iota(jnp.int32, qt.shape, 0)
        zero = jnp.zeros_like(qt)
        maps = [jnp.where(feat < ROT_DIM, qt, zero), jnp.where(feat >= ROT_DIM, qt, zero)]
    else:
        maps = [qt]
    tq = s_refs[0].shape[1]
    n_col = qt.shape[1] // tq
    qs = [qm[:, c * tq:(c + 1) * tq] for qm in maps for c in range(n_col)]
    fold = lambda a: a.reshape(a.shape[0] // ACC_ROWS, ACC_ROWS, a.shape[1])

    def scores(u):
        m_acc = jnp.full((ACC_ROWS, tq), -jnp.inf, f32)
        for k_ref, start, row0, size in chunks:
            s = jnp.dot(k_ref[pl.ds(start, size), :], qs[u], preferred_element_type=f32)
            s_refs[u][pl.ds(row0, size), :] = s
            m_acc = jnp.maximum(m_acc, jnp.max(fold(s), axis=0))
        return jnp.max(m_acc, axis=0, keepdims=True)

    def values(u, m_col):
        l_acc = jnp.zeros((ACC_ROWS, tq), f32)
        for _, _, row0, size in chunks:
            p = jnp.exp2(s_refs[u][pl.ds(row0, size), :] - m_col)
            l_acc = l_acc + jnp.sum(fold(p), axis=0)
            p_refs[u][pl.ds(row0, size), :] = p.astype(bf16)
        ot = jnp.dot(vc_ref[...], p_refs[u][pl.ds(0, n_ctx), :], preferred_element_type=f32)
        if has_x_keys:
            ot = ot + jnp.dot(vx_ref[...], p_refs[u][pl.ds(n_ctx, vx_ref.shape[1]), :], preferred_element_type=f32)
        return (ot / jnp.sum(l_acc, axis=0, keepdims=True)).T

    outs = []
    m_prev = scores(0)
    for u in range(1, len(qs)):
        m_next = scores(u)
        outs.append(values(u - 1, m_prev))
        m_prev = m_next
    outs.append(values(len(qs) - 1, m_prev))
    if diff:
        lq1, lk1, lq2, lk2, subln = (r[...] for r in refs)
        lam = (jnp.exp(jnp.sum(lq1 * lk1, axis=-1, keepdims=True))
               - jnp.exp(jnp.sum(lq2 * lk2, axis=-1, keepdims=True)) + lambda_init)
    for c in range(n_col):
        rows = pl.ds(c * tq, tq)
        if diff:
            dlt = outs[c] - lam * outs[n_col + c]
            o_ref[rows, :] = (_rms_g(dlt, subln) * (1.0 - lambda_init)).astype(o_ref.dtype)
        else:
            o_ref[rows, :] = outs[c].astype(o_ref.dtype)


def _attention(qt, kc, vct, kx, vxt, params, n_batch, ctx_len, tq, tk, lambda_init):
    n_heads = SEC // HEAD_W
    dq = qt.shape[0] // n_heads
    lq = qt.shape[1] // n_batch
    nq = lq // tq
    diff = params is not None
    in_specs = [pl.BlockSpec((dq, tq), lambda b, h, i: (h, b * nq + i)),
                pl.BlockSpec((ctx_len, dq), lambda b, h, i: (b, h)),
                pl.BlockSpec((HEAD_W, ctx_len), lambda b, h, i: (h, b))]
    args = [qt, kc, vct]
    n_keys = ctx_len
    if kx is not None:
        lx = kx.shape[0] // n_batch
        n_keys += lx
        in_specs += [pl.BlockSpec((lx, dq), lambda b, h, i: (b, h)),
                     pl.BlockSpec((HEAD_W, lx), lambda b, h, i: (h, b))]
        args += [kx, vxt]
    if diff:
        in_specs += [pl.BlockSpec(p.shape, lambda b, h, i: (0, 0)) for p in params]
        args += list(params)
    unit = min(tq, ATTN_UNIT)
    n_units = (2 if diff else 1) * (tq // unit)
    return pl.pallas_call(
        functools.partial(_attn_kernel, diff=diff, has_x_keys=kx is not None, tk=tk, lambda_init=lambda_init),
        grid=(n_batch, n_heads, nq),
        in_specs=in_specs,
        out_specs=pl.BlockSpec((tq, HEAD_W), lambda b, h, i: (b * nq + i, h)),
        out_shape=jax.ShapeDtypeStruct((qt.shape[1], n_heads * HEAD_W), bf16),
        scratch_shapes=[pltpu.VMEM((n_keys, unit), f32)] * n_units + [pltpu.VMEM((n_keys, unit), bf16)] * n_units,
        compiler_params=_cp("arbitrary", "arbitrary", "arbitrary"),
        name="attention",
    )(*args)


def _s5_kernel(uc_ref, ux_ref, lre_ref, lim_ref, lst_ref, btr_ref, bti_ref, ctr_ref, cti_ref,
               yc_ref, yx_ref, wb_re, wb_im, wc_re, wc_im, a_re, a_im, h_re, h_im, s_re, s_im):
    d = pl.program_id(0)
    c = pl.program_id(1)
    n_batch, t_len, width = uc_ref.shape
    n_state = lre_ref.shape[1]
    half = n_state // 2
    n_groups = width // S5_GROUP
    n_blk = half // LANES

    @pl.when(c == 0)
    def _():
        lre, lim = lre_ref[...], lim_ref[...]
        step = jnp.exp(lst_ref[...])
        mag = jnp.exp(lre * step)
        are = mag * jnp.cos(lim * step)
        aim = mag * jnp.sin(lim * step)
        den = lre * lre + lim * lim
        fre = ((are - 1.0) * lre + aim * lim) / den
        fim = (aim * lre - (are - 1.0) * lim) / den
        row = lax.broadcasted_iota(jnp.int32, (width, n_state), 0) // S5_GROUP
        col = lax.broadcasted_iota(jnp.int32, (width, n_state), 1) // S5_STATE
        blk = row == col
        tile = lambda r: jnp.broadcast_to(r[...][None], (n_groups, S5_GROUP, n_state)).reshape(width, n_state)
        btr, bti = tile(btr_ref), tile(bti_ref)
        wb_re[...] = jnp.where(blk, fre * btr - fim * bti, 0.0).astype(bf16)
        wb_im[...] = jnp.where(blk, fre * bti + fim * btr, 0.0).astype(bf16)
        wc_re[...] = jnp.where(blk, tile(ctr_ref), 0.0).astype(bf16)
        wc_im[...] = jnp.where(blk, -tile(cti_ref), 0.0).astype(bf16)
        for hf in range(2):
            rows = slice(hf * n_batch, (hf + 1) * n_batch)
            for j in range(n_blk):
                cols = slice(hf * half + j * LANES, hf * half + (j + 1) * LANES)
                a_re[j, rows, :] = jnp.broadcast_to(are[:, cols], (n_batch, LANES))
                a_im[j, rows, :] = jnp.broadcast_to(aim[:, cols], (n_batch, LANES))
        h_re[...] = jnp.zeros_like(h_re)
        h_im[...] = jnp.zeros_like(h_im)

    pitch = s_re.shape[1] // (2 * n_batch)
    kb = LANES
    sb = kb // S5_GROUP * S5_STATE
    nt = (((1,), (1,)), ((), ()))

    def slab(col0, b):
        hf, rem = divmod(col0, half)
        return rem // LANES, pl.ds((hf * n_batch + b) * pitch, t_len)

    def run(u_ref, y_ref):
        for b in range(n_batch):
            u = u_ref[b]
            for bb in range(width // kb):
                ub = u[:, bb * kb:(bb + 1) * kb]
                bre = jnp.dot(ub, wb_re[bb * kb:(bb + 1) * kb, bb * sb:(bb + 1) * sb], preferred_element_type=f32)
                bim = jnp.dot(ub, wb_im[bb * kb:(bb + 1) * kb, bb * sb:(bb + 1) * sb], preferred_element_type=f32)
                for jj in range(sb // LANES):
                    j, rows = slab(bb * sb + jj * LANES, b)
                    s_re[j, rows, :] = bre[:, jj * LANES:(jj + 1) * LANES]
                    s_im[j, rows, :] = bim[:, jj * LANES:(jj + 1) * LANES]

        ar = [a_re[j] for j in range(n_blk)]
        ai = [a_im[j] for j in range(n_blk)]

        def step(k, carry):
            hr, hi = carry
            t = k + d * (t_len - 1 - 2 * k)
            rows = pl.ds(t, 2 * n_batch, stride=pitch)
            nr, ni = [], []
            for j in range(n_blk):
                nr.append(ar[j] * hr[j] - ai[j] * hi[j] + s_re[j, rows, :])
                ni.append(ar[j] * hi[j] + ai[j] * hr[j] + s_im[j, rows, :])
                s_re[j, rows, :] = nr[j]
                s_im[j, rows, :] = ni[j]
            return nr, ni

        init = ([h_re[j] for j in range(n_blk)], [h_im[j] for j in range(n_blk)])
        hr, hi = lax.fori_loop(0, t_len, step, init)
        for j in range(n_blk):
            h_re[j] = hr[j]
            h_im[j] = hi[j]

        for b in range(n_batch):
            ys = []
            for bb in range(width // kb):
                slabs = [slab(bb * sb + jj * LANES, b) for jj in range(sb // LANES)]
                sr = jnp.concatenate([s_re[j, rows, :] for j, rows in slabs], axis=1).astype(bf16)
                si = jnp.concatenate([s_im[j, rows, :] for j, rows in slabs], axis=1).astype(bf16)
                ys.append(
                    lax.dot_general(sr, wc_re[bb * kb:(bb + 1) * kb, bb * sb:(bb + 1) * sb], nt,
                                    preferred_element_type=f32)
                    + lax.dot_general(si, wc_im[bb * kb:(bb + 1) * kb, bb * sb:(bb + 1) * sb], nt,
                                      preferred_element_type=f32))
            y_ref[b] = jnp.concatenate(ys, axis=1)

    @pl.when(c == 0)
    def _():
        run(uc_ref, yc_ref)

    @pl.when(c > 0)
    def _():
        run(ux_ref, yx_ref)


def _s5_scan(px_c, px_x, lam_re, lam_im, log_step, bt_re, bt_im, ct_re, ct_im, n_batch):
    t_len = px_c.shape[1]
    seq = px_x.shape[1]
    assert seq % t_len == 0
    nch = seq // t_len
    n_state = lam_re.shape[-1]
    half = n_state // 2
    pitch = t_len + SUBLANES
    assert t_len % (2 * SUBLANES) == 0

    def x_chunk(d, c):
        return jnp.where(d == 0, jnp.maximum(c - 1, 0), jnp.minimum(nch - c, nch - 1))

    vec = pl.BlockSpec((None, 1, n_state), lambda d, c: (d, 0, 0))
    mat = pl.BlockSpec((None, S5_GROUP, n_state), lambda d, c: (d, 0, 0))
    return pl.pallas_call(
        _s5_kernel,
        grid=(2, nch + 1),
        in_specs=[pl.BlockSpec((n_batch, t_len, SEC), lambda d, c: (0, 0, 3)),
                  pl.BlockSpec((n_batch, t_len, SEC), lambda d, c: (0, x_chunk(d, c), 3)),
                  vec, vec, vec, mat, mat, mat, mat],
        out_specs=[pl.BlockSpec((None, n_batch, t_len, SEC), lambda d, c: (d, 0, 0, 0)),
                   pl.BlockSpec((None, n_batch, t_len, SEC), lambda d, c: (d, 0, x_chunk(d, c), 0))],
        out_shape=[jax.ShapeDtypeStruct((2, n_batch, t_len, SEC), f32),
                   jax.ShapeDtypeStruct((2, n_batch, seq, SEC), f32)],
        scratch_shapes=[pltpu.VMEM((SEC, n_state), bf16)] * 4
        + [pltpu.VMEM((half // LANES, 2 * n_batch, LANES), f32)] * 4
        + [pltpu.VMEM((half // LANES, 2 * n_batch * pitch, LANES), f32)] * 2,
        compiler_params=_cp("arbitrary", "arbitrary"),
        name="s5_scan",
    )(px_c, px_x, lam_re, lam_im, log_step, bt_re, bt_im, ct_re, ct_im)


def _s5_glu_kernel(yf_ref, yb_ref, u_ref, d_ref, w_ref, o_ref):
    y = d_ref[...] * u_ref[...].astype(f32) + yf_ref[...] + yb_ref[...]
    g = jax.nn.gelu(y)
    z = jnp.dot(g.astype(bf16), w_ref[...], preferred_element_type=f32)
    o_ref[...] = (g * jax.nn.sigmoid(z)).astype(o_ref.dtype)


def _s5_glu(y, px, s5_d, w_glu, tm):
    rows = px.shape[0]
    return pl.pallas_call(
        _s5_glu_kernel,
        grid=(rows // tm,),
        in_specs=[pl.BlockSpec((None, tm, SEC), lambda i: (0, i, 0)),
                  pl.BlockSpec((None, tm, SEC), lambda i: (1, i, 0)),
                  pl.BlockSpec((tm, SEC), lambda i: (i, 3)),
                  pl.BlockSpec((1, SEC), lambda i: (0, 0)),
                  pl.BlockSpec((SEC, SEC), lambda i: (0, 0))],
        out_specs=pl.BlockSpec((tm, SEC), lambda i: (i, 0)),
        out_shape=jax.ShapeDtypeStruct((rows, SEC), bf16),
        compiler_params=_cp("arbitrary"),
        name="s5_glu",
    )(y, y, px, s5_d, w_glu)


CONV_HALO = 16


def _conv_kernel(vp_ref, vc_ref, vn_ref, gp_ref, gc_ref, gn_ref, w_ref, b_ref, lg_ref, lb_ref, o_ref, ext_ref):
    j = pl.program_id(1)
    tc = vc_ref.shape[0]
    glu = lambda v, g: v[...].astype(f32) * jax.nn.sigmoid(g[...].astype(f32))
    ext_ref[0, pl.ds(0, CONV_HALO), :] = jnp.where(j > 0, glu(vp_ref, gp_ref), 0.0)
    ext_ref[0, pl.ds(CONV_HALO, tc), :] = glu(vc_ref, gc_ref)
    ext_ref[0, pl.ds(CONV_HALO + tc, CONV_HALO), :] = jnp.where(j < pl.num_programs(1) - 1, glu(vn_ref, gn_ref), 0.0)
    n_keep = tc + 2 * CONV_HALO - SUBLANES
    for s in range(1, SUBLANES):
        ext_ref[s, pl.ds(0, n_keep), :] = ext_ref[0, pl.ds(s, n_keep), :]
    y = jnp.zeros((tc, vc_ref.shape[1]), f32)
    for k in range(CONV_K):
        off = CONV_HALO - CONV_K // 2 + k
        y += w_ref[pl.ds(k, 1), :] * ext_ref[off % SUBLANES, pl.ds(off - off % SUBLANES, tc), :]
    y = y + b_ref[...]
    mu = jnp.mean(y, axis=-1, keepdims=True)
    var = jnp.mean(jnp.square(y - mu), axis=-1, keepdims=True)
    z = (y - mu) * lax.rsqrt(var + NORM_EPS) * lg_ref[...] + lb_ref[...]
    o_ref[...] = jax.nn.silu(z).astype(o_ref.dtype)


def _conv(px, conv_w, conv_b, ln_g, ln_b, n_batch, tc):
    rows = px.shape[0]
    seq = rows // n_batch
    nt = seq // tc
    hpt = tc // CONV_HALO
    cur = lambda s: pl.BlockSpec((tc, SEC), lambda b, j: (b * nt + j, s))
    prev = lambda s: pl.BlockSpec((CONV_HALO, SEC), lambda b, j: (jnp.maximum((b * nt + j) * hpt - 1, 0), s))
    nxt = lambda s: pl.BlockSpec(
        (CONV_HALO, SEC), lambda b, j: (jnp.minimum((b * nt + j + 1) * hpt, rows // CONV_HALO - 1), s))
    full = lambda a: pl.BlockSpec(a.shape, lambda b, j: (0,) * a.ndim)
    return pl.pallas_call(
        _conv_kernel,
        grid=(n_batch, nt),
        in_specs=[prev(6), cur(6), nxt(6), prev(7), cur(7), nxt(7),
                  full(conv_w), full(conv_b), full(ln_g), full(ln_b)],
        out_specs=pl.BlockSpec((tc, SEC), lambda b, j: (b * nt + j, 0)),
        out_shape=jax.ShapeDtypeStruct((rows, SEC), bf16),
        scratch_shapes=[pltpu.VMEM((SUBLANES, tc + 2 * CONV_HALO, SEC), f32)],
        compiler_params=_cp("arbitrary", "arbitrary"),
        name="conv",
    )(px, px, px, px, px, px, conv_w, conv_b, ln_g, ln_b)


def kernel(x, c, ctx, c_ctx, w_ada, b_ada, norm_pre_mix, norm_post_mix, norm_pre_ffn, norm_post_ffn, w_in, w_out, da_lam_q1, da_lam_k1, da_lam_q2, da_lam_k2, da_subln, s5_lam_re, s5_lam_im, s5_log_step, s5_b_re, s5_b_im, s5_c_re, s5_c_im, s5_d, s5_w_glu, mla_q_norm, mla_kv_norm, mla_w_uq, mla_w_ukv, conv_w, conv_b, conv_ln_g, conv_ln_b, w_ffn_in, w_ffn_out):
    n_batch, seq, d = x.shape
    ctx_len = ctx.shape[1]
    depth = w_in.shape[0]
    n_state = s5_lam_re.shape[2] * s5_lam_re.shape[3]
    mla_end = 4 * SEC + SEC + MLA_KV_RANK + ROT_DIM
    assert d == 4 * SEC and w_in.shape[2] == mla_end + 2 * SEC

    da_scale = ROT_DIM ** -0.5 * math.log2(math.e)
    mla_scale = (HEAD_W + ROT_DIM) ** -0.5 * math.log2(math.e)

    tm_x = _row_tile(seq, 1024)
    tm_c = _row_tile(n_batch * ctx_len, 1024)
    tpb = seq // tm_x
    grp_x = lambda i: i // tpb
    grp_c = lambda i: n_batch
    tq_da = _row_tile(seq, 512)
    tq_mla = _row_tile(seq, 1024)
    tk = _row_tile(seq, 512)

    n_grp = -(-(n_batch + 1) // SUBLANES) * SUBLANES
    cc = jnp.concatenate([c, c_ctx[None], jnp.zeros((n_grp - n_batch - 1, d), f32)], axis=0)
    mods = _ada(cc, w_ada, b_ada).reshape(depth, n_grp, 1, 6 * d)

    cos_x, sin_x = _rope_tables(seq)
    cos_c, sin_c = jnp.ones((tm_c, LANES), f32), jnp.zeros((tm_c, LANES), f32)

    w_in_p = jnp.concatenate([w_in[:, :, :mla_end], jnp.zeros((depth, d, N_SEC * SEC - w_in.shape[2]), f32),
                              w_in[:, :, mla_end:]], axis=2).astype(bf16)
    w_out_b = w_out.astype(bf16)
    w_ffn_in_b = w_ffn_in.astype(bf16)
    w_ffn_out_b = w_ffn_out.astype(bf16)
    w_glu_b = s5_w_glu.astype(bf16)
    n_mla = SEC // HEAD_W
    wq = mla_w_uq.reshape(depth, SEC, n_mla, HEAD_W + ROT_DIM)
    wq = jnp.pad(wq, ((0, 0), (0, 0), (0, 0), (0, MLA_QK - HEAD_W - ROT_DIM))).reshape(depth, SEC, n_mla * MLA_QK)
    wq = wq.astype(bf16)
    wkv = mla_w_ukv.reshape(depth, MLA_KV_RANK, n_mla, 2, HEAD_W).transpose(0, 1, 3, 2, 4)
    wkv = wkv.reshape(depth, MLA_KV_RANK, 2 * n_mla * HEAD_W).astype(bf16)

    flat = lambda a: a.reshape(depth, 2, 1, n_state)
    s5_lre, s5_lim = flat(s5_lam_re), flat(s5_lam_im)
    s5_lst = flat(jnp.broadcast_to(s5_log_step[..., None], s5_lam_re.shape))
    s5_btr = s5_b_re.transpose(0, 1, 4, 2, 3).reshape(depth, 2, S5_GROUP, n_state)
    s5_bti = s5_b_im.transpose(0, 1, 4, 2, 3).reshape(depth, 2, S5_GROUP, n_state)
    s5_ctr = s5_c_re.transpose(0, 1, 3, 2, 4).reshape(depth, 2, S5_GROUP, n_state)
    s5_cti = s5_c_im.transpose(0, 1, 3, 2, 4).reshape(depth, 2, S5_GROUP, n_state)

    row = lambda a, l: a[l][None]
    conv_w_p = jnp.pad(conv_w, ((0, 0), (0, 1), (0, 0)))

    xs = x.reshape(n_batch * seq, d)
    cs = ctx.reshape(n_batch * ctx_len, d)
    for l in range(depth):
        last = l == depth - 1
        lambda_init = 0.8 - 0.6 * math.exp(-0.3 * l)
        g_pre = row(norm_pre_mix, l)

        px = _in_proj(xs, g_pre, mods, l, grp_x, w_in_p, tm_x, 2 * SEC)
        pc = _in_proj(cs, g_pre, mods, l, grp_c, w_in_p, tm_c, 2 * SEC)

        qx, kx, vx = _da_prep(px, cos_x, sin_x, tpb, tm_x, da_scale)
        qc, kc, vc = _da_prep(pc, cos_c, sin_c, 1, tm_c, da_scale)
        da_params = (row(da_lam_q1, l), row(da_lam_k1, l), row(da_lam_q2, l), row(da_lam_k2, l), row(da_subln, l))
        y_da_x = _attention(qx, kc, vc, kx, vx, da_params, n_batch, ctx_len, tq_da, tk, lambda_init)

        mla_w = (row(mla_q_norm, l), row(mla_kv_norm, l), wq[l], wkv[l])
        mqx, mkx, mvx = _mla_prep(px, *mla_w, cos_x, sin_x, tpb, tm_x, mla_scale)
        mqc, mkc, mvc = _mla_prep(pc, *mla_w, cos_c, sin_c, 1, tm_c, mla_scale)
        y_mla_x = _attention(mqx, mkc, mvc, mkx, mvx, None, n_batch, ctx_len, tq_mla, tk, lambda_init)

        ys_c, ys_x = _s5_scan(pc.reshape(n_batch, ctx_len, -1), px.reshape(n_batch, seq, -1),
                              s5_lre[l], s5_lim[l], s5_lst[l], s5_btr[l], s5_bti[l], s5_ctr[l], s5_cti[l], n_batch)
        y_s5_x = _s5_glu(ys_x.reshape(2, n_batch * seq, SEC), px, row(s5_d, l), w_glu_b[l], tm_x)

        cv = (conv_w_p[l], row(conv_b, l), row(conv_ln_g, l), row(conv_ln_b, l))
        y_cv_x = _conv(px, *cv, n_batch, _row_tile(seq, 512))

        g_post, g_pre_f, g_post_f = row(norm_post_mix, l), row(norm_pre_ffn, l), row(norm_post_ffn, l)
        tm_o = _row_tile(tm_x, 512)
        x1 = _mix_out((y_da_x, y_s5_x, y_mla_x, y_cv_x), w_out_b, xs, g_post, mods, l,
                      lambda i: i // (seq // tm_o), tm_o)
        xs = _ffn(x1, g_pre_f, g_post_f, mods, l, lambda i: i // (seq // tm_o), w_ffn_in_b, w_ffn_out_b, tm_o, SEC)

        if not last:
            y_da_c = _attention(qc, kc, vc, None, None, da_params, n_batch, ctx_len, ctx_len, tk, lambda_init)
            y_mla_c = _attention(mqc, mkc, mvc, None, None, None, n_batch, ctx_len, ctx_len, tk, lambda_init)
            y_s5_c = _s5_glu(ys_c.reshape(2, n_batch * ctx_len, SEC), pc, row(s5_d, l), w_glu_b[l], tm_c)
            y_cv_c = _conv(pc, *cv, n_batch, ctx_len)
            tm_oc = _row_tile(tm_c, 512)
            c1 = _mix_out((y_da_c, y_s5_c, y_mla_c, y_cv_c), w_out_b, cs, g_post, mods, l, grp_c, tm_oc)
            cs = _ffn(c1, g_pre_f, g_post_f, mods, l, grp_c, w_ffn_in_b, w_ffn_out_b, tm_oc, SEC)
    return xs.reshape(n_batch, seq, d)
```

```python
import functools
import math

import jax
import jax.numpy as jnp
from jax import lax
from jax.experimental import pallas as pl
from jax.experimental.pallas import tpu as pltpu

f32 = jnp.float32
bf16 = jnp.bfloat16

NORM_EPS = 1e-6
ROPE_BASE = 10000.0
GRID_W = 64
ROT_DIM = 64
CONV_K = 31
S5_GROUP = 16
S5_STATE = 64

LANES = 128
SUBLANES = 8
SEC = 512
N_SEC = 8
HEAD_W = 128
MLA_QK = 256
MLA_KV_RANK = 256
ATTN_UNIT = 256
ACC_ROWS = SUBLANES
V7X_VMEM_BYTES = 64 * 1024 * 1024
VMEM_LIMIT = V7X_VMEM_BYTES * 7 // 8


def _cp(*sem):
    return pltpu.CompilerParams(dimension_semantics=sem, vmem_limit_bytes=VMEM_LIMIT)


def _row_tile(rows, want):
    t = min(rows, want)
    assert rows % t == 0, (rows, want)
    return t


def _ada_kernel(cc_ref, w_ref, b_ref, o_ref):
    s = jax.nn.silu(cc_ref[...]).astype(bf16)
    o_ref[...] = jnp.dot(s, w_ref[...].astype(bf16), preferred_element_type=f32) + b_ref[...]


def _ada(cc, w_ada, b_ada):
    depth, d, n = w_ada.shape
    g = cc.shape[0]
    tn = 512
    return pl.pallas_call(
        _ada_kernel,
        grid=(depth, n // tn),
        in_specs=[pl.BlockSpec((g, d), lambda l, j: (0, 0)),
                  pl.BlockSpec((None, d, tn), lambda l, j: (l, 0, j)),
                  pl.BlockSpec((None, 1, tn), lambda l, j: (l, 0, j))],
        out_specs=pl.BlockSpec((None, g, tn), lambda l, j: (l, 0, j)),
        out_shape=jax.ShapeDtypeStruct((depth, g, n), f32),
        compiler_params=_cp("arbitrary", "arbitrary"),
        name="ada",
    )(cc, w_ada, b_ada.reshape(depth, 1, n))


def _row_chunks(n_rows, chunk, body):
    def step(i, carry):
        body(pl.ds(pl.multiple_of(i * chunk, chunk), chunk))
        return carry

    lax.fori_loop(0, n_rows // chunk, step, 0, unroll=8)


def _norm_mod_rows(x_ref, g_ref, sh_ref, sc_ref, h_ref):
    g, shift, scale1 = g_ref[...], sh_ref[...], 1.0 + sc_ref[...]

    def body(rows):
        x = x_ref[rows, :]
        ms = jnp.mean(x * x, axis=-1, keepdims=True)
        h_ref[rows, :] = ((x * lax.rsqrt(ms + NORM_EPS) * g) * scale1 + shift).astype(bf16)

    _row_chunks(x_ref.shape[0], 2 * SUBLANES, body)


def _residual_rows(acc_ref, x_ref, g_ref, gate_ref, o_ref):
    g, gate = g_ref[...], gate_ref[...]

    def body(rows):
        acc = acc_ref[rows, :]
        ms = jnp.mean(acc * acc, axis=-1, keepdims=True)
        o_ref[rows, :] = x_ref[rows, :] + gate * (acc * lax.rsqrt(ms + NORM_EPS) * g)

    _row_chunks(x_ref.shape[0], SUBLANES, body)


def _in_proj_kernel(x_ref, g_ref, sh_ref, sc_ref, w_ref, o_ref, h_ref):
    @pl.when(pl.program_id(1) == 0)
    def _():
        _norm_mod_rows(x_ref, g_ref, sh_ref, sc_ref, h_ref)

    o_ref[...] = jnp.dot(h_ref[...], w_ref[...], preferred_element_type=f32).astype(o_ref.dtype)


def _ffn_kernel(x_ref, g_ref, sh_ref, sc_ref, wg_ref, wu_ref, wo_ref, g2_ref, gate_ref, o_ref, h_ref, acc_ref):
    j = pl.program_id(1)

    @pl.when(j == 0)
    def _():
        _norm_mod_rows(x_ref, g_ref, sh_ref, sc_ref, h_ref)
        acc_ref[...] = jnp.zeros_like(acc_ref)

    h = h_ref[...]
    a = jnp.dot(h, wg_ref[...], preferred_element_type=f32)
    b = jnp.dot(h, wu_ref[...], preferred_element_type=f32)
    act = (jax.nn.silu(a) * b).astype(bf16)
    acc_ref[...] += jnp.dot(act, wo_ref[...], preferred_element_type=f32)

    @pl.when(j == pl.num_programs(1) - 1)
    def _():
        _residual_rows(acc_ref, x_ref, g2_ref, gate_ref, o_ref)


def _mod_spec(l, grp, k, d):
    return pl.BlockSpec((None, None, 1, d), lambda i, j: (l, grp(i), 0, k))


def _in_proj(x, g, mods, l, grp, w, tm, tn):
    rows, d = x.shape
    n = w.shape[2]
    return pl.pallas_call(
        _in_proj_kernel,
        grid=(rows // tm, n // tn),
        in_specs=[pl.BlockSpec((tm, d), lambda i, j: (i, 0)),
                  pl.BlockSpec((1, d), lambda i, j: (0, 0)),
                  _mod_spec(l, grp, 0, d), _mod_spec(l, grp, 1, d),
                  pl.BlockSpec((None, d, tn), lambda i, j: (l, 0, j))],
        out_specs=pl.BlockSpec((tm, tn), lambda i, j: (i, j)),
        out_shape=jax.ShapeDtypeStruct((rows, n), bf16),
        scratch_shapes=[pltpu.VMEM((tm, d), bf16)],
        compiler_params=_cp("arbitrary", "arbitrary"),
        name="in_proj",
    )(x, g, mods, mods, w)


def _ffn(x, g_pre, g_post, mods, l, grp, w_in, w_out, tm, tn):
    rows, d = x.shape
    dff = w_out.shape[1]
    nj = dff // tn
    return pl.pallas_call(
        _ffn_kernel,
        grid=(rows // tm, nj),
        in_specs=[pl.BlockSpec((tm, d), lambda i, j: (i, 0)),
                  pl.BlockSpec((1, d), lambda i, j: (0, 0)),
                  _mod_spec(l, grp, 3, d), _mod_spec(l, grp, 4, d),
                  pl.BlockSpec((None, d, tn), lambda i, j: (l, 0, j)),
                  pl.BlockSpec((None, d, tn), lambda i, j: (l, 0, j + nj)),
                  pl.BlockSpec((None, tn, d), lambda i, j: (l, j, 0)),
                  pl.BlockSpec((1, d), lambda i, j: (0, 0)),
                  _mod_spec(l, grp, 5, d)],
        out_specs=pl.BlockSpec((tm, d), lambda i, j: (i, 0)),
        out_shape=jax.ShapeDtypeStruct((rows, d), f32),
        scratch_shapes=[pltpu.VMEM((tm, d), bf16), pltpu.VMEM((tm, d), f32)],
        compiler_params=_cp("arbitrary", "arbitrary"),
        name="ffn",
    )(x, g_pre, mods, mods, w_in, w_in, w_out, g_post, mods)


def _mix_out_kernel(y0_ref, y1_ref, y2_ref, y3_ref, w_ref, x_ref, g_ref, gate_ref, o_ref):
    tk = y0_ref.shape[1]
    acc = None
    for s, y_ref in enumerate((y0_ref, y1_ref, y2_ref, y3_ref)):
        part = jnp.dot(y_ref[...], w_ref[s * tk:(s + 1) * tk, :], preferred_element_type=f32)
        acc = part if acc is None else acc + part
    ms = jnp.mean(acc * acc, axis=-1, keepdims=True)
    o_ref[...] = x_ref[...] + gate_ref[...] * (acc * lax.rsqrt(ms + NORM_EPS) * g_ref[...])


def _mix_out(ys, w, x, g, mods, l, grp, tm):
    rows, d = x.shape
    tk = ys[0].shape[1]
    mod = _mod_spec(l, grp, 2, d)
    return pl.pallas_call(
        _mix_out_kernel,
        grid=(rows // tm,),
        in_specs=[pl.BlockSpec((tm, tk), lambda i: (i, 0))] * len(ys) + [
            pl.BlockSpec((None, len(ys) * tk, d), lambda i: (l, 0, 0)),
            pl.BlockSpec((tm, d), lambda i: (i, 0)),
            pl.BlockSpec((1, d), lambda i: (0, 0)),
            pl.BlockSpec(mod.block_shape, lambda i: mod.index_map(i, 0))],
        out_specs=pl.BlockSpec((tm, d), lambda i: (i, 0)),
        out_shape=jax.ShapeDtypeStruct((rows, d), f32),
        compiler_params=_cp("arbitrary"),
        name="mix_out",
    )(*ys, w, x, g, mods)


def _rope_tables(n_tokens):
    n_rows = n_tokens // GRID_W
    row = jnp.repeat(jnp.arange(n_rows, dtype=f32), GRID_W)
    col = jnp.tile(jnp.arange(GRID_W, dtype=f32), n_rows)
    n_freq = ROT_DIM // 4
    inv = ROPE_BASE ** (-jnp.arange(n_freq, dtype=f32) / n_freq)
    ang = jnp.concatenate([row[:, None] * inv, col[:, None] * inv], axis=-1)
    cos, sin = jnp.cos(ang), jnp.sin(ang)
    return jnp.tile(cos, (1, 4)), jnp.tile(jnp.concatenate([-sin, sin], axis=-1), (1, 2))


def _rope_slab(x, cos, sin):
    lane = lax.broadcasted_iota(jnp.int32, x.shape, 1)
    partner = jnp.where(lane % ROT_DIM < ROT_DIM // 2,
                        pltpu.roll(x, LANES - ROT_DIM // 2, 1), pltpu.roll(x, ROT_DIM // 2, 1))
    return x * cos + partner * sin


def _da_prep_kernel(q_ref, k_ref, v_ref, cos_ref, sin_ref, qo_ref, ko_ref, vo_ref, *, scale):
    cos, sin = cos_ref[...], sin_ref[...]
    for h in range(SEC // LANES):
        sl = slice(h * LANES, (h + 1) * LANES)
        qo_ref[sl, :] = (_rope_slab(q_ref[:, sl].astype(f32), cos, sin) * scale).T.astype(bf16)
        ko_ref[:, sl] = _rope_slab(k_ref[:, sl].astype(f32), cos, sin).astype(bf16)
        vo_ref[sl, :] = v_ref[:, sl].astype(f32).T.astype(bf16)


def _da_prep(px, cos, sin, pos_tiles, tm, scale):
    rows = px.shape[0]
    sec = lambda s: pl.BlockSpec((tm, SEC), lambda i: (i, s))
    tab = pl.BlockSpec((tm, LANES), lambda i: (i % pos_tiles, 0))
    nat = pl.BlockSpec((tm, SEC), lambda i: (i, 0))
    tr = pl.BlockSpec((SEC, tm), lambda i: (0, i))
    return pl.pallas_call(
        functools.partial(_da_prep_kernel, scale=scale),
        grid=(rows // tm,),
        in_specs=[sec(0), sec(1), sec(2), tab, tab],
        out_specs=[tr, nat, tr],
        out_shape=[jax.ShapeDtypeStruct((SEC, rows), bf16), jax.ShapeDtypeStruct((rows, SEC), bf16),
                   jax.ShapeDtypeStruct((SEC, rows), bf16)],
        compiler_params=_cp("arbitrary"),
        name="da_prep",
    )(px, px, px, cos, sin)


def _rms_g(x, g):
    ms = jnp.mean(x * x, axis=-1, keepdims=True)
    return x * lax.rsqrt(ms + NORM_EPS) * g


def _mla_prep_kernel(cq_ref, ckv_ref, gq_ref, gkv_ref, wq_ref, wkv_ref, cos_ref, sin_ref,
                     qo_ref, ko_ref, vo_ref, *, scale):
    cos, sin = cos_ref[...], sin_ref[...]
    n_heads = SEC // HEAD_W
    q = jnp.dot(_rms_g(cq_ref[...].astype(f32), gq_ref[...]).astype(bf16), wq_ref[...],
                preferred_element_type=f32)
    kv = jnp.dot(_rms_g(ckv_ref[:, :MLA_KV_RANK].astype(f32), gkv_ref[...]).astype(bf16), wkv_ref[...],
                 preferred_element_type=f32)
    kr = _rope_slab(ckv_ref[:, MLA_KV_RANK:MLA_KV_RANK + LANES].astype(f32), cos, sin).astype(bf16)
    for h in range(n_heads):
        base = h * MLA_QK
        qo_ref[base:base + LANES, :] = (q[:, base:base + LANES] * scale).T.astype(bf16)
        qo_ref[base + LANES:base + MLA_QK, :] = (
            _rope_slab(q[:, base + LANES:base + MLA_QK], cos, sin) * scale).T.astype(bf16)
        ko_ref[:, base:base + LANES] = kv[:, h * LANES:(h + 1) * LANES].astype(bf16)
        ko_ref[:, base + LANES:base + MLA_QK] = kr
        vo_ref[h * LANES:(h + 1) * LANES, :] = kv[:, (n_heads + h) * LANES:(n_heads + h + 1) * LANES].T.astype(bf16)


def _mla_prep(px, gq, gkv, wq, wkv, cos, sin, pos_tiles, tm, scale):
    rows = px.shape[0]
    n_heads = SEC // HEAD_W
    sec = lambda s: pl.BlockSpec((tm, SEC), lambda i: (i, s))
    tab = pl.BlockSpec((tm, LANES), lambda i: (i % pos_tiles, 0))
    full = lambda a: pl.BlockSpec(a.shape, lambda i: (0,) * a.ndim)
    return pl.pallas_call(
        functools.partial(_mla_prep_kernel, scale=scale),
        grid=(rows // tm,),
        in_specs=[sec(4), sec(5), full(gq), full(gkv), full(wq), full(wkv), tab, tab],
        out_specs=[pl.BlockSpec((n_heads * MLA_QK, tm), lambda i: (0, i)),
                   pl.BlockSpec((tm, n_heads * MLA_QK), lambda i: (i, 0)),
                   pl.BlockSpec((SEC, tm), lambda i: (0, i))],
        out_shape=[jax.ShapeDtypeStruct((n_heads * MLA_QK, rows), bf16),
                   jax.ShapeDtypeStruct((rows, n_heads * MLA_QK), bf16),
                   jax.ShapeDtypeStruct((SEC, rows), bf16)],
        compiler_params=_cp("arbitrary"),
        name="mla_prep",
    )(px, px, gq, gkv, wq, wkv, cos, sin)


def _attn_kernel(*refs, diff, has_x_keys, tk, lambda_init):
    refs = list(refs)
    n_units = (2 if diff else 1) * (refs[0].shape[1] // refs[-1].shape[1])
    p_refs = [refs.pop() for _ in range(n_units)][::-1]
    s_refs = [refs.pop() for _ in range(n_units)][::-1]
    o_ref = refs.pop()
    q_ref, kc_ref, vc_ref = refs[:3]
    refs = refs[3:]
    n_ctx = kc_ref.shape[0]
    chunks = [(kc_ref, 0, 0, n_ctx)]
    if has_x_keys:
        kx_ref, vx_ref = refs[:2]
        refs = refs[2:]
        chunks += [(kx_ref, c * tk, n_ctx + c * tk, tk) for c in range(kx_ref.shape[0] // tk)]
    qt = q_ref[...]
    if diff:
        feat = lax.broadcasted_iota(jnp.int32, qt.shape, 0)
        zero = jnp.zeros_like(qt)
        maps = [jnp.where(feat < ROT_DIM, qt, zero), jnp.where(feat >= ROT_DIM, qt, zero)]
    else:
        maps = [qt]
    tq = s_refs[0].shape[1]
    n_col = qt.shape[1] // tq
    qs = [qm[:, c * tq:(c + 1) * tq] for qm in maps for c in range(n_col)]
    fold = lambda a: a.reshape(a.shape[0] // ACC_ROWS, ACC_ROWS, a.shape[1])

    def scores(u):
        m_acc = jnp.full((ACC_ROWS, tq), -jnp.inf, f32)
        for k_ref, start, row0, size in chunks:
            s = jnp.dot(k_ref[pl.ds(start, size), :], qs[u], preferred_element_type=f32)
            s_refs[u][pl.ds(row0, size), :] = s
            m_acc = jnp.maximum(m_acc, jnp.max(fold(s), axis=0))
        return jnp.max(m_acc, axis=0, keepdims=True)

    def values(u, m_col):
        l_acc = jnp.zeros((ACC_ROWS, tq), f32)
        for _, _, row0, size in chunks:
            p = jnp.exp2(s_refs[u][pl.ds(row0, size), :] - m_col)
            l_acc = l_acc + jnp.sum(fold(p), axis=0)
            p_refs[u][pl.ds(row0, size), :] = p.astype(bf16)
        ot = jnp.dot(vc_ref[...], p_refs[u][pl.ds(0, n_ctx), :], preferred_element_type=f32)
        if has_x_keys:
            ot = ot + jnp.dot(vx_ref[...], p_refs[u][pl.ds(n_ctx, vx_ref.shape[1]), :], preferred_element_type=f32)
        return (ot / jnp.sum(l_acc, axis=0, keepdims=True)).T

    outs = []
    m_prev = scores(0)
    for u in range(1, len(qs)):
        m_next = scores(u)
        outs.append(values(u - 1, m_prev))
        m_prev = m_next
    outs.append(values(len(qs) - 1, m_prev))
    if diff:
        lq1, lk1, lq2, lk2, subln = (r[...] for r in refs)
        lam = (jnp.exp(jnp.sum(lq1 * lk1, axis=-1, keepdims=True))
               - jnp.exp(jnp.sum(lq2 * lk2, axis=-1, keepdims=True)) + lambda_init)
    for c in range(n_col):
        rows = pl.ds(c * tq, tq)
        if diff:
            dlt = outs[c] - lam * outs[n_col + c]
            o_ref[rows, :] = (_rms_g(dlt, subln) * (1.0 - lambda_init)).astype(o_ref.dtype)
        else:
            o_ref[rows, :] = outs[c].astype(o_ref.dtype)


def _attention(qt, kc, vct, kx, vxt, params, n_batch, ctx_len, tq, tk, lambda_init):
    n_heads = SEC // HEAD_W
    dq = qt.shape[0] // n_heads
    lq = qt.shape[1] // n_batch
    nq = lq // tq
    diff = params is not None
    in_specs = [pl.BlockSpec((dq, tq), lambda b, h, i: (h, b * nq + i)),
                pl.BlockSpec((ctx_len, dq), lambda b, h, i: (b, h)),
                pl.BlockSpec((HEAD_W, ctx_len), lambda b, h, i: (h, b))]
    args = [qt, kc, vct]
    n_keys = ctx_len
    if kx is not None:
        lx = kx.shape[0] // n_batch
        n_keys += lx
        in_specs += [pl.BlockSpec((lx, dq), lambda b, h, i: (b, h)),
                     pl.BlockSpec((HEAD_W, lx), lambda b, h, i: (h, b))]
        args += [kx, vxt]
    if diff:
        in_specs += [pl.BlockSpec(p.shape, lambda b, h, i: (0, 0)) for p in params]
        args += list(params)
    unit = min(tq, ATTN_UNIT)
    n_units = (2 if diff else 1) * (tq // unit)
    return pl.pallas_call(
        functools.partial(_attn_kernel, diff=diff, has_x_keys=kx is not None, tk=tk, lambda_init=lambda_init),
        grid=(n_batch, n_heads, nq),
        in_specs=in_specs,
        out_specs=pl.BlockSpec((tq, HEAD_W), lambda b, h, i: (b * nq + i, h)),
        out_shape=jax.ShapeDtypeStruct((qt.shape[1], n_heads * HEAD_W), bf16),
        scratch_shapes=[pltpu.VMEM((n_keys, unit), f32)] * n_units + [pltpu.VMEM((n_keys, unit), bf16)] * n_units,
        compiler_params=_cp("arbitrary", "arbitrary", "arbitrary"),
        name="attention",
    )(*args)


def _s5_kernel(uc_ref, ux_ref, lre_ref, lim_ref, lst_ref, btr_ref, bti_ref, ctr_ref, cti_ref,
               yc_ref, yx_ref, wb_re, wb_im, wc_re, wc_im, a_re, a_im, h_re, h_im, s_re, s_im):
    d = pl.program_id(0)
    c = pl.program_id(1)
    n_batch, t_len, width = uc_ref.shape
    n_state = lre_ref.shape[1]
    half = n_state // 2
    n_groups = width // S5_GROUP
    n_blk = half // LANES

    @pl.when(c == 0)
    def _():
        lre, lim = lre_ref[...], lim_ref[...]
        step = jnp.exp(lst_ref[...])
        mag = jnp.exp(lre * step)
        are = mag * jnp.cos(lim * step)
        aim = mag * jnp.sin(lim * step)
        den = lre * lre + lim * lim
        fre = ((are - 1.0) * lre + aim * lim) / den
        fim = (aim * lre - (are - 1.0) * lim) / den
        row = lax.broadcasted_iota(jnp.int32, (width, n_state), 0) // S5_GROUP
        col = lax.broadcasted_iota(jnp.int32, (width, n_state), 1) // S5_STATE
        blk = row == col
        tile = lambda r: jnp.broadcast_to(r[...][None], (n_groups, S5_GROUP, n_state)).reshape(width, n_state)
        btr, bti = tile(btr_ref), tile(bti_ref)
        wb_re[...] = jnp.where(blk, fre * btr - fim * bti, 0.0).astype(bf16)
        wb_im[...] = jnp.where(blk, fre * bti + fim * btr, 0.0).astype(bf16)
        wc_re[...] = jnp.where(blk, tile(ctr_ref), 0.0).astype(bf16)
        wc_im[...] = jnp.where(blk, -tile(cti_ref), 0.0).astype(bf16)
        for hf in range(2):
            rows = slice(hf * n_batch, (hf + 1) * n_batch)
            for j in range(n_blk):
                cols = slice(hf * half + j * LANES, hf * half + (j + 1) * LANES)
                a_re[j, rows, :] = jnp.broadcast_to(are[:, cols], (n_batch, LANES))
                a_im[j, rows, :] = jnp.broadcast_to(aim[:, cols], (n_batch, LANES))
        h_re[...] = jnp.zeros_like(h_re)
        h_im[...] = jnp.zeros_like(h_im)

    pitch = s_re.shape[1] // (2 * n_batch)
    kb = LANES
    sb = kb // S5_GROUP * S5_STATE
    nt = (((1,), (1,)), ((), ()))

    def slab(col0, b):
        hf, rem = divmod(col0, half)
        return rem // LANES, pl.ds((hf * n_batch + b) * pitch, t_len)

    def run(u_ref, y_ref):
        for b in range(n_batch):
            u = u_ref[b]
            for bb in range(width // kb):
                ub = u[:, bb * kb:(bb + 1) * kb]
                bre = jnp.dot(ub, wb_re[bb * kb:(bb + 1) * kb, bb * sb:(bb + 1) * sb], preferred_element_type=f32)
                bim = jnp.dot(ub, wb_im[bb * kb:(bb + 1) * kb, bb * sb:(bb + 1) * sb], preferred_element_type=f32)
                for jj in range(sb // LANES):
                    j, rows = slab(bb * sb + jj * LANES, b)
                    s_re[j, rows, :] = bre[:, jj * LANES:(jj + 1) * LANES]
                    s_im[j, rows, :] = bim[:, jj * LANES:(jj + 1) * LANES]

        ar = [a_re[j] for j in range(n_blk)]
        ai = [a_im[j] for j in range(n_blk)]

        def step(k, carry):
            hr, hi = carry
            t = k + d * (t_len - 1 - 2 * k)
            rows = pl.ds(t, 2 * n_batch, stride=pitch)
            nr, ni = [], []
            for j in range(n_blk):
                nr.append(ar[j] * hr[j] - ai[j] * hi[j] + s_re[j, rows, :])
                ni.append(ar[j] * hi[j] + ai[j] * hr[j] + s_im[j, rows, :])
                s_re[j, rows, :] = nr[j]
                s_im[j, rows, :] = ni[j]
            return nr, ni

        init = ([h_re[j] for j in range(n_blk)], [h_im[j] for j in range(n_blk)])
        hr, hi = lax.fori_loop(0, t_len, step, init)
        for j in range(n_blk):
            h_re[j] = hr[j]
            h_im[j] = hi[j]

        for b in range(n_batch):
            ys = []
            for bb in range(width // kb):
                slabs = [slab(bb * sb + jj * LANES, b) for jj in range(sb // LANES)]
                sr = jnp.concatenate([s_re[j, rows, :] for j, rows in slabs], axis=1).astype(bf16)
                si = jnp.concatenate([s_im[j, rows, :] for j, rows in slabs], axis=1).astype(bf16)
                ys.append(
                    lax.dot_general(sr, wc_re[bb * kb:(bb + 1) * kb, bb * sb:(bb + 1) * sb], nt,
                                    preferred_element_type=f32)
                    + lax.dot_general(si, wc_im[bb * kb:(bb + 1) * kb, bb * sb:(bb + 1) * sb], nt,
                                      preferred_element_type=f32))
            y_ref[b] = jnp.concatenate(ys, axis=1)

    @pl.when(c == 0)
    def _():
        run(uc_ref, yc_ref)

    @pl.when(c > 0)
    def _():
        run(ux_ref, yx_ref)


def _s5_scan(px_c, px_x, lam_re, lam_im, log_step, bt_re, bt_im, ct_re, ct_im, n_batch):
    t_len = px_c.shape[1]
    seq = px_x.shape[1]
    assert seq % t_len == 0
    nch = seq // t_len
    n_state = lam_re.shape[-1]
    half = n_state // 2
    pitch = t_len + SUBLANES
    assert t_len % (2 * SUBLANES) == 0

    def x_chunk(d, c):
        return jnp.where(d == 0, jnp.maximum(c - 1, 0), jnp.minimum(nch - c, nch - 1))

    vec = pl.BlockSpec((None, 1, n_state), lambda d, c: (d, 0, 0))
    mat = pl.BlockSpec((None, S5_GROUP, n_state), lambda d, c: (d, 0, 0))
    return pl.pallas_call(
        _s5_kernel,
        grid=(2, nch + 1),
        in_specs=[pl.BlockSpec((n_batch, t_len, SEC), lambda d, c: (0, 0, 3)),
                  pl.BlockSpec((n_batch, t_len, SEC), lambda d, c: (0, x_chunk(d, c), 3)),
                  vec, vec, vec, mat, mat, mat, mat],
        out_specs=[pl.BlockSpec((None, n_batch, t_len, SEC), lambda d, c: (d, 0, 0, 0)),
                   pl.BlockSpec((None, n_batch, t_len, SEC), lambda d, c: (d, 0, x_chunk(d, c), 0))],
        out_shape=[jax.ShapeDtypeStruct((2, n_batch, t_len, SEC), f32),
                   jax.ShapeDtypeStruct((2, n_batch, seq, SEC), f32)],
        scratch_shapes=[pltpu.VMEM((SEC, n_state), bf16)] * 4
        + [pltpu.VMEM((half // LANES, 2 * n_batch, LANES), f32)] * 4
        + [pltpu.VMEM((half // LANES, 2 * n_batch * pitch, LANES), f32)] * 2,
        compiler_params=_cp("arbitrary", "arbitrary"),
        name="s5_scan",
    )(px_c, px_x, lam_re, lam_im, log_step, bt_re, bt_im, ct_re, ct_im)


def _s5_glu_kernel(yf_ref, yb_ref, u_ref, d_ref, w_ref, o_ref):
    y = d_ref[...] * u_ref[...].astype(f32) + yf_ref[...] + yb_ref[...]
    g = jax.nn.gelu(y)
    z = jnp.dot(g.astype(bf16), w_ref[...], preferred_element_type=f32)
    o_ref[...] = (g * jax.nn.sigmoid(z)).astype(o_ref.dtype)


def _s5_glu(y, px, s5_d, w_glu, tm):
    rows = px.shape[0]
    return pl.pallas_call(
        _s5_glu_kernel,
        grid=(rows // tm,),
        in_specs=[pl.BlockSpec((None, tm, SEC), lambda i: (0, i, 0)),
                  pl.BlockSpec((None, tm, SEC), lambda i: (1, i, 0)),
                  pl.BlockSpec((tm, SEC), lambda i: (i, 3)),
                  pl.BlockSpec((1, SEC), lambda i: (0, 0)),
                  pl.BlockSpec((SEC, SEC), lambda i: (0, 0))],
        out_specs=pl.BlockSpec((tm, SEC), lambda i: (i, 0)),
        out_shape=jax.ShapeDtypeStruct((rows, SEC), bf16),
        compiler_params=_cp("arbitrary"),
        name="s5_glu",
    )(y, y, px, s5_d, w_glu)


CONV_HALO = 16


def _conv_kernel(vp_ref, vc_ref, vn_ref, gp_ref, gc_ref, gn_ref, w_ref, b_ref, lg_ref, lb_ref, o_ref, ext_ref):
    j = pl.program_id(1)
    tc = vc_ref.shape[0]
    glu = lambda v, g: v[...].astype(f32) * jax.nn.sigmoid(g[...].astype(f32))
    ext_ref[0, pl.ds(0, CONV_HALO), :] = jnp.where(j > 0, glu(vp_ref, gp_ref), 0.0)
    ext_ref[0, pl.ds(CONV_HALO, tc), :] = glu(vc_ref, gc_ref)
    ext_ref[0, pl.ds(CONV_HALO + tc, CONV_HALO), :] = jnp.where(j < pl.num_programs(1) - 1, glu(vn_ref, gn_ref), 0.0)
    n_keep = tc + 2 * CONV_HALO - SUBLANES
    for s in range(1, SUBLANES):
        ext_ref[s, pl.ds(0, n_keep), :] = ext_ref[0, pl.ds(s, n_keep), :]
    y = jnp.zeros((tc, vc_ref.shape[1]), f32)
    for k in range(CONV_K):
        off = CONV_HALO - CONV_K // 2 + k
        y += w_ref[pl.ds(k, 1), :] * ext_ref[off % SUBLANES, pl.ds(off - off % SUBLANES, tc), :]
    y = y + b_ref[...]
    mu = jnp.mean(y, axis=-1, keepdims=True)
    var = jnp.mean(jnp.square(y - mu), axis=-1, keepdims=True)
    z = (y - mu) * lax.rsqrt(var + NORM_EPS) * lg_ref[...] + lb_ref[...]
    o_ref[...] = jax.nn.silu(z).astype(o_ref.dtype)


def _conv(px, conv_w, conv_b, ln_g, ln_b, n_batch, tc):
    rows = px.shape[0]
    seq = rows // n_batch
    nt = seq // tc
    hpt = tc // CONV_HALO
    cur = lambda s: pl.BlockSpec((tc, SEC), lambda b, j: (b * nt + j, s))
    prev = lambda s: pl.BlockSpec((CONV_HALO, SEC), lambda b, j: (jnp.maximum((b * nt + j) * hpt - 1, 0), s))
    nxt = lambda s: pl.BlockSpec(
        (CONV_HALO, SEC), lambda b, j: (jnp.minimum((b * nt + j + 1) * hpt, rows // CONV_HALO - 1), s))
    full = lambda a: pl.BlockSpec(a.shape, lambda b, j: (0,) * a.ndim)
    return pl.pallas_call(
        _conv_kernel,
        grid=(n_batch, nt),
        in_specs=[prev(6), cur(6), nxt(6), prev(7), cur(7), nxt(7),
                  full(conv_w), full(conv_b), full(ln_g), full(ln_b)],
        out_specs=pl.BlockSpec((tc, SEC), lambda b, j: (b * nt + j, 0)),
        out_shape=jax.ShapeDtypeStruct((rows, SEC), bf16),
        scratch_shapes=[pltpu.VMEM((SUBLANES, tc + 2 * CONV_HALO, SEC), f32)],
        compiler_params=_cp("arbitrary", "arbitrary"),
        name="conv",
    )(px, px, px, px, px, px, conv_w, conv_b, ln_g, ln_b)


def kernel(x, c, ctx, c_ctx, w_ada, b_ada, norm_pre_mix, norm_post_mix, norm_pre_ffn, norm_post_ffn, w_in, w_out, da_lam_q1, da_lam_k1, da_lam_q2, da_lam_k2, da_subln, s5_lam_re, s5_lam_im, s5_log_step, s5_b_re, s5_b_im, s5_c_re, s5_c_im, s5_d, s5_w_glu, mla_q_norm, mla_kv_norm, mla_w_uq, mla_w_ukv, conv_w, conv_b, conv_ln_g, conv_ln_b, w_ffn_in, w_ffn_out):
    n_batch, seq, d = x.shape
    ctx_len = ctx.shape[1]
    depth = w_in.shape[0]
    n_state = s5_lam_re.shape[2] * s5_lam_re.shape[3]
    mla_end = 4 * SEC + SEC + MLA_KV_RANK + ROT_DIM
    assert d == 4 * SEC and w_in.shape[2] == mla_end + 2 * SEC

    da_scale = ROT_DIM ** -0.5 * math.log2(math.e)
    mla_scale = (HEAD_W + ROT_DIM) ** -0.5 * math.log2(math.e)

    tm_x = _row_tile(seq, 1024)
    tm_c = _row_tile(n_batch * ctx_len, 1024)
    tpb = seq // tm_x
    grp_x = lambda i: i // tpb
    grp_c = lambda i: n_batch
    tq_da = _row_tile(seq, 512)
    tq_mla = _row_tile(seq, 1024)
    tk = _row_tile(seq, 512)

    n_grp = -(-(n_batch + 1) // SUBLANES) * SUBLANES
    cc = jnp.concatenate([c, c_ctx[None], jnp.zeros((n_grp - n_batch - 1, d), f32)], axis=0)
    mods = _ada(cc, w_ada, b_ada).reshape(depth, n_grp, 1, 6 * d)

    cos_x, sin_x = _rope_tables(seq)
    cos_c, sin_c = jnp.ones((tm_c, LANES), f32), jnp.zeros((tm_c, LANES), f32)

    w_in_p = jnp.concatenate([w_in[:, :, :mla_end], jnp.zeros((depth, d, N_SEC * SEC - w_in.shape[2]), f32),
                              w_in[:, :, mla_end:]], axis=2).astype(bf16)
    w_out_b = w_out.astype(bf16)
    w_ffn_in_b = w_ffn_in.astype(bf16)
    w_ffn_out_b = w_ffn_out.astype(bf16)
    w_glu_b = s5_w_glu.astype(bf16)
    n_mla = SEC // HEAD_W
    wq = mla_w_uq.reshape(depth, SEC, n_mla, HEAD_W + ROT_DIM)
    wq = jnp.pad(wq, ((0, 0), (0, 0), (0, 0), (0, MLA_QK - HEAD_W - ROT_DIM))).reshape(depth, SEC, n_mla * MLA_QK)
    wq = wq.astype(bf16)
    wkv = mla_w_ukv.reshape(depth, MLA_KV_RANK, n_mla, 2, HEAD_W).transpose(0, 1, 3, 2, 4)
    wkv = wkv.reshape(depth, MLA_KV_RANK, 2 * n_mla * HEAD_W).astype(bf16)

    flat = lambda a: a.reshape(depth, 2, 1, n_state)
    s5_lre, s5_lim = flat(s5_lam_re), flat(s5_lam_im)
    s5_lst = flat(jnp.broadcast_to(s5_log_step[..., None], s5_lam_re.shape))
    s5_btr = s5_b_re.transpose(0, 1, 4, 2, 3).reshape(depth, 2, S5_GROUP, n_state)
    s5_bti = s5_b_im.transpose(0, 1, 4, 2, 3).reshape(depth, 2, S5_GROUP, n_state)
    s5_ctr = s5_c_re.transpose(0, 1, 3, 2, 4).reshape(depth, 2, S5_GROUP, n_state)
    s5_cti = s5_c_im.transpose(0, 1, 3, 2, 4).reshape(depth, 2, S5_GROUP, n_state)

    row = lambda a, l: a[l][None]
    conv_w_p = jnp.pad(conv_w, ((0, 0), (0, 1), (0, 0)))

    xs = x.reshape(n_batch * seq, d)
    cs = ctx.reshape(n_batch * ctx_len, d)
    for l in range(depth):
        last = l == depth - 1
        lambda_init = 0.8 - 0.6 * math.exp(-0.3 * l)
        g_pre = row(norm_pre_mix, l)

        px = _in_proj(xs, g_pre, mods, l, grp_x, w_in_p, tm_x, 4 * SEC)
        pc = _in_proj(cs, g_pre, mods, l, grp_c, w_in_p, tm_c, 4 * SEC)

        qx, kx, vx = _da_prep(px, cos_x, sin_x, tpb, tm_x, da_scale)
        qc, kc, vc = _da_prep(pc, cos_c, sin_c, 1, tm_c, da_scale)
        da_params = (row(da_lam_q1, l), row(da_lam_k1, l), row(da_lam_q2, l), row(da_lam_k2, l), row(da_subln, l))
        y_da_x = _attention(qx, kc, vc, kx, vx, da_params, n_batch, ctx_len, tq_da, tk, lambda_init)

        mla_w = (row(mla_q_norm, l), row(mla_kv_norm, l), wq[l], wkv[l])
        mqx, mkx, mvx = _mla_prep(px, *mla_w, cos_x, sin_x, tpb, tm_x, mla_scale)
        mqc, mkc, mvc = _mla_prep(pc, *mla_w, cos_c, sin_c, 1, tm_c, mla_scale)
        y_mla_x = _attention(mqx, mkc, mvc, mkx, mvx, None, n_batch, ctx_len, tq_mla, tk, lambda_init)

        ys_c, ys_x = _s5_scan(pc.reshape(n_batch, ctx_len, -1), px.reshape(n_batch, seq, -1),
                              s5_lre[l], s5_lim[l], s5_lst[l], s5_btr[l], s5_bti[l], s5_ctr[l], s5_cti[l], n_batch)
        y_s5_x = _s5_glu(ys_x.reshape(2, n_batch * seq, SEC), px, row(s5_d, l), w_glu_b[l], tm_x)

        cv = (conv_w_p[l], row(conv_b, l), row(conv_ln_g, l), row(conv_ln_b, l))
        y_cv_x = _conv(px, *cv, n_batch, _row_tile(seq, 512))

        g_post, g_pre_f, g_post_f = row(norm_post_mix, l), row(norm_pre_ffn, l), row(norm_post_ffn, l)
        tm_o = _row_tile(tm_x, 512)
        x1 = _mix_out((y_da_x, y_s5_x, y_mla_x, y_cv_x), w_out_b, xs, g_post, mods, l,
                      lambda i: i // (seq // tm_o), tm_o)
        xs = _ffn(x1, g_pre_f, g_post_f, mods, l, lambda i: i // (seq // tm_o), w_ffn_in_b, w_ffn_out_b, tm_o, SEC)

        if not last:
            y_da_c = _attention(qc, kc, vc, None, None, da_params, n_batch, ctx_len, ctx_len, tk, lambda_init)
            y_mla_c = _attention(mqc, mkc, mvc, None, None, None, n_batch, ctx_len, ctx_len, tk, lambda_init)
            y_s5_c = _s5_glu(ys_c.reshape(2, n_batch * ctx_len, SEC), pc, row(s5_d, l), w_glu_b[l], tm_c)
            y_cv_c = _conv(pc, *cv, n_batch, ctx_len)
            tm_oc = _row_tile(tm_c, 512)
            c1 = _mix_out((y_da_c, y_s5_c, y_mla_c, y_cv_c), w_out_b, cs, g_post, mods, l, grp_c, tm_oc)
            cs = _ffn(c1, g_pre_f, g_post_f, mods, l, grp_c, w_ffn_in_b, w_ffn_out_b, tm_oc, SEC)
    return xs.reshape(n_batch, seq, d)
```

```python
import functools
import math

import jax
import jax.numpy as jnp
from jax import lax
from jax.experimental import pallas as pl
from jax.experimental.pallas import tpu as pltpu

f32 = jnp.float32
bf16 = jnp.bfloat16

NORM_EPS = 1e-6
ROPE_BASE = 10000.0
GRID_W = 64
ROT_DIM = 64
CONV_K = 31
S5_GROUP = 16
S5_STATE = 64

LANES = 128
SUBLANES = 8
SEC = 512
N_SEC = 8
HEAD_W = 128
MLA_QK = 256
MLA_KV_RANK = 256
ATTN_UNIT = 256
ACC_ROWS = SUBLANES
V7X_VMEM_BYTES = 64 * 1024 * 1024
VMEM_LIMIT = V7X_VMEM_BYTES * 7 // 8


def _cp(*sem):
    return pltpu.CompilerParams(dimension_semantics=sem, vmem_limit_bytes=VMEM_LIMIT)


def _row_tile(rows, want):
    t = min(rows, want)
    assert rows % t == 0, (rows, want)
    return t


def _ada_kernel(cc_ref, w_ref, b_ref, o_ref):
    s = jax.nn.silu(cc_ref[...]).astype(bf16)
    o_ref[...] = jnp.dot(s, w_ref[...].astype(bf16), preferred_element_type=f32) + b_ref[...]


def _ada(cc, w_ada, b_ada):
    depth, d, n = w_ada.shape
    g = cc.shape[0]
    tn = 512
    return pl.pallas_call(
        _ada_kernel,
        grid=(depth, n // tn),
        in_specs=[pl.BlockSpec((g, d), lambda l, j: (0, 0)),
                  pl.BlockSpec((None, d, tn), lambda l, j: (l, 0, j)),
                  pl.BlockSpec((None, 1, tn), lambda l, j: (l, 0, j))],
        out_specs=pl.BlockSpec((None, g, tn), lambda l, j: (l, 0, j)),
        out_shape=jax.ShapeDtypeStruct((depth, g, n), f32),
        compiler_params=_cp("arbitrary", "arbitrary"),
        name="ada",
    )(cc, w_ada, b_ada.reshape(depth, 1, n))


def _row_chunks(n_rows, chunk, body):
    def step(i, carry):
        body(pl.ds(pl.multiple_of(i * chunk, chunk), chunk))
        return carry

    lax.fori_loop(0, n_rows // chunk, step, 0, unroll=8)


def _norm_mod_rows(x_ref, g_ref, sh_ref, sc_ref, h_ref):
    g, shift, scale1 = g_ref[...], sh_ref[...], 1.0 + sc_ref[...]

    def body(rows):
        x = x_ref[rows, :]
        ms = jnp.mean(x * x, axis=-1, keepdims=True)
        h_ref[rows, :] = ((x * lax.rsqrt(ms + NORM_EPS) * g) * scale1 + shift).astype(bf16)

    _row_chunks(x_ref.shape[0], 2 * SUBLANES, body)


def _residual_rows(acc_ref, x_ref, g_ref, gate_ref, o_ref):
    g, gate = g_ref[...], gate_ref[...]

    def body(rows):
        acc = acc_ref[rows, :]
        ms = jnp.mean(acc * acc, axis=-1, keepdims=True)
        o_ref[rows, :] = x_ref[rows, :] + gate * (acc * lax.rsqrt(ms + NORM_EPS) * g)

    _row_chunks(x_ref.shape[0], SUBLANES, body)


def _in_proj_kernel(x_ref, g_ref, sh_ref, sc_ref, w_ref, o_ref, h_ref):
    @pl.when(pl.program_id(1) == 0)
    def _():
        _norm_mod_rows(x_ref, g_ref, sh_ref, sc_ref, h_ref)

    o_ref[...] = jnp.dot(h_ref[...], w_ref[...], preferred_element_type=f32).astype(o_ref.dtype)


def _ffn_kernel(x_ref, g_ref, sh_ref, sc_ref, wg_ref, wu_ref, wo_ref, g2_ref, gate_ref, o_ref, h_ref, acc_ref):
    j = pl.program_id(1)

    @pl.when(j == 0)
    def _():
        _norm_mod_rows(x_ref, g_ref, sh_ref, sc_ref, h_ref)
        acc_ref[...] = jnp.zeros_like(acc_ref)

    h = h_ref[...]
    a = jnp.dot(h, wg_ref[...], preferred_element_type=f32)
    b = jnp.dot(h, wu_ref[...], preferred_element_type=f32)
    act = (jax.nn.silu(a) * b).astype(bf16)
    acc_ref[...] += jnp.dot(act, wo_ref[...], preferred_element_type=f32)

    @pl.when(j == pl.num_programs(1) - 1)
    def _():
        _residual_rows(acc_ref, x_ref, g2_ref, gate_ref, o_ref)


def _mod_spec(l, grp, k, d):
    return pl.BlockSpec((None, None, 1, d), lambda i, j: (l, grp(i), 0, k))


def _in_proj(x, g, mods, l, grp, w, tm, tn):
    rows, d = x.shape
    n = w.shape[2]
    return pl.pallas_call(
        _in_proj_kernel,
        grid=(rows // tm, n // tn),
        in_specs=[pl.BlockSpec((tm, d), lambda i, j: (i, 0)),
                  pl.BlockSpec((1, d), lambda i, j: (0, 0)),
                  _mod_spec(l, grp, 0, d), _mod_spec(l, grp, 1, d),
                  pl.BlockSpec((None, d, tn), lambda i, j: (l, 0, j))],
        out_specs=pl.BlockSpec((tm, tn), lambda i, j: (i, j)),
        out_shape=jax.ShapeDtypeStruct((rows, n), bf16),
        scratch_shapes=[pltpu.VMEM((tm, d), bf16)],
        compiler_params=_cp("arbitrary", "arbitrary"),
        name="in_proj",
    )(x, g, mods, mods, w)


def _ffn(x, g_pre, g_post, mods, l, grp, w_in, w_out, tm, tn):
    rows, d = x.shape
    dff = w_out.shape[1]
    nj = dff // tn
    return pl.pallas_call(
        _ffn_kernel,
        grid=(rows // tm, nj),
        in_specs=[pl.BlockSpec((tm, d), lambda i, j: (i, 0)),
                  pl.BlockSpec((1, d), lambda i, j: (0, 0)),
                  _mod_spec(l, grp, 3, d), _mod_spec(l, grp, 4, d),
                  pl.BlockSpec((None, d, tn), lambda i, j: (l, 0, j)),
                  pl.BlockSpec((None, d, tn), lambda i, j: (l, 0, j + nj)),
                  pl.BlockSpec((None, tn, d), lambda i, j: (l, j, 0)),
                  pl.BlockSpec((1, d), lambda i, j: (0, 0)),
                  _mod_spec(l, grp, 5, d)],
        out_specs=pl.BlockSpec((tm, d), lambda i, j: (i, 0)),
        out_shape=jax.ShapeDtypeStruct((rows, d), f32),
        scratch_shapes=[pltpu.VMEM((tm, d), bf16), pltpu.VMEM((tm, d), f32)],
        compiler_params=_cp("arbitrary", "arbitrary"),
        name="ffn",
    )(x, g_pre, mods, mods, w_in, w_in, w_out, g_post, mods)


def _mix_out_kernel(y0_ref, y1_ref, y2_ref, y3_ref, w_ref, x_ref, g_ref, gate_ref, o_ref):
    tk = y0_ref.shape[1]
    acc = None
    for s, y_ref in enumerate((y0_ref, y1_ref, y2_ref, y3_ref)):
        part = jnp.dot(y_ref[...], w_ref[s * tk:(s + 1) * tk, :], preferred_element_type=f32)
        acc = part if acc is None else acc + part
    ms = jnp.mean(acc * acc, axis=-1, keepdims=True)
    o_ref[...] = x_ref[...] + gate_ref[...] * (acc * lax.rsqrt(ms + NORM_EPS) * g_ref[...])


def _mix_out(ys, w, x, g, mods, l, grp, tm):
    rows, d = x.shape
    tk = ys[0].shape[1]
    mod = _mod_spec(l, grp, 2, d)
    return pl.pallas_call(
        _mix_out_kernel,
        grid=(rows // tm,),
        in_specs=[pl.BlockSpec((tm, tk), lambda i: (i, 0))] * len(ys) + [
            pl.BlockSpec((None, len(ys) * tk, d), lambda i: (l, 0, 0)),
            pl.BlockSpec((tm, d), lambda i: (i, 0)),
            pl.BlockSpec((1, d), lambda i: (0, 0)),
            pl.BlockSpec(mod.block_shape, lambda i: mod.index_map(i, 0))],
        out_specs=pl.BlockSpec((tm, d), lambda i: (i, 0)),
        out_shape=jax.ShapeDtypeStruct((rows, d), f32),
        compiler_params=_cp("arbitrary"),
        name="mix_out",
    )(*ys, w, x, g, mods)


def _rope_tables(n_tokens):
    n_rows = n_tokens // GRID_W
    row = jnp.repeat(jnp.arange(n_rows, dtype=f32), GRID_W)
    col = jnp.tile(jnp.arange(GRID_W, dtype=f32), n_rows)
    n_freq = ROT_DIM // 4
    inv = ROPE_BASE ** (-jnp.arange(n_freq, dtype=f32) / n_freq)
    ang = jnp.concatenate([row[:, None] * inv, col[:, None] * inv], axis=-1)
    cos, sin = jnp.cos(ang), jnp.sin(ang)
    return jnp.tile(cos, (1, 4)), jnp.tile(jnp.concatenate([-sin, sin], axis=-1), (1, 2))


def _rope_slab(x, cos, sin):
    lane = lax.broadcasted_iota(jnp.int32, x.shape, 1)
    partner = jnp.where(lane % ROT_DIM < ROT_DIM // 2,
                        pltpu.roll(x, LANES - ROT_DIM // 2, 1), pltpu.roll(x, ROT_DIM // 2, 1))
    return x * cos + partner * sin


def _da_prep_kernel(q_ref, k_ref, v_ref, cos_ref, sin_ref, qo_ref, ko_ref, vo_ref, *, scale):
    cos, sin = cos_ref[...], sin_ref[...]
    for h in range(SEC // LANES):
        sl = slice(h * LANES, (h + 1) * LANES)
        qo_ref[sl, :] = (_rope_slab(q_ref[:, sl].astype(f32), cos, sin) * scale).T.astype(bf16)
        ko_ref[:, sl] = _rope_slab(k_ref[:, sl].astype(f32), cos, sin).astype(bf16)
        vo_ref[sl, :] = v_ref[:, sl].astype(f32).T.astype(bf16)


def _da_prep(px, cos, sin, pos_tiles, tm, scale):
    rows = px.shape[0]
    sec = lambda s: pl.BlockSpec((tm, SEC), lambda i: (i, s))
    tab = pl.BlockSpec((tm, LANES), lambda i: (i % pos_tiles, 0))
    nat = pl.BlockSpec((tm, SEC), lambda i: (i, 0))
    tr = pl.BlockSpec((SEC, tm), lambda i: (0, i))
    return pl.pallas_call(
        functools.partial(_da_prep_kernel, scale=scale),
        grid=(rows // tm,),
        in_specs=[sec(0), sec(1), sec(2), tab, tab],
        out_specs=[tr, nat, tr],
        out_shape=[jax.ShapeDtypeStruct((SEC, rows), bf16), jax.ShapeDtypeStruct((rows, SEC), bf16),
                   jax.ShapeDtypeStruct((SEC, rows), bf16)],
        compiler_params=_cp("arbitrary"),
        name="da_prep",
    )(px, px, px, cos, sin)


def _rms_g(x, g):
    ms = jnp.mean(x * x, axis=-1, keepdims=True)
    return x * lax.rsqrt(ms + NORM_EPS) * g


def _mla_prep_kernel(cq_ref, ckv_ref, gq_ref, gkv_ref, wq_ref, wkv_ref, cos_ref, sin_ref,
                     qo_ref, ko_ref, vo_ref, *, scale):
    cos, sin = cos_ref[...], sin_ref[...]
    n_heads = SEC // HEAD_W
    q = jnp.dot(_rms_g(cq_ref[...].astype(f32), gq_ref[...]).astype(bf16), wq_ref[...],
                preferred_element_type=f32)
    kv = jnp.dot(_rms_g(ckv_ref[:, :MLA_KV_RANK].astype(f32), gkv_ref[...]).astype(bf16), wkv_ref[...],
                 preferred_element_type=f32)
    kr = _rope_slab(ckv_ref[:, MLA_KV_RANK:MLA_KV_RANK + LANES].astype(f32), cos, sin).astype(bf16)
    for h in range(n_heads):
        base = h * MLA_QK
        qo_ref[base:base + LANES, :] = (q[:, base:base + LANES] * scale).T.astype(bf16)
        qo_ref[base + LANES:base + MLA_QK, :] = (
            _rope_slab(q[:, base + LANES:base + MLA_QK], cos, sin) * scale).T.astype(bf16)
        ko_ref[:, base:base + LANES] = kv[:, h * LANES:(h + 1) * LANES].astype(bf16)
        ko_ref[:, base + LANES:base + MLA_QK] = kr
        vo_ref[h * LANES:(h + 1) * LANES, :] = kv[:, (n_heads + h) * LANES:(n_heads + h + 1) * LANES].T.astype(bf16)


def _mla_prep(px, gq, gkv, wq, wkv, cos, sin, pos_tiles, tm, scale):
    rows = px.shape[0]
    n_heads = SEC // HEAD_W
    sec = lambda s: pl.BlockSpec((tm, SEC), lambda i: (i, s))
    tab = pl.BlockSpec((tm, LANES), lambda i: (i % pos_tiles, 0))
    full = lambda a: pl.BlockSpec(a.shape, lambda i: (0,) * a.ndim)
    return pl.pallas_call(
        functools.partial(_mla_prep_kernel, scale=scale),
        grid=(rows // tm,),
        in_specs=[sec(4), sec(5), full(gq), full(gkv), full(wq), full(wkv), tab, tab],
        out_specs=[pl.BlockSpec((n_heads * MLA_QK, tm), lambda i: (0, i)),
                   pl.BlockSpec((tm, n_heads * MLA_QK), lambda i: (i, 0)),
                   pl.BlockSpec((SEC, tm), lambda i: (0, i))],
        out_shape=[jax.ShapeDtypeStruct((n_heads * MLA_QK, rows), bf16),
                   jax.ShapeDtypeStruct((rows, n_heads * MLA_QK), bf16),
                   jax.ShapeDtypeStruct((SEC, rows), bf16)],
        compiler_params=_cp("arbitrary"),
        name="mla_prep",
    )(px, px, gq, gkv, wq, wkv, cos, sin)


def _attn_kernel(*refs, diff, has_x_keys, tk, lambda_init):
    refs = list(refs)
    n_units = (2 if diff else 1) * (refs[0].shape[1] // refs[-1].shape[1])
    p_refs = [refs.pop() for _ in range(n_units)][::-1]
    s_refs = [refs.pop() for _ in range(n_units)][::-1]
    o_ref = refs.pop()
    q_ref, kc_ref, vc_ref = refs[:3]
    refs = refs[3:]
    n_ctx = kc_ref.shape[0]
    chunks = [(kc_ref, 0, 0, n_ctx)]
    if has_x_keys:
        kx_ref, vx_ref = refs[:2]
        refs = refs[2:]
        chunks += [(kx_ref, c * tk, n_ctx + c * tk, tk) for c in range(kx_ref.shape[0] // tk)]
    qt = q_ref[...]
    if diff:
        feat = lax.broadcasted_iota(jnp.int32, qt.shape, 0)
        zero = jnp.zeros_like(qt)
        maps = [jnp.where(feat < ROT_DIM, qt, zero), jnp.where(feat >= ROT_DIM, qt, zero)]
    else:
        maps = [qt]
    tq = s_refs[0].shape[1]
    n_col = qt.shape[1] // tq
    qs = [qm[:, c * tq:(c + 1) * tq] for qm in maps for c in range(n_col)]
    fold = lambda a: a.reshape(a.shape[0] // ACC_ROWS, ACC_ROWS, a.shape[1])

    def scores(u):
        m_acc = jnp.full((ACC_ROWS, tq), -jnp.inf, f32)
        for k_ref, start, row0, size in chunks:
            s = jnp.dot(k_ref[pl.ds(start, size), :], qs[u], preferred_element_type=f32)
            s_refs[u][pl.ds(row0, size), :] = s
            m_acc = jnp.maximum(m_acc, jnp.max(fold(s), axis=0))
        return jnp.max(m_acc, axis=0, keepdims=True)

    def values(u, m_col):
        l_acc = jnp.zeros((ACC_ROWS, tq), f32)
        for _, _, row0, size in chunks:
            p = jnp.exp2(s_refs[u][pl.ds(row0, size), :] - m_col)
            l_acc = l_acc + jnp.sum(fold(p), axis=0)
            p_refs[u][pl.ds(row0, size), :] = p.astype(bf16)
        ot = jnp.dot(vc_ref[...], p_refs[u][pl.ds(0, n_ctx), :], preferred_element_type=f32)
        if has_x_keys:
            ot = ot + jnp.dot(vx_ref[...], p_refs[u][pl.ds(n_ctx, vx_ref.shape[1]), :], preferred_element_type=f32)
        return (ot / jnp.sum(l_acc, axis=0, keepdims=True)).T

    outs = []
    m_prev = scores(0)
    for u in range(1, len(qs)):
        m_next = scores(u)
        outs.append(values(u - 1, m_prev))
        m_prev = m_next
    outs.append(values(len(qs) - 1, m_prev))
    if diff:
        lq1, lk1, lq2, lk2, subln = (r[...] for r in refs)
        lam = (jnp.exp(jnp.sum(lq1 * lk1, axis=-1, keepdims=True))
               - jnp.exp(jnp.sum(lq2 * lk2, axis=-1, keepdims=True)) + lambda_init)
    for c in range(n_col):
        rows = pl.ds(c * tq, tq)
        if diff:
            dlt = outs[c] - lam * outs[n_col + c]
            o_ref[rows, :] = (_rms_g(dlt, subln) * (1.0 - lambda_init)).astype(o_ref.dtype)
        else:
            o_ref[rows, :] = outs[c].astype(o_ref.dtype)


def _attention(qt, kc, vct, kx, vxt, params, n_batch, ctx_len, tq, tk, lambda_init):
    n_heads = SEC // HEAD_W
    dq = qt.shape[0] // n_heads
    lq = qt.shape[1] // n_batch
    nq = lq // tq
    diff = params is not None
    in_specs = [pl.BlockSpec((dq, tq), lambda b, h, i: (h, b * nq + i)),
                pl.BlockSpec((ctx_len, dq), lambda b, h, i: (b, h)),
                pl.BlockSpec((HEAD_W, ctx_len), lambda b, h, i: (h, b))]
    args = [qt, kc, vct]
    n_keys = ctx_len
    if kx is not None:
        lx = kx.shape[0] // n_batch
        n_keys += lx
        in_specs += [pl.BlockSpec((lx, dq), lambda b, h, i: (b, h)),
                     pl.BlockSpec((HEAD_W, lx), lambda b, h, i: (h, b))]
        args += [kx, vxt]
    if diff:
        in_specs += [pl.BlockSpec(p.shape, lambda b, h, i: (0, 0)) for p in params]
        args += list(params)
    unit = min(tq, ATTN_UNIT)
    n_units = (2 if diff else 1) * (tq // unit)
    return pl.pallas_call(
        functools.partial(_attn_kernel, diff=diff, has_x_keys=kx is not None, tk=tk, lambda_init=lambda_init),
        grid=(n_batch, n_heads, nq),
        in_specs=in_specs,
        out_specs=pl.BlockSpec((tq, HEAD_W), lambda b, h, i: (b * nq + i, h)),
        out_shape=jax.ShapeDtypeStruct((qt.shape[1], n_heads * HEAD_W), bf16),
        scratch_shapes=[pltpu.VMEM((n_keys, unit), f32)] * n_units + [pltpu.VMEM((n_keys, unit), bf16)] * n_units,
        compiler_params=_cp("arbitrary", "arbitrary", "arbitrary"),
        name="attention",
    )(*args)


def _s5_kernel(uc_ref, ux_ref, lre_ref, lim_ref, lst_ref, btr_ref, bti_ref, ctr_ref, cti_ref,
               yc_ref, yx_ref, wb_re, wb_im, wc_re, wc_im, a_re, a_im, h_re, h_im, s_re, s_im):
    d = pl.program_id(0)
    c = pl.program_id(1)
    n_batch, t_len, width = uc_ref.shape
    n_state = lre_ref.shape[1]
    half = n_state // 2
    n_groups = width // S5_GROUP
    n_blk = half // LANES

    @pl.when(c == 0)
    def _():
        lre, lim = lre_ref[...], lim_ref[...]
        step = jnp.exp(lst_ref[...])
        mag = jnp.exp(lre * step)
        are = mag * jnp.cos(lim * step)
        aim = mag * jnp.sin(lim * step)
        den = lre * lre + lim * lim
        fre = ((are - 1.0) * lre + aim * lim) / den
        fim = (aim * lre - (are - 1.0) * lim) / den
        row = lax.broadcasted_iota(jnp.int32, (width, n_state), 0) // S5_GROUP
        col = lax.broadcasted_iota(jnp.int32, (width, n_state), 1) // S5_STATE
        blk = row == col
        tile = lambda r: jnp.broadcast_to(r[...][None], (n_groups, S5_GROUP, n_state)).reshape(width, n_state)
        btr, bti = tile(btr_ref), tile(bti_ref)
        wb_re[...] = jnp.where(blk, fre * btr - fim * bti, 0.0).astype(bf16)
        wb_im[...] = jnp.where(blk, fre * bti + fim * btr, 0.0).astype(bf16)
        wc_re[...] = jnp.where(blk, tile(ctr_ref), 0.0).astype(bf16)
        wc_im[...] = jnp.where(blk, -tile(cti_ref), 0.0).astype(bf16)
        for hf in range(2):
            rows = slice(hf * n_batch, (hf + 1) * n_batch)
            for j in range(n_blk):
                cols = slice(hf * half + j * LANES, hf * half + (j + 1) * LANES)
                a_re[j, rows, :] = jnp.broadcast_to(are[:, cols], (n_batch, LANES))
                a_im[j, rows, :] = jnp.broadcast_to(aim[:, cols], (n_batch, LANES))
        h_re[...] = jnp.zeros_like(h_re)
        h_im[...] = jnp.zeros_like(h_im)

    pitch = s_re.shape[1] // (2 * n_batch)
    kb = LANES
    sb = kb // S5_GROUP * S5_STATE
    nt = (((1,), (1,)), ((), ()))

    def slab(col0, b):
        hf, rem = divmod(col0, half)
        return rem // LANES, pl.ds((hf * n_batch + b) * pitch, t_len)

    def run(u_ref, y_ref):
        for b in range(n_batch):
            u = u_ref[b]
            for bb in range(width // kb):
                ub = u[:, bb * kb:(bb + 1) * kb]
                bre = jnp.dot(ub, wb_re[bb * kb:(bb + 1) * kb, bb * sb:(bb + 1) * sb], preferred_element_type=f32)
                bim = jnp.dot(ub, wb_im[bb * kb:(bb + 1) * kb, bb * sb:(bb + 1) * sb], preferred_element_type=f32)
                for jj in range(sb // LANES):
                    j, rows = slab(bb * sb + jj * LANES, b)
                    s_re[j, rows, :] = bre[:, jj * LANES:(jj + 1) * LANES]
                    s_im[j, rows, :] = bim[:, jj * LANES:(jj + 1) * LANES]

        ar = [a_re[j] for j in range(n_blk)]
        ai = [a_im[j] for j in range(n_blk)]

        def step(k, carry):
            hr, hi = carry
            t = k + d * (t_len - 1 - 2 * k)
            rows = pl.ds(t, 2 * n_batch, stride=pitch)
            nr, ni = [], []
            for j in range(n_blk):
                nr.append(ar[j] * hr[j] - ai[j] * hi[j] + s_re[j, rows, :])
                ni.append(ar[j] * hi[j] + ai[j] * hr[j] + s_im[j, rows, :])
                s_re[j, rows, :] = nr[j]
                s_im[j, rows, :] = ni[j]
            return nr, ni

        init = ([h_re[j] for j in range(n_blk)], [h_im[j] for j in range(n_blk)])
        hr, hi = lax.fori_loop(0, t_len, step, init)
        for j in range(n_blk):
            h_re[j] = hr[j]
            h_im[j] = hi[j]

        for b in range(n_batch):
            ys = []
            for bb in range(width // kb):
                slabs = [slab(bb * sb + jj * LANES, b) for jj in range(sb // LANES)]
                sr = jnp.concatenate([s_re[j, rows, :] for j, rows in slabs], axis=1).astype(bf16)
                si = jnp.concatenate([s_im[j, rows, :] for j, rows in slabs], axis=1).astype(bf16)
                ys.append(
                    lax.dot_general(sr, wc_re[bb * kb:(bb + 1) * kb, bb * sb:(bb + 1) * sb], nt,
                                    preferred_element_type=f32)
                    + lax.dot_general(si, wc_im[bb * kb:(bb + 1) * kb, bb * sb:(bb + 1) * sb], nt,
                                      preferred_element_type=f32))
            y_ref[b] = jnp.concatenate(ys, axis=1)

    @pl.when(c == 0)
    def _():
        run(uc_ref, yc_ref)

    @pl.when(c > 0)
    def _():
        run(ux_ref, yx_ref)


def _s5_scan(px_c, px_x, lam_re, lam_im, log_step, bt_re, bt_im, ct_re, ct_im, n_batch):
    t_len = px_c.shape[1]
    seq = px_x.shape[1]
    assert seq % t_len == 0
    nch = seq // t_len
    n_state = lam_re.shape[-1]
    half = n_state // 2
    pitch = t_len + SUBLANES
    assert t_len % (2 * SUBLANES) == 0

    def x_chunk(d, c):
        return jnp.where(d == 0, jnp.maximum(c - 1, 0), jnp.minimum(nch - c, nch - 1))

    vec = pl.BlockSpec((None, 1, n_state), lambda d, c: (d, 0, 0))
    mat = pl.BlockSpec((None, S5_GROUP, n_state), lambda d, c: (d, 0, 0))
    return pl.pallas_call(
        _s5_kernel,
        grid=(2, nch + 1),
        in_specs=[pl.BlockSpec((n_batch, t_len, SEC), lambda d, c: (0, 0, 3)),
                  pl.BlockSpec((n_batch, t_len, SEC), lambda d, c: (0, x_chunk(d, c), 3)),
                  vec, vec, vec, mat, mat, mat, mat],
        out_specs=[pl.BlockSpec((None, n_batch, t_len, SEC), lambda d, c: (d, 0, 0, 0)),
                   pl.BlockSpec((None, n_batch, t_len, SEC), lambda d, c: (d, 0, x_chunk(d, c), 0))],
        out_shape=[jax.ShapeDtypeStruct((2, n_batch, t_len, SEC), f32),
                   jax.ShapeDtypeStruct((2, n_batch, seq, SEC), f32)],
        scratch_shapes=[pltpu.VMEM((SEC, n_state), bf16)] * 4
        + [pltpu.VMEM((half // LANES, 2 * n_batch, LANES), f32)] * 4
        + [pltpu.VMEM((half // LANES, 2 * n_batch * pitch, LANES), f32)] * 2,
        compiler_params=_cp("arbitrary", "arbitrary"),
        name="s5_scan",
    )(px_c, px_x, lam_re, lam_im, log_step, bt_re, bt_im, ct_re, ct_im)


def _s5_glu_kernel(yf_ref, yb_ref, u_ref, d_ref, w_ref, o_ref):
    y = d_ref[...] * u_ref[...].astype(f32) + yf_ref[...] + yb_ref[...]
    g = jax.nn.gelu(y)
    z = jnp.dot(g.astype(bf16), w_ref[...], preferred_element_type=f32)
    o_ref[...] = (g * jax.nn.sigmoid(z)).astype(o_ref.dtype)


def _s5_glu(y, px, s5_d, w_glu, tm):
    rows = px.shape[0]
    return pl.pallas_call(
        _s5_glu_kernel,
        grid=(rows // tm,),
        in_specs=[pl.BlockSpec((None, tm, SEC), lambda i: (0, i, 0)),
                  pl.BlockSpec((None, tm, SEC), lambda i: (1, i, 0)),
                  pl.BlockSpec((tm, SEC), lambda i: (i, 3)),
                  pl.BlockSpec((1, SEC), lambda i: (0, 0)),
                  pl.BlockSpec((SEC, SEC), lambda i: (0, 0))],
        out_specs=pl.BlockSpec((tm, SEC), lambda i: (i, 0)),
        out_shape=jax.ShapeDtypeStruct((rows, SEC), bf16),
        compiler_params=_cp("arbitrary"),
        name="s5_glu",
    )(y, y, px, s5_d, w_glu)


CONV_HALO = 16


def _conv_kernel(vp_ref, vc_ref, vn_ref, gp_ref, gc_ref, gn_ref, w_ref, b_ref, lg_ref, lb_ref, o_ref, *ext):
    j = pl.program_id(1)
    tc = vc_ref.shape[0]
    glu = lambda v, g: v[...].astype(f32) * jax.nn.sigmoid(g[...].astype(f32))
    ext[0][pl.ds(0, CONV_HALO), :] = jnp.where(j > 0, glu(vp_ref, gp_ref), 0.0)
    ext[0][pl.ds(CONV_HALO, tc), :] = glu(vc_ref, gc_ref)
    ext[0][pl.ds(CONV_HALO + tc, CONV_HALO), :] = jnp.where(j < pl.num_programs(1) - 1, glu(vn_ref, gn_ref), 0.0)
    n_keep = tc + 2 * CONV_HALO - SUBLANES
    for s in range(1, SUBLANES):
        ext[s][pl.ds(0, n_keep), :] = ext[0][pl.ds(s, n_keep), :]
    y = jnp.zeros((tc, vc_ref.shape[1]), f32)
    for k in range(CONV_K):
        off = CONV_HALO - CONV_K // 2 + k
        y += w_ref[pl.ds(k, 1), :] * ext[off % SUBLANES][pl.ds(off - off % SUBLANES, tc), :]
    y = y + b_ref[...]
    mu = jnp.mean(y, axis=-1, keepdims=True)
    var = jnp.mean(jnp.square(y - mu), axis=-1, keepdims=True)
    z = (y - mu) * lax.rsqrt(var + NORM_EPS) * lg_ref[...] + lb_ref[...]
    o_ref[...] = jax.nn.silu(z).astype(o_ref.dtype)


def _conv(px, conv_w, conv_b, ln_g, ln_b, n_batch, tc):
    rows = px.shape[0]
    seq = rows // n_batch
    nt = seq // tc
    hpt = tc // CONV_HALO
    cur = lambda s: pl.BlockSpec((tc, SEC), lambda b, j: (b * nt + j, s))
    prev = lambda s: pl.BlockSpec((CONV_HALO, SEC), lambda b, j: (jnp.maximum((b * nt + j) * hpt - 1, 0), s))
    nxt = lambda s: pl.BlockSpec(
        (CONV_HALO, SEC), lambda b, j: (jnp.minimum((b * nt + j + 1) * hpt, rows // CONV_HALO - 1), s))
    full = lambda a: pl.BlockSpec(a.shape, lambda b, j: (0,) * a.ndim)
    return pl.pallas_call(
        _conv_kernel,
        grid=(n_batch, nt),
        in_specs=[prev(6), cur(6), nxt(6), prev(7), cur(7), nxt(7),
                  full(conv_w), full(conv_b), full(ln_g), full(ln_b)],
        out_specs=pl.BlockSpec((tc, SEC), lambda b, j: (b * nt + j, 0)),
        out_shape=jax.ShapeDtypeStruct((rows, SEC), bf16),
        scratch_shapes=[pltpu.VMEM((tc + 2 * CONV_HALO, SEC), f32)] * SUBLANES,
        compiler_params=_cp("arbitrary", "arbitrary"),
        name="conv",
    )(px, px, px, px, px, px, conv_w, conv_b, ln_g, ln_b)


def kernel(x, c, ctx, c_ctx, w_ada, b_ada, norm_pre_mix, norm_post_mix, norm_pre_ffn, norm_post_ffn, w_in, w_out, da_lam_q1, da_lam_k1, da_lam_q2, da_lam_k2, da_subln, s5_lam_re, s5_lam_im, s5_log_step, s5_b_re, s5_b_im, s5_c_re, s5_c_im, s5_d, s5_w_glu, mla_q_norm, mla_kv_norm, mla_w_uq, mla_w_ukv, conv_w, conv_b, conv_ln_g, conv_ln_b, w_ffn_in, w_ffn_out):
    n_batch, seq, d = x.shape
    ctx_len = ctx.shape[1]
    depth = w_in.shape[0]
    n_state = s5_lam_re.shape[2] * s5_lam_re.shape[3]
    mla_end = 4 * SEC + SEC + MLA_KV_RANK + ROT_DIM
    assert d == 4 * SEC and w_in.shape[2] == mla_end + 2 * SEC

    da_scale = ROT_DIM ** -0.5 * math.log2(math.e)
    mla_scale = (HEAD_W + ROT_DIM) ** -0.5 * math.log2(math.e)

    tm_x = _row_tile(seq, 1024)
    tm_c = _row_tile(n_batch * ctx_len, 1024)
    tpb = seq // tm_x
    grp_x = lambda i: i // tpb
    grp_c = lambda i: n_batch
    tq_da = _row_tile(seq, 512)
    tq_mla = _row_tile(seq, 1024)
    tk = _row_tile(seq, 512)

    n_grp = -(-(n_batch + 1) // SUBLANES) * SUBLANES
    cc = jnp.concatenate([c, c_ctx[None], jnp.zeros((n_grp - n_batch - 1, d), f32)], axis=0)
    mods = _ada(cc, w_ada, b_ada).reshape(depth, n_grp, 1, 6 * d)

    cos_x, sin_x = _rope_tables(seq)
    cos_c, sin_c = jnp.ones((tm_c, LANES), f32), jnp.zeros((tm_c, LANES), f32)

    w_in_p = jnp.concatenate([w_in[:, :, :mla_end], jnp.zeros((depth, d, N_SEC * SEC - w_in.shape[2]), f32),
                              w_in[:, :, mla_end:]], axis=2).astype(bf16)
    w_out_b = w_out.astype(bf16)
    w_ffn_in_b = w_ffn_in.astype(bf16)
    w_ffn_out_b = w_ffn_out.astype(bf16)
    w_glu_b = s5_w_glu.astype(bf16)
    n_mla = SEC // HEAD_W
    wq = mla_w_uq.reshape(depth, SEC, n_mla, HEAD_W + ROT_DIM)
    wq = jnp.pad(wq, ((0, 0), (0, 0), (0, 0), (0, MLA_QK - HEAD_W - ROT_DIM))).reshape(depth, SEC, n_mla * MLA_QK)
    wq = wq.astype(bf16)
    wkv = mla_w_ukv.reshape(depth, MLA_KV_RANK, n_mla, 2, HEAD_W).transpose(0, 1, 3, 2, 4)
    wkv = wkv.reshape(depth, MLA_KV_RANK, 2 * n_mla * HEAD_W).astype(bf16)

    flat = lambda a: a.reshape(depth, 2, 1, n_state)
    s5_lre, s5_lim = flat(s5_lam_re), flat(s5_lam_im)
    s5_lst = flat(jnp.broadcast_to(s5_log_step[..., None], s5_lam_re.shape))
    s5_btr = s5_b_re.transpose(0, 1, 4, 2, 3).reshape(depth, 2, S5_GROUP, n_state)
    s5_bti = s5_b_im.transpose(0, 1, 4, 2, 3).reshape(depth, 2, S5_GROUP, n_state)
    s5_ctr = s5_c_re.transpose(0, 1, 3, 2, 4).reshape(depth, 2, S5_GROUP, n_state)
    s5_cti = s5_c_im.transpose(0, 1, 3, 2, 4).reshape(depth, 2, S5_GROUP, n_state)

    row = lambda a, l: a[l][None]
    conv_w_p = jnp.pad(conv_w, ((0, 0), (0, 1), (0, 0)))

    xs = x.reshape(n_batch * seq, d)
    cs = ctx.reshape(n_batch * ctx_len, d)
    for l in range(depth):
        last = l == depth - 1
        lambda_init = 0.8 - 0.6 * math.exp(-0.3 * l)
        g_pre = row(norm_pre_mix, l)

        px = _in_proj(xs, g_pre, mods, l, grp_x, w_in_p, tm_x, 4 * SEC)
        pc = _in_proj(cs, g_pre, mods, l, grp_c, w_in_p, tm_c, 4 * SEC)

        qx, kx, vx = _da_prep(px, cos_x, sin_x, tpb, tm_x, da_scale)
        qc, kc, vc = _da_prep(pc, cos_c, sin_c, 1, tm_c, da_scale)
        da_params = (row(da_lam_q1, l), row(da_lam_k1, l), row(da_lam_q2, l), row(da_lam_k2, l), row(da_subln, l))
        y_da_x = _attention(qx, kc, vc, kx, vx, da_params, n_batch, ctx_len, tq_da, tk, lambda_init)

        mla_w = (row(mla_q_norm, l), row(mla_kv_norm, l), wq[l], wkv[l])
        mqx, mkx, mvx = _mla_prep(px, *mla_w, cos_x, sin_x, tpb, tm_x, mla_scale)
        mqc, mkc, mvc = _mla_prep(pc, *mla_w, cos_c, sin_c, 1, tm_c, mla_scale)
        y_mla_x = _attention(mqx, mkc, mvc, mkx, mvx, None, n_batch, ctx_len, tq_mla, tk, lambda_init)

        ys_c, ys_x = _s5_scan(pc.reshape(n_batch, ctx_len, -1), px.reshape(n_batch, seq, -1),
                              s5_lre[l], s5_lim[l], s5_lst[l], s5_btr[l], s5_bti[l], s5_ctr[l], s5_cti[l], n_batch)
        y_s5_x = _s5_glu(ys_x.reshape(2, n_batch * seq, SEC), px, row(s5_d, l), w_glu_b[l], tm_x)

        cv = (conv_w_p[l], row(conv_b, l), row(conv_ln_g, l), row(conv_ln_b, l))
        y_cv_x = _conv(px, *cv, n_batch, _row_tile(seq, 512))

        g_post, g_pre_f, g_post_f = row(norm_post_mix, l), row(norm_pre_ffn, l), row(norm_post_ffn, l)
        tm_o = _row_tile(tm_x, 512)
        x1 = _mix_out((y_da_x, y_s5_x, y_mla_x, y_cv_x), w_out_b, xs, g_post, mods, l,
                      lambda i: i // (seq // tm_o), tm_o)
        xs = _ffn(x1, g_pre_f, g_post_f, mods, l, lambda i: i // (seq // tm_o), w_ffn_in_b, w_ffn_out_b, tm_o, SEC)

        if not last:
            y_da_c = _attention(qc, kc, vc, None, None, da_params, n_batch, ctx_len, ctx_len, tk, lambda_init)
            y_mla_c = _attention(mqc, mkc, mvc, None, None, None, n_batch, ctx_len, ctx_len, tk, lambda_init)
            y_s5_c = _s5_glu(ys_c.reshape(2, n_batch * ctx_len, SEC), pc, row(s5_d, l), w_glu_b[l], tm_c)
            y_cv_c = _conv(pc, *cv, n_batch, ctx_len)
            tm_oc = _row_tile(tm_c, 512)
            c1 = _mix_out((y_da_c, y_s5_c, y_mla_c, y_cv_c), w_out_b, cs, g_post, mods, l, grp_c, tm_oc)
            cs = _ffn(c1, g_pre_f, g_post_f, mods, l, grp_c, w_ffn_in_b, w_ffn_out_b, tm_oc, SEC)
    return xs.reshape(n_batch, seq, d)
```
